```python
import math
import jax, jax.numpy as jnp
from jax import lax
import numpy as np

D_MODEL = 2048
BATCH = 4
SEQ = 2048
DEPTH = 2
DEC_BATCH = 128
DEC_SEQ = 1
PAST_LEN = 16384
PAGE_SIZE = 128

S5_WIDTH = D_MODEL // 2
S5_GROUP = 16
S5_GROUPS = S5_WIDTH // S5_GROUP
S5_STATE = 64
GLA_HEADS = 4
GLA_DK = D_MODEL // 16
GLA_DV = D_MODEL // 8
GLA_RANK = 16
GLA_GATE_NORM = 16.0
RET_HEADS = 8
RET_DK = D_MODEL // 16
RET_DV = D_MODEL // 16
ROPE_BASE = 10000.0
CHUNK = 64
N_BRANCH = 3
D_FF = -(-8 * D_MODEL // (3 * 256)) * 256
EPS = 1e-6
SPLITS = (S5_WIDTH,
          GLA_HEADS * GLA_DK, GLA_HEADS * GLA_DK, GLA_HEADS * GLA_DV, GLA_HEADS * GLA_DV, GLA_RANK,
          RET_HEADS * RET_DK, RET_HEADS * RET_DK, RET_HEADS * RET_DV, RET_HEADS * RET_DV,
          N_BRANCH * D_MODEL)
W_IN = sum(SPLITS)

kernel_name = 'hybrid_s5_gla_retnet_step'

F32 = jnp.float32


def rmsnorm(x, gain):
    xf = x.astype(F32)
    y = xf * lax.rsqrt(jnp.mean(xf * xf, axis=-1, keepdims=True) + EPS)
    return (y * gain.astype(F32)).astype(x.dtype)


def rotary(x, pos):
    half = x.shape[-1] // 2
    inv = 1.0 / (ROPE_BASE ** jnp.linspace(0.0, 1.0, half, dtype=F32))
    ang = pos[:, None] * inv[None, :]
    cos = jnp.cos(ang)[None, :, None, :]
    sin = jnp.sin(ang)[None, :, None, :]
    xp = x.reshape(x.shape[:-1] + (half, 2))
    xe, xo = xp[..., 0], xp[..., 1]
    return jnp.stack([xe * cos - xo * sin, xo * cos + xe * sin], axis=-1).reshape(x.shape)


def _complex_affine_combine(e1, e2):
    a1r, a1i, b1r, b1i = e1
    a2r, a2i, b2r, b2i = e2
    return (a2r * a1r - a2i * a1i, a2r * a1i + a2i * a1r,
            a2r * b1r - a2i * b1i + b2r, a2r * b1i + a2i * b1r + b2i)


def s5_scan(u, x0_re, x0_im, a_re, a_im, log_dt, b_re, b_im, c_re, c_im, d_skip):
    bsz, length, _ = u.shape
    ug = u.reshape(bsz, length, S5_GROUPS, S5_GROUP)
    dt = jnp.exp(log_dt.astype(F32))[:, None]
    ar, ai = a_re.astype(F32), a_im.astype(F32)
    mag = jnp.exp(ar * dt)
    abr, abi = mag * jnp.cos(ai * dt), mag * jnp.sin(ai * dt)
    den = ar * ar + ai * ai
    nr, ni = abr - 1.0, abi
    fr, fi = (nr * ar + ni * ai) / den, (ni * ar - nr * ai) / den
    br, bi = b_re.astype(F32), b_im.astype(F32)
    bbr = fr[..., None] * br - fi[..., None] * bi
    bbi = fr[..., None] * bi + fi[..., None] * br
    xr_in = jnp.einsum('blgn,gpn->blgp', ug, bbr)
    xi_in = jnp.einsum('blgn,gpn->blgp', ug, bbi)
    x0r, x0i = x0_re.astype(F32), x0_im.astype(F32)
    xr_in = xr_in.at[:, 0].add(abr * x0r - abi * x0i)
    xi_in = xi_in.at[:, 0].add(abr * x0i + abi * x0r)
    a_r = jnp.broadcast_to(abr, xr_in.shape)
    a_i = jnp.broadcast_to(abi, xi_in.shape)
    _, _, xr, xi = lax.associative_scan(_complex_affine_combine, (a_r, a_i, xr_in, xi_in), axis=1)
    y = (jnp.einsum('blgp,gnp->blgn', xr, c_re.astype(F32))
         - jnp.einsum('blgp,gnp->blgn', xi, c_im.astype(F32))
         + d_skip.astype(F32).reshape(S5_GROUPS, S5_GROUP) * ug)
    return y.reshape(bsz, length, S5_WIDTH), xr[:, -1], xi[:, -1]


def gated_linear_recurrence(q, k, v, g, s0):
    bsz, length, heads, dk = q.shape
    dv = v.shape[-1]
    c = CHUNK if length % CHUNK == 0 else length
    n = length // c

    def to_chunks(a):
        return a.reshape(bsz, n, c, heads, a.shape[-1]).transpose(1, 0, 3, 2, 4)

    mask = jnp.tril(jnp.ones((c, c), dtype=bool))[:, :, None]

    def step(s, inp):
        qi, ki, vi, gi = inp
        b = jnp.cumsum(gi, axis=2)
        diff = b[:, :, :, None, :] - b[:, :, None, :, :]
        decay = jnp.exp(jnp.where(mask, diff, -jnp.inf))
        scores = jnp.einsum('bhtk,bhsk,bhtsk->bhts', qi, ki, decay)
        o = (jnp.einsum('bhts,bhsv->bhtv', scores, vi)
             + jnp.einsum('bhtk,bhkv->bhtv', qi * jnp.exp(b), s))
        b_last = b[:, :, -1:, :]
        s_new = (jnp.exp(b_last[:, :, 0, :])[..., None] * s
                 + jnp.einsum('bhsk,bhsv->bhkv', ki * jnp.exp(b_last - b), vi))
        return s_new, o

    s_fin, o = lax.scan(step, s0.astype(F32), tuple(map(to_chunks, (q, k, v, g))))
    o = o.transpose(1, 0, 3, 2, 4).reshape(bsz, length, heads, dv)
    return o, s_fin


def trunk_layer(x, pos, st_s5_re, st_s5_im, st_gla, st_ret,
                norm_mix, w_in, s5_a_re, s5_a_im, s5_log_dt, s5_b_re, s5_b_im, s5_c_re, s5_c_im,
                s5_d, s5_w_glu, s5_b_glu, s5_w_out, gla_w_gate, gla_b_gate, gla_norm, gla_w_out,
                ret_w_out, w_mix_out, norm_ffn, w_ffn_gate, w_ffn_up, w_ffn_down):
    bsz, length, _ = x.shape
    h = rmsnorm(x, norm_mix)
    proj = jnp.einsum('bld,de->ble', h, w_in).astype(F32)
    cuts = np.cumsum(SPLITS)[:-1].tolist()
    u, gq, gk, gv, gr, glr, rq, rk, rv, rg, mg = jnp.split(proj, cuts, axis=-1)

    y5, new_s5_re, new_s5_im = s5_scan(u, st_s5_re, st_s5_im, s5_a_re, s5_a_im, s5_log_dt,
                                       s5_b_re, s5_b_im, s5_c_re, s5_c_im, s5_d)
    y5 = jax.nn.gelu(y5)
    y5 = y5 * jax.nn.sigmoid(jnp.einsum('blc,ce->ble', y5, s5_w_glu.astype(F32)) + s5_b_glu.astype(F32))
    br_s5 = jnp.einsum('blc,cd->bld', y5, s5_w_out.astype(F32))

    q = gq.reshape(bsz, length, GLA_HEADS, GLA_DK) * (GLA_DK ** -0.5)
    k = gk.reshape(bsz, length, GLA_HEADS, GLA_DK)
    v = gv.reshape(bsz, length, GLA_HEADS, GLA_DV)
    logit = jnp.einsum('blr,re->ble', glr, gla_w_gate.astype(F32)) + gla_b_gate.astype(F32)
    g = (jax.nn.log_sigmoid(logit) / GLA_GATE_NORM).reshape(bsz, length, GLA_HEADS, GLA_DK)
    o, new_gla = gated_linear_recurrence(q, k, v, g, st_gla)
    o = o * lax.rsqrt(jnp.mean(o * o, axis=-1, keepdims=True) + EPS) * gla_norm.astype(F32)
    o = o.reshape(bsz, length, GLA_HEADS * GLA_DV) * jax.nn.silu(gr)
    br_gla = jnp.einsum('blc,cd->bld', o, gla_w_out.astype(F32))

    q = rotary(rq.reshape(bsz, length, RET_HEADS, RET_DK), pos)
    k = rotary(rk.reshape(bsz, length, RET_HEADS, RET_DK), pos) * (RET_DK ** -0.5)
    v = rv.reshape(bsz, length, RET_HEADS, RET_DV)
    log_gamma = jnp.log1p(-jnp.power(2.0, -5.0 - jnp.arange(RET_HEADS, dtype=F32)))
    gdec = jnp.broadcast_to(log_gamma[None, None, :, None], (bsz, length, RET_HEADS, RET_DK))
    o, new_ret = gated_linear_recurrence(q, k, v, gdec, st_ret)
    mu = jnp.mean(o, axis=-1, keepdims=True)
    o = (o - mu) * lax.rsqrt(jnp.mean(jnp.square(o - mu), axis=-1, keepdims=True) + EPS)
    o = o.reshape(bsz, length, RET_HEADS * RET_DV) * jax.nn.silu(rg)
    br_ret = jnp.einsum('blc,cd->bld', o, ret_w_out.astype(F32))

    gates = jax.nn.sigmoid(mg).reshape(bsz, length, N_BRANCH, D_MODEL)
    merged = gates[:, :, 0] * br_s5 + gates[:, :, 1] * br_gla + gates[:, :, 2] * br_ret
    x = x + jnp.einsum('bld,de->ble', merged, w_mix_out.astype(F32)).astype(x.dtype)

    h2 = rmsnorm(x, norm_ffn)
    ff = jax.nn.silu(jnp.einsum('bld,df->blf', h2, w_ffn_gate)) * jnp.einsum('bld,df->blf', h2, w_ffn_up)
    x = x + jnp.einsum('blf,fd->bld', ff, w_ffn_down).astype(x.dtype)
    dt = x.dtype
    return (x, new_s5_re.astype(dt), new_s5_im.astype(dt), new_gla.astype(dt), new_ret.astype(dt))


def setup_inputs(seed: int = 0) -> dict:
    key = jax.random.key(seed)
    ks = jax.random.split(key, 30)

    def nrm(i, shape, scale):
        return jax.random.normal(ks[i], shape, F32) * scale

    glu_cols = GLA_HEADS * GLA_DK
    return {
        'x_prompt': nrm(0, (BATCH, SEQ, D_MODEL), 1.0),
        'x_sample': nrm(1, (DEC_BATCH, DEC_SEQ, D_MODEL), 1.0),
        'state_s5_re': nrm(2, (DEPTH, DEC_BATCH, S5_GROUPS, S5_STATE), 0.5),
        'state_s5_im': nrm(3, (DEPTH, DEC_BATCH, S5_GROUPS, S5_STATE), 0.5),
        'state_gla': nrm(4, (DEPTH, DEC_BATCH, GLA_HEADS, GLA_DK, GLA_DV), 1.0),
        'state_ret': nrm(5, (DEPTH, DEC_BATCH, RET_HEADS, RET_DK, RET_DV), 1.0),
        'norm_mix': 1.0 + nrm(6, (DEPTH, D_MODEL), 0.02),
        'w_in': nrm(7, (DEPTH, D_MODEL, W_IN), D_MODEL ** -0.5),
        's5_a_re': -0.5 + nrm(8, (DEPTH, S5_GROUPS, S5_STATE), 0.01),
        's5_a_im': jnp.pi * jnp.arange(S5_STATE, dtype=F32) + nrm(9, (DEPTH, S5_GROUPS, S5_STATE), 0.01),
        's5_log_dt': jax.random.uniform(ks[10], (DEPTH, S5_GROUPS), F32, minval=math.log(1e-3), maxval=math.log(1e-1)),
        's5_b_re': nrm(11, (DEPTH, S5_GROUPS, S5_STATE, S5_GROUP), (2 * S5_GROUP) ** -0.5),
        's5_b_im': nrm(12, (DEPTH, S5_GROUPS, S5_STATE, S5_GROUP), (2 * S5_GROUP) ** -0.5),
        's5_c_re': nrm(13, (DEPTH, S5_GROUPS, S5_GROUP, S5_STATE), (2 * S5_STATE) ** -0.5),
        's5_c_im': nrm(14, (DEPTH, S5_GROUPS, S5_GROUP, S5_STATE), (2 * S5_STATE) ** -0.5),
        's5_d': nrm(15, (DEPTH, S5_WIDTH), 1.0),
        's5_w_glu': nrm(16, (DEPTH, S5_WIDTH, S5_WIDTH), S5_WIDTH ** -0.5),
        's5_b_glu': nrm(17, (DEPTH, S5_WIDTH), 0.01),
        's5_w_out': nrm(18, (DEPTH, S5_WIDTH, D_MODEL), S5_WIDTH ** -0.5),
        'gla_w_gate': nrm(19, (DEPTH, GLA_RANK, glu_cols), GLA_RANK ** -0.5),
        'gla_b_gate': nrm(20, (DEPTH, glu_cols), 0.01),
        'gla_norm': 1.0 + nrm(21, (DEPTH, GLA_DV), 0.02),
        'gla_w_out': nrm(22, (DEPTH, GLA_HEADS * GLA_DV, D_MODEL), (GLA_HEADS * GLA_DV) ** -0.5),
        'ret_w_out': nrm(23, (DEPTH, RET_HEADS * RET_DV, D_MODEL), (RET_HEADS * RET_DV) ** -0.5),
        'w_mix_out': nrm(24, (DEPTH, D_MODEL, D_MODEL), D_MODEL ** -0.5),
        'norm_ffn': 1.0 + nrm(25, (DEPTH, D_MODEL), 0.02),
        'w_ffn_gate': nrm(26, (DEPTH, D_MODEL, D_FF), D_MODEL ** -0.5),
        'w_ffn_up': nrm(27, (DEPTH, D_MODEL, D_FF), D_MODEL ** -0.5),
        'w_ffn_down': nrm(28, (DEPTH, D_FF, D_MODEL), D_FF ** -0.5),
        'norm_final': 1.0 + nrm(29, (D_MODEL,), 0.02),
    }


def reference(x_prompt, x_sample, state_s5_re, state_s5_im, state_gla, state_ret,
              norm_mix, w_in, s5_a_re, s5_a_im, s5_log_dt, s5_b_re, s5_b_im, s5_c_re, s5_c_im,
              s5_d, s5_w_glu, s5_b_glu, s5_w_out, gla_w_gate, gla_b_gate, gla_norm, gla_w_out,
              ret_w_out, w_mix_out, norm_ffn, w_ffn_gate, w_ffn_up, w_ffn_down, norm_final):
    bp, lp, _ = x_prompt.shape
    pos_prompt = jnp.arange(lp, dtype=F32)
    pos_sample = PAST_LEN + jnp.arange(x_sample.shape[1], dtype=F32)
    zero_s5 = jnp.zeros((bp, S5_GROUPS, S5_STATE), F32)
    zero_gla = jnp.zeros((bp, GLA_HEADS, GLA_DK, GLA_DV), F32)
    zero_ret = jnp.zeros((bp, RET_HEADS, RET_DK, RET_DV), F32)
    hp, hs = x_prompt, x_sample
    p_s5r, p_s5i, p_gla, p_ret = [], [], [], []
    s_s5r, s_s5i, s_gla, s_ret = [], [], [], []
    for l in range(DEPTH):
        lw = (norm_mix[l], w_in[l], s5_a_re[l], s5_a_im[l], s5_log_dt[l], s5_b_re[l], s5_b_im[l],
              s5_c_re[l], s5_c_im[l], s5_d[l], s5_w_glu[l], s5_b_glu[l], s5_w_out[l], gla_w_gate[l],
              gla_b_gate[l], gla_norm[l], gla_w_out[l], ret_w_out[l], w_mix_out[l], norm_ffn[l],
              w_ffn_gate[l], w_ffn_up[l], w_ffn_down[l])
        hp, a, b, c, d = trunk_layer(hp, pos_prompt, zero_s5, zero_s5, zero_gla, zero_ret, *lw)
        p_s5r.append(a); p_s5i.append(b); p_gla.append(c); p_ret.append(d)
        hs, a, b, c, d = trunk_layer(hs, pos_sample, state_s5_re[l], state_s5_im[l], state_gla[l], state_ret[l], *lw)
        s_s5r.append(a); s_s5i.append(b); s_gla.append(c); s_ret.append(d)
    y_prompt = rmsnorm(hp, norm_final)
    y_sample = rmsnorm(hs, norm_final)
    return (y_prompt, y_sample,
            jnp.stack(p_s5r), jnp.stack(p_s5i), jnp.stack(p_gla), jnp.stack(p_ret),
            jnp.stack(s_s5r), jnp.stack(s_s5i), jnp.stack(s_gla), jnp.stack(s_ret))
```

```python
import functools
import math

import jax
import jax.numpy as jnp
import numpy as np
from jax import lax
from jax.experimental import pallas as pl
from jax.experimental.pallas import tpu as pltpu

F32 = jnp.float32
BF16 = jnp.bfloat16
HIGHEST = lax.Precision.HIGHEST

D_MODEL = 2048
S5_WIDTH = D_MODEL // 2
S5_GROUP = 16
S5_GROUPS = S5_WIDTH // S5_GROUP
S5_STATE = 64
S5_CHUNK = 16
GLA_HEADS = 4
GLA_DK = D_MODEL // 16
GLA_DV = D_MODEL // 8
GLA_RANK = 16
GLA_GATE_NORM = 16.0
RET_HEADS = 8
RET_DK = D_MODEL // 16
RET_DV = D_MODEL // 16
ROPE_BASE = 10000.0
PAST_LEN = 16384
N_BRANCH = 3
D_FF = -(-8 * D_MODEL // (3 * 256)) * 256
EPS = 1e-6
CHUNK = 128
LANES = 128
SAMPLE_BLOCK = 8

OFF_U = 0
OFF_GQ = OFF_U + S5_WIDTH
OFF_GK = OFF_GQ + GLA_HEADS * GLA_DK
OFF_GV = OFF_GK + GLA_HEADS * GLA_DK
OFF_GR = OFF_GV + GLA_HEADS * GLA_DV
OFF_RQ = OFF_GR + GLA_HEADS * GLA_DV
OFF_RK = OFF_RQ + RET_HEADS * RET_DK
OFF_RV = OFF_RK + RET_HEADS * RET_DK
OFF_RG = OFF_RV + RET_HEADS * RET_DV
OFF_MG = OFF_RG + RET_HEADS * RET_DV
W_MAIN = OFF_MG + N_BRANCH * D_MODEL
GLR_COL = OFF_RQ

VMEM_LIMIT = 56 * 1024 * 1024

NT = (((1,), (1,)), ((), ()))
TN = (((0,), (0,)), ((), ()))


def _params(n_grid):
    return pltpu.CompilerParams(dimension_semantics=("arbitrary",) * n_grid,
                                vmem_limit_bytes=VMEM_LIMIT)


def _silu(x):
    return x * jax.nn.sigmoid(x)


def _rmsnorm_kernel(x_ref, g_ref, o_ref):
    x = x_ref[...]
    y = x * lax.rsqrt(jnp.mean(x * x, axis=-1, keepdims=True) + EPS)
    o_ref[...] = (y * g_ref[...]).astype(o_ref.dtype)


def rmsnorm(x, gain, out_dtype):
    m, d = x.shape
    tm = min(m, 512)
    return pl.pallas_call(
        _rmsnorm_kernel,
        grid=(m // tm,),
        in_specs=[pl.BlockSpec((tm, d), lambda i: (i, 0)),
                  pl.BlockSpec((1, d), lambda i: (0, 0))],
        out_specs=pl.BlockSpec((tm, d), lambda i: (i, 0)),
        out_shape=jax.ShapeDtypeStruct((m, d), out_dtype),
        compiler_params=_params(1),
        name="rmsnorm",
    )(x, gain.reshape(1, d))


def _stage_weights(pairs):
    @pl.when(pl.program_id(1) == 0)
    def _():
        for w_ref, wb_ref in pairs:
            wb_ref[...] = w_ref[...].astype(BF16)


def _dot(a, b):
    return jnp.dot(a, b, preferred_element_type=F32)


def _in_proj_kernel(a_ref, w_ref, o_ref, wb_ref):
    _stage_weights([(w_ref, wb_ref)])
    o_ref[...] = _dot(a_ref[...], wb_ref[...])


def in_proj(h, w, tm, tn):
    m, k = h.shape
    n = w.shape[1]
    return pl.pallas_call(
        _in_proj_kernel,
        grid=(n // tn, m // tm),
        in_specs=[pl.BlockSpec((tm, k), lambda j, i: (i, 0)),
                  pl.BlockSpec((k, tn), lambda j, i: (0, j))],
        out_specs=pl.BlockSpec((tm, tn), lambda j, i: (i, j)),
        out_shape=jax.ShapeDtypeStruct((m, n), F32),
        scratch_shapes=[pltpu.VMEM((k, tn), BF16)],
        compiler_params=_params(2),
        name="in_proj",
    )(h, w)


def _gla_gate_kernel(h_ref, wl_ref, wg_ref, b_ref, o_ref):
    glr = _dot(h_ref[...], wl_ref[...].astype(BF16))
    logit = _dot(glr.astype(BF16), wg_ref[...].astype(BF16)) + b_ref[...]
    log_sig = jnp.minimum(logit, 0.0) - jnp.log1p(jnp.exp(-jnp.abs(logit)))
    o_ref[...] = log_sig / GLA_GATE_NORM


def gla_gate(h, w_glr, w_gate, b_gate, tm):
    m, k = h.shape
    n = w_gate.shape[1]
    return pl.pallas_call(
        _gla_gate_kernel,
        grid=(m // tm,),
        in_specs=[pl.BlockSpec((tm, k), lambda i: (i, 0)),
                  pl.BlockSpec((k, LANES), lambda i: (0, 0)),
                  pl.BlockSpec((LANES, n), lambda i: (0, 0)),
                  pl.BlockSpec((1, n), lambda i: (0, 0))],
        out_specs=pl.BlockSpec((tm, n), lambda i: (i, 0)),
        out_shape=jax.ShapeDtypeStruct((m, n), F32),
        compiler_params=_params(1),
        name="gla_gate",
    )(h, w_glr, w_gate, b_gate.reshape(1, n))


def _s5_glu_kernel(a_ref, w_ref, y_ref, b_ref, o_ref, wb_ref):
    _stage_weights([(w_ref, wb_ref)])
    acc = _dot(a_ref[...].astype(BF16), wb_ref[...])
    o_ref[...] = (y_ref[...] * jax.nn.sigmoid(acc + b_ref[...])).astype(o_ref.dtype)


def s5_glu(y5, w, b, tm, tn):
    m, k = y5.shape
    n = w.shape[1]
    return pl.pallas_call(
        _s5_glu_kernel,
        grid=(n // tn, m // tm),
        in_specs=[pl.BlockSpec((tm, k), lambda j, i: (i, 0)),
                  pl.BlockSpec((k, tn), lambda j, i: (0, j)),
                  pl.BlockSpec((tm, tn), lambda j, i: (i, j)),
                  pl.BlockSpec((1, tn), lambda j, i: (0, j))],
        out_specs=pl.BlockSpec((tm, tn), lambda j, i: (i, j)),
        out_shape=jax.ShapeDtypeStruct((m, n), BF16),
        scratch_shapes=[pltpu.VMEM((k, tn), BF16)],
        compiler_params=_params(2),
        name="s5_glu",
    )(y5, w, y5, b.reshape(1, n))


def _merge_kernel(a5_ref, ag_ref, ar_ref, w5_ref, wg_ref, wr_ref, m5_ref, mg_ref, mr_ref,
                  o_ref, w5b_ref, wgb_ref, wrb_ref):
    _stage_weights([(w5_ref, w5b_ref), (wg_ref, wgb_ref), (wr_ref, wrb_ref)])
    merged = (jax.nn.sigmoid(m5_ref[...]) * _dot(a5_ref[...], w5b_ref[...])
              + jax.nn.sigmoid(mg_ref[...]) * _dot(ag_ref[...], wgb_ref[...])
              + jax.nn.sigmoid(mr_ref[...]) * _dot(ar_ref[...], wrb_ref[...]))
    o_ref[...] = merged.astype(o_ref.dtype)


def merge_branches(a5, ag, ar, w5, wg, wr, proj, tm, tn):
    m, k = a5.shape
    n = w5.shape[1]
    a_spec = pl.BlockSpec((tm, k), lambda j, i: (i, 0))
    w_spec = pl.BlockSpec((k, tn), lambda j, i: (0, j))

    def gate_spec(branch):
        first = (OFF_MG + branch * D_MODEL) // tn
        return pl.BlockSpec((tm, tn), lambda j, i: (i, first + j))

    return pl.pallas_call(
        _merge_kernel,
        grid=(n // tn, m // tm),
        in_specs=[a_spec, a_spec, a_spec, w_spec, w_spec, w_spec,
                  gate_spec(0), gate_spec(1), gate_spec(2)],
        out_specs=pl.BlockSpec((tm, tn), lambda j, i: (i, j)),
        out_shape=jax.ShapeDtypeStruct((m, n), BF16),
        scratch_shapes=[pltpu.VMEM((k, tn), BF16)] * 3,
        compiler_params=_params(2),
        name="merge_branches",
    )(a5, ag, ar, w5, wg, wr, proj, proj, proj)


def _residual_kernel(a_ref, w_ref, x_ref, o_ref, wb_ref):
    _stage_weights([(w_ref, wb_ref)])
    o_ref[...] = x_ref[...] + _dot(a_ref[...], wb_ref[...])


def residual_matmul(a, w, x, tm, tn):
    m, k = a.shape
    n = w.shape[1]
    return pl.pallas_call(
        _residual_kernel,
        grid=(n // tn, m // tm),
        in_specs=[pl.BlockSpec((tm, k), lambda j, i: (i, 0)),
                  pl.BlockSpec((k, tn), lambda j, i: (0, j)),
                  pl.BlockSpec((tm, tn), lambda j, i: (i, j))],
        out_specs=pl.BlockSpec((tm, tn), lambda j, i: (i, j)),
        out_shape=jax.ShapeDtypeStruct((m, n), F32),
        scratch_shapes=[pltpu.VMEM((k, tn), BF16)],
        compiler_params=_params(2),
        name="residual_matmul",
    )(a, w, x)


def _ffn_up_kernel(a_ref, wg_ref, wu_ref, o_ref, wgb_ref, wub_ref):
    _stage_weights([(wg_ref, wgb_ref), (wu_ref, wub_ref)])
    a = a_ref[...]
    o_ref[...] = (_silu(_dot(a, wgb_ref[...])) * _dot(a, wub_ref[...])).astype(o_ref.dtype)


def ffn_up(h, wg, wu, tm, tn):
    m, k = h.shape
    n = wg.shape[1]
    w_spec = pl.BlockSpec((k, tn), lambda j, i: (0, j))
    return pl.pallas_call(
        _ffn_up_kernel,
        grid=(n // tn, m // tm),
        in_specs=[pl.BlockSpec((tm, k), lambda j, i: (i, 0)), w_spec, w_spec],
        out_specs=pl.BlockSpec((tm, tn), lambda j, i: (i, j)),
        out_shape=jax.ShapeDtypeStruct((m, n), BF16),
        scratch_shapes=[pltpu.VMEM((k, tn), BF16)] * 2,
        compiler_params=_params(2),
        name="ffn_up",
    )(h, wg, wu)


def _s5_prep_kernel(ldt_ref, arc_ref, aic_ref, arr_ref, air_ref, b1_ref, crt_ref, cit_ref,
                    t_ref, p_ref, q_ref, bbt_ref, w0_ref):
    s = S5_CHUNK
    dt = jnp.exp(ldt_ref[0])
    ar_row, ai_row = arr_ref[0], air_ref[0]
    lam_r_row, lam_i_row = ar_row * dt, ai_row * dt
    lam_r_col, lam_i_col = arc_ref[0] * dt, aic_ref[0] * dt

    mag = jnp.exp(lam_r_row)
    abr, abi = mag * jnp.cos(lam_i_row), mag * jnp.sin(lam_i_row)
    den = ar_row * ar_row + ai_row * ai_row
    nr = abr - 1.0
    fr = (nr * ar_row + abi * ai_row) / den
    fi = (abi * ar_row - nr * ai_row) / den

    b1 = b1_ref[0]
    lane = lax.broadcasted_iota(jnp.int32, b1.shape, 1)
    b2 = pltpu.roll(b1, S5_STATE, axis=1) * jnp.where(lane < S5_STATE, -1.0, 1.0)
    bbt = fr * b1 + fi * b2
    bbt_sw = fr * b2 - fi * b1
    bbt_ref[0] = bbt

    rows = lax.broadcasted_iota(jnp.int32, (s * S5_GROUP, LANES), 0)
    tau_p = (s - 1 - jnp.right_shift(rows, 4)).astype(F32)
    mag_p = jnp.exp(lam_r_row * tau_p)
    ar_p, ai_p = mag_p * jnp.cos(lam_i_row * tau_p), mag_p * jnp.sin(lam_i_row * tau_p)
    p_ref[0] = (jnp.concatenate([bbt] * s, axis=0) * ar_p
                + jnp.concatenate([bbt_sw] * s, axis=0) * ai_p).astype(p_ref.dtype)

    crt, cit = crt_ref[0], cit_ref[0]
    lanes_t = jnp.right_shift(lax.broadcasted_iota(jnp.int32, crt.shape, 1), 4).astype(F32)

    def coeffs(tau):
        mag_t = jnp.exp(lam_r_col * tau)
        ar_t, ai_t = mag_t * jnp.cos(lam_i_col * tau), mag_t * jnp.sin(lam_i_col * tau)
        return jnp.concatenate([ar_t * crt - ai_t * cit, -(ar_t * cit + ai_t * crt)], axis=0)

    w0 = coeffs(lanes_t)
    w0_ref[0] = w0
    q_ref[0] = coeffs(lanes_t + 1.0).astype(q_ref.dtype)

    t0 = jnp.dot(bbt, w0, precision=HIGHEST, preferred_element_type=F32)
    lane_t = lax.broadcasted_iota(jnp.int32, t0.shape, 1)
    blocks = [t0]
    for i in range(1, s):
        blocks.append(jnp.where(lane_t >= i * S5_GROUP, pltpu.roll(t0, i * S5_GROUP, axis=1), 0.0))
    t_ref[0] = jnp.concatenate(blocks, axis=0).astype(t_ref.dtype)


def s5_prep(a_re, a_im, log_dt, b_re, b_im, c_re, c_im):
    g, p, n, s = S5_GROUPS, S5_STATE, S5_GROUP, S5_CHUNK
    dup = lambda a: jnp.concatenate([a, a], axis=-1)[:, None, :]
    b1 = jnp.concatenate([b_re.transpose(0, 2, 1), b_im.transpose(0, 2, 1)], axis=-1)
    crt = jnp.tile(c_re.transpose(0, 2, 1), (1, 1, s))
    cit = jnp.tile(c_im.transpose(0, 2, 1), (1, 1, s))
    spec = lambda *shape: pl.BlockSpec((1,) + shape, lambda i: (i, 0, 0))
    return pl.pallas_call(
        _s5_prep_kernel,
        grid=(g,),
        in_specs=[spec(1, 1), spec(p, 1), spec(p, 1), spec(1, 2 * p), spec(1, 2 * p),
                  spec(n, 2 * p), spec(p, s * n), spec(p, s * n)],
        out_specs=[spec(s * n, s * n), spec(s * n, 2 * p), spec(2 * p, s * n),
                   spec(n, 2 * p), spec(2 * p, s * n)],
        out_shape=[jax.ShapeDtypeStruct((g, s * n, s * n), BF16),
                   jax.ShapeDtypeStruct((g, s * n, 2 * p), BF16),
                   jax.ShapeDtypeStruct((g, 2 * p, s * n), BF16),
                   jax.ShapeDtypeStruct((g, n, 2 * p), F32),
                   jax.ShapeDtypeStruct((g, 2 * p, s * n), F32)],
        compiler_params=_params(1),
        name="s5_prep",
    )(log_dt[:, None, None], a_re[:, :, None], a_im[:, :, None], dup(a_re), dup(a_im), b1, crt, cit)


def _s5_z_kernel(u_ref, p_ref, z_ref):
    z_ref[0] = _dot(u_ref[0].astype(BF16), p_ref[0])


def _abar_power(ldt_ref, arr_ref, air_ref, power):
    dt = jnp.exp(ldt_ref[...])
    mag = jnp.exp(arr_ref[...] * dt * power)
    ang = air_ref[...] * dt * power
    lane = lax.broadcasted_iota(jnp.int32, mag.shape, 2)
    return mag * jnp.cos(ang), mag * jnp.sin(ang) * jnp.where(lane < S5_STATE, -1.0, 1.0)


def _complex_scale(x, a_r, a_i_signed):
    return a_r * x + a_i_signed * pltpu.roll(x, S5_STATE, axis=x.ndim - 1)


def _s5_scan_kernel(z_ref, ldt_ref, arr_ref, air_ref, x_ref, fin_ref, *, bsz):
    a_r, a_i = _abar_power(ldt_ref, arr_ref, air_ref, float(S5_CHUNK))
    n_chunks = z_ref.shape[1] // bsz

    def step(j, x):
        rows = pl.ds(pl.multiple_of(j * bsz, bsz), bsz)
        x_ref[:, rows, :] = x
        return _complex_scale(x, a_r, a_i) + z_ref[:, rows, :]

    fin_ref[...] = lax.fori_loop(0, n_chunks, step, jnp.zeros(fin_ref.shape, F32))


def _s5_y_kernel(u_ref, x_ref, t_ref, q_ref, d_ref, y_ref):
    u = u_ref[0]
    y = _dot(u.astype(BF16), t_ref[0]) + _dot(x_ref[0].astype(BF16), q_ref[0]) + d_ref[0] * u
    y_ref[0] = jax.nn.gelu(y)


def s5_prompt(u, bsz, length, t_mat, p_mat, q_mat, log_dt, a_re, a_im, d_skip):
    g, n, s, p2 = S5_GROUPS, S5_GROUP, S5_CHUNK, 2 * S5_STATE
    n_chunks = length // s
    rows = n_chunks * bsz
    ug = u.reshape(bsz, n_chunks, s, g, n).transpose(3, 1, 0, 2, 4).reshape(g, rows, s * n)
    gspec = lambda *shape: pl.BlockSpec((1,) + shape, lambda i: (i, 0, 0))
    z = pl.pallas_call(
        _s5_z_kernel,
        grid=(g,),
        in_specs=[gspec(rows, s * n), gspec(s * n, p2)],
        out_specs=gspec(rows, p2),
        out_shape=jax.ShapeDtypeStruct((g, rows, p2), F32),
        compiler_params=_params(1),
        name="s5_z",
    )(ug, p_mat)
    dup = lambda a: jnp.concatenate([a, a], axis=-1)[:, None, :]
    x_start, x_fin = pl.pallas_call(
        functools.partial(_s5_scan_kernel, bsz=bsz),
        out_shape=[jax.ShapeDtypeStruct((g, rows, p2), F32),
                   jax.ShapeDtypeStruct((g, bsz, p2), F32)],
        compiler_params=pltpu.CompilerParams(vmem_limit_bytes=VMEM_LIMIT),
        name="s5_scan",
    )(z, log_dt[:, None, None], dup(a_re), dup(a_im))
    d_tile = jnp.tile(d_skip.reshape(g, 1, n), (1, 1, s))
    y = pl.pallas_call(
        _s5_y_kernel,
        grid=(g,),
        in_specs=[gspec(rows, s * n), gspec(rows, p2), gspec(s * n, s * n), gspec(p2, s * n),
                  gspec(1, s * n)],
        out_specs=gspec(rows, s * n),
        out_shape=jax.ShapeDtypeStruct((g, rows, s * n), F32),
        compiler_params=_params(1),
        name="s5_y",
    )(ug, x_start, t_mat, q_mat, d_tile)
    y = y.reshape(g, n_chunks, bsz, s, n).transpose(2, 1, 3, 0, 4).reshape(bsz * length, g * n)
    return y, x_fin


def _s5_sample_kernel(u_ref, xr_ref, xi_ref, bbt_ref, w0_ref, ldt_ref, arr_ref, air_ref, d_ref,
                      y_ref, nr_ref, ni_ref):
    p = S5_STATE
    dt = jnp.exp(ldt_ref[0])
    mag = jnp.exp(arr_ref[0][:, :p] * dt)
    ang = air_ref[0][:, :p] * dt
    abr, abi = mag * jnp.cos(ang), mag * jnp.sin(ang)
    u = u_ref[0]
    z = _dot(u.astype(BF16), bbt_ref[0].astype(BF16))
    xr, xi = xr_ref[0], xi_ref[0]
    new_r = abr * xr - abi * xi + z[:, :p]
    new_i = abr * xi + abi * xr + z[:, p:]
    nr_ref[0] = new_r
    ni_ref[0] = new_i
    c = w0_ref[0][:, :S5_GROUP].astype(BF16)
    y = _dot(new_r.astype(BF16), c[:p]) + _dot(new_i.astype(BF16), c[p:]) + d_ref[0] * u
    y_ref[0] = jax.nn.gelu(y)


def s5_sample(u, x_re, x_im, bbt, w0, log_dt, a_re, a_im, d_skip):
    g, n, p = S5_GROUPS, S5_GROUP, S5_STATE
    bsz = u.shape[0]
    dup = lambda a: jnp.concatenate([a, a], axis=-1)[:, None, :]
    gspec = lambda *shape: pl.BlockSpec((1,) + shape, lambda i: (i, 0, 0))
    y, new_r, new_i = pl.pallas_call(
        _s5_sample_kernel,
        grid=(g,),
        in_specs=[gspec(bsz, n), gspec(bsz, p), gspec(bsz, p), gspec(n, 2 * p),
                  gspec(2 * p, S5_CHUNK * n), gspec(1, 1), gspec(1, 2 * p), gspec(1, 2 * p),
                  gspec(1, n)],
        out_specs=[gspec(bsz, n), gspec(bsz, p), gspec(bsz, p)],
        out_shape=[jax.ShapeDtypeStruct((g, bsz, n), F32),
                   jax.ShapeDtypeStruct((g, bsz, p), F32),
                   jax.ShapeDtypeStruct((g, bsz, p), F32)],
        compiler_params=_params(1),
        name="s5_sample",
    )(u.reshape(bsz, g, n).transpose(1, 0, 2), x_re.transpose(1, 0, 2), x_im.transpose(1, 0, 2),
      bbt, w0, log_dt[:, None, None], dup(a_re), dup(a_im), d_skip.reshape(g, 1, n))
    return (y.transpose(1, 0, 2).reshape(bsz, g * n), new_r.transpose(1, 0, 2),
            new_i.transpose(1, 0, 2))


def _gla_kernel(q_ref, k_ref, v_ref, gr_ref, g_ref, nrm_ref, o_ref, st_ref, s_ref):
    t = pl.program_id(1)

    @pl.when(t == 0)
    def _():
        s_ref[...] = jnp.zeros(s_ref.shape, F32)

    c = q_ref.shape[0]
    row = lax.broadcasted_iota(jnp.int32, (c, c), 0)
    col = lax.broadcasted_iota(jnp.int32, (c, c), 1)
    causal = col <= row
    ones_tri = causal.astype(F32)
    for h in range(GLA_HEADS):
        ks = slice(h * GLA_DK, (h + 1) * GLA_DK)
        vs = slice(h * GLA_DV, (h + 1) * GLA_DV)
        b = jnp.dot(ones_tri, g_ref[:, ks], precision=HIGHEST, preferred_element_type=F32)
        b_end = b[c - 1:c, :]
        q_in = (q_ref[:, ks] * (GLA_DK ** -0.5) * jnp.exp(b)).astype(BF16)
        k_in = (k_ref[:, ks] * jnp.exp(-b)).astype(BF16)
        k_end = (k_ref[:, ks] * jnp.exp(b_end - b)).astype(BF16)
        v = v_ref[:, vs].astype(BF16)
        scores = jnp.where(causal, lax.dot_general(q_in, k_in, NT, preferred_element_type=F32), 0.0)
        s_t = s_ref[h]
        o = (_dot(scores.astype(BF16), v)
             + lax.dot_general(q_in, s_t.astype(BF16), NT, preferred_element_type=F32))
        s_ref[h] = jnp.exp(b_end) * s_t + lax.dot_general(v, k_end, TN, preferred_element_type=F32)
        o = o * lax.rsqrt(jnp.mean(o * o, axis=-1, keepdims=True) + EPS) * nrm_ref[...]
        o_ref[:, vs] = (o * _silu(gr_ref[:, vs])).astype(o_ref.dtype)

    @pl.when(t == pl.num_programs(1) - 1)
    def _():
        st_ref[0] = s_ref[...]


def gla_prompt(proj, g, gla_norm, bsz, length):
    c = CHUNK
    nc = length // c
    hk, hv = GLA_HEADS * GLA_DK, GLA_HEADS * GLA_DV
    rows = lambda width, off: pl.BlockSpec((c, width), lambda b, t: (b * nc + t, off // width))
    o, st = pl.pallas_call(
        _gla_kernel,
        grid=(bsz, nc),
        in_specs=[rows(hk, OFF_GQ), rows(hk, OFF_GK), rows(hv, OFF_GV), rows(hv, OFF_GR),
                  rows(hk, 0), pl.BlockSpec((1, GLA_DV), lambda b, t: (0, 0))],
        out_specs=[rows(hv, 0),
                   pl.BlockSpec((1, GLA_HEADS, GLA_DV, GLA_DK), lambda b, t: (b, 0, 0, 0))],
        out_shape=[jax.ShapeDtypeStruct((bsz * length, hv), BF16),
                   jax.ShapeDtypeStruct((bsz, GLA_HEADS, GLA_DV, GLA_DK), F32)],
        scratch_shapes=[pltpu.VMEM((GLA_HEADS, GLA_DV, GLA_DK), F32)],
        compiler_params=_params(2),
        name="gla_prompt",
    )(proj, proj, proj, proj, g, gla_norm.reshape(1, GLA_DV))
    return o, st.swapaxes(2, 3)


def _rotate_pairs(x, cos2, sin2):
    lane = lax.broadcasted_iota(jnp.int32, x.shape, 1)
    partner = jnp.where(lane % 2 == 0, pltpu.roll(x, x.shape[1] - 1, axis=1), pltpu.roll(x, 1, axis=1))
    return x * cos2 + partner * sin2


def _ret_kernel(q_ref, k_ref, v_ref, rg_ref, cos_ref, sin_ref, dm_ref, qd_ref, kd_ref, gc_ref,
                o_ref, st_ref, s_ref):
    t = pl.program_id(1)

    @pl.when(t == 0)
    def _():
        s_ref[...] = jnp.zeros(s_ref.shape, F32)

    cos2, sin2 = cos_ref[...], sin_ref[...]
    for h in range(RET_HEADS):
        ks = slice(h * RET_DK, (h + 1) * RET_DK)
        vs = slice(h * RET_DV, (h + 1) * RET_DV)
        q = _rotate_pairs(q_ref[:, ks], cos2, sin2)
        k = _rotate_pairs(k_ref[:, ks], cos2, sin2) * (RET_DK ** -0.5)
        v = v_ref[:, vs].astype(BF16)
        scores = lax.dot_general(q.astype(BF16), k.astype(BF16), NT, preferred_element_type=F32) * dm_ref[h]
        s_t = s_ref[h]
        o = (_dot(scores.astype(BF16), v)
             + lax.dot_general((q * qd_ref[h]).astype(BF16), s_t.astype(BF16), NT,
                               preferred_element_type=F32))
        s_ref[h] = gc_ref[h] * s_t + lax.dot_general(v, (k * kd_ref[h]).astype(BF16), TN,
                                                     preferred_element_type=F32)
        oc = o - jnp.mean(o, axis=-1, keepdims=True)
        o = oc * lax.rsqrt(jnp.mean(oc * oc, axis=-1, keepdims=True) + EPS)
        o_ref[:, vs] = (o * _silu(rg_ref[:, vs])).astype(o_ref.dtype)

    @pl.when(t == pl.num_programs(1) - 1)
    def _():
        st_ref[0] = s_ref[...]


def _rotary_tables(pos):
    half = RET_DK // 2
    inv = 1.0 / (ROPE_BASE ** jnp.linspace(0.0, 1.0, half, dtype=F32))
    ang = pos[:, None] * inv[None, :]
    cos2 = jnp.repeat(jnp.cos(ang), 2, axis=-1)
    sin2 = jnp.stack([-jnp.sin(ang), jnp.sin(ang)], axis=-1).reshape(pos.shape[0], RET_DK)
    return cos2, sin2


def _log_gamma():
    return jnp.log1p(-jnp.power(2.0, -5.0 - jnp.arange(RET_HEADS, dtype=F32)))


def ret_prompt(proj, bsz, length):
    c = CHUNK
    nc = length // c
    hk, hv = RET_HEADS * RET_DK, RET_HEADS * RET_DV
    cos2, sin2 = _rotary_tables(jnp.arange(length, dtype=F32))
    lg = _log_gamma()[:, None, None]
    steps = jnp.arange(c, dtype=F32)
    lag = steps[:, None] - steps[None, :]
    dmat = jnp.where(lag >= 0, jnp.exp(lg * jnp.maximum(lag, 0.0)), 0.0)
    wide = lambda a: jnp.broadcast_to(a, (RET_HEADS, a.shape[1], LANES))
    qdec = wide(jnp.exp(lg * (steps + 1.0)[None, :, None]))
    kdec = wide(jnp.exp(lg * (c - 1.0 - steps)[None, :, None]))
    gchunk = wide(jnp.exp(lg * float(c)))
    rows = lambda width, off: pl.BlockSpec((c, width), lambda b, t: (b * nc + t, off // width))
    table = pl.BlockSpec((c, RET_DK), lambda b, t: (t, 0))
    const = lambda a: pl.BlockSpec(a.shape, lambda b, t: (0, 0, 0))
    o, st = pl.pallas_call(
        _ret_kernel,
        grid=(bsz, nc),
        in_specs=[rows(hk, OFF_RQ), rows(hk, OFF_RK), rows(hv, OFF_RV), rows(hv, OFF_RG),
                  table, table, const(dmat), const(qdec), const(kdec), const(gchunk)],
        out_specs=[rows(hv, 0),
                   pl.BlockSpec((1, RET_HEADS, RET_DV, RET_DK), lambda b, t: (b, 0, 0, 0))],
        out_shape=[jax.ShapeDtypeStruct((bsz * length, hv), BF16),
                   jax.ShapeDtypeStruct((bsz, RET_HEADS, RET_DV, RET_DK), F32)],
        scratch_shapes=[pltpu.VMEM((RET_HEADS, RET_DV, RET_DK), F32)],
        compiler_params=_params(2),
        name="ret_prompt",
    )(proj, proj, proj, proj, cos2, sin2, dmat, qdec, kdec, gchunk)
    return o, st.swapaxes(2, 3)


def _gla_sample_kernel(s_ref, qt_ref, kt_ref, gt_ref, v_ref, gr_ref, nrm_ref, so_ref, o_ref):
    for i in range(SAMPLE_BLOCK):
        for h in range(GLA_HEADS):
            vs = slice(h * GLA_DV, (h + 1) * GLA_DV)
            q = qt_ref[0, h][:, i:i + 1] * (GLA_DK ** -0.5)
            k = kt_ref[0, h][:, i:i + 1]
            decay = jnp.exp(gt_ref[0, h][:, i:i + 1])
            s_new = decay * s_ref[i, h] + k * v_ref[i:i + 1, vs]
            so_ref[i, h] = s_new
            o = jnp.sum(q * s_new, axis=0, keepdims=True)
            o = o * lax.rsqrt(jnp.mean(o * o, axis=-1, keepdims=True) + EPS) * nrm_ref[...]
            o_ref[i:i + 1, vs] = (o * _silu(gr_ref[i:i + 1, vs])).astype(o_ref.dtype)


def _columns(a, heads, dk):
    bsz = a.shape[0]
    return a.reshape(bsz // SAMPLE_BLOCK, SAMPLE_BLOCK, heads, dk).transpose(0, 2, 3, 1)


def gla_sample(proj, g, state, gla_norm):
    bsz = proj.shape[0]
    bb = SAMPLE_BLOCK
    hk, hv = GLA_HEADS * GLA_DK, GLA_HEADS * GLA_DV
    col_spec = pl.BlockSpec((1, GLA_HEADS, GLA_DK, bb), lambda i: (i, 0, 0, 0))
    st_spec = pl.BlockSpec((bb, GLA_HEADS, GLA_DK, GLA_DV), lambda i: (i, 0, 0, 0))
    rows = lambda off: pl.BlockSpec((bb, hv), lambda i: (i, off // hv))
    return pl.pallas_call(
        _gla_sample_kernel,
        grid=(bsz // bb,),
        in_specs=[st_spec, col_spec, col_spec, col_spec, rows(OFF_GV), rows(OFF_GR),
                  pl.BlockSpec((1, GLA_DV), lambda i: (0, 0))],
        out_specs=[st_spec, rows(0)],
        out_shape=[jax.ShapeDtypeStruct(state.shape, F32), jax.ShapeDtypeStruct((bsz, hv), BF16)],
        compiler_params=_params(1),
        name="gla_sample",
    )(state, _columns(proj[:, OFF_GQ:OFF_GQ + hk], GLA_HEADS, GLA_DK),
      _columns(proj[:, OFF_GK:OFF_GK + hk], GLA_HEADS, GLA_DK), _columns(g, GLA_HEADS, GLA_DK),
      proj, proj, gla_norm.reshape(1, GLA_DV))


def _rot_kernel(q_ref, k_ref, cos_ref, sin_ref, qo_ref, ko_ref):
    cos2, sin2 = cos_ref[...], sin_ref[...]
    for h in range(RET_HEADS):
        ks = slice(h * RET_DK, (h + 1) * RET_DK)
        qo_ref[:, ks] = _rotate_pairs(q_ref[:, ks], cos2, sin2)
        ko_ref[:, ks] = _rotate_pairs(k_ref[:, ks], cos2, sin2) * (RET_DK ** -0.5)


def _ret_sample_kernel(s_ref, qt_ref, kt_ref, gam_ref, v_ref, rg_ref, so_ref, o_ref):
    for i in range(SAMPLE_BLOCK):
        for h in range(RET_HEADS):
            vs = slice(h * RET_DV, (h + 1) * RET_DV)
            q = qt_ref[0, h][:, i:i + 1]
            k = kt_ref[0, h][:, i:i + 1]
            s_new = gam_ref[h] * s_ref[i, h] + k * v_ref[i:i + 1, vs]
            so_ref[i, h] = s_new
            o = jnp.sum(q * s_new, axis=0, keepdims=True)
            oc = o - jnp.mean(o, axis=-1, keepdims=True)
            o = oc * lax.rsqrt(jnp.mean(oc * oc, axis=-1, keepdims=True) + EPS)
            o_ref[i:i + 1, vs] = (o * _silu(rg_ref[i:i + 1, vs])).astype(o_ref.dtype)


def ret_sample(proj, state):
    bsz = proj.shape[0]
    bb = SAMPLE_BLOCK
    hk, hv = RET_HEADS * RET_DK, RET_HEADS * RET_DV
    cos2, sin2 = _rotary_tables(PAST_LEN + jnp.arange(1, dtype=F32))
    full = pl.BlockSpec((bsz, hk), lambda i: (0, 0))
    q_rot, k_rot = pl.pallas_call(
        _rot_kernel,
        grid=(1,),
        in_specs=[pl.BlockSpec((bsz, hk), lambda i: (0, OFF_RQ // hk)),
                  pl.BlockSpec((bsz, hk), lambda i: (0, OFF_RK // hk)),
                  pl.BlockSpec((1, RET_DK), lambda i: (0, 0)), pl.BlockSpec((1, RET_DK), lambda i: (0, 0))],
        out_specs=[full, full],
        out_shape=[jax.ShapeDtypeStruct((bsz, hk), F32)] * 2,
        compiler_params=_params(1),
        name="ret_rotary",
    )(proj, proj, cos2, sin2)
    gamma = jnp.broadcast_to(jnp.exp(_log_gamma())[:, None, None], (RET_HEADS, 1, LANES))
    col_spec = pl.BlockSpec((1, RET_HEADS, RET_DK, bb), lambda i: (i, 0, 0, 0))
    st_spec = pl.BlockSpec((bb, RET_HEADS, RET_DK, RET_DV), lambda i: (i, 0, 0, 0))
    rows = lambda off: pl.BlockSpec((bb, hv), lambda i: (i, off // hv))
    return pl.pallas_call(
        _ret_sample_kernel,
        grid=(bsz // bb,),
        in_specs=[st_spec, col_spec, col_spec, pl.BlockSpec(gamma.shape, lambda i: (0, 0, 0)),
                  rows(OFF_RV), rows(OFF_RG)],
        out_specs=[st_spec, rows(0)],
        out_shape=[jax.ShapeDtypeStruct(state.shape, F32), jax.ShapeDtypeStruct((bsz, hv), BF16)],
        compiler_params=_params(1),
        name="ret_sample",
    )(state, _columns(q_rot, RET_HEADS, RET_DK), _columns(k_rot, RET_HEADS, RET_DK), gamma, proj, proj)


def _layer(x, prompt_shape, states, lw, s5_mats):
    (norm_mix, w_main, w_glr, s5_log_dt, s5_a_re, s5_a_im, s5_d, s5_w_glu, s5_b_glu, s5_w_out,
     gla_w_gate, gla_b_gate, gla_norm, gla_w_out, ret_w_out, w_mix_out, norm_ffn,
     w_ffn_gate, w_ffn_up, w_ffn_down) = lw
    t_mat, p_mat, q_mat, bbt, w0 = s5_mats
    m = x.shape[0]
    tm = min(m, 512)
    tn = 512

    h = rmsnorm(x, norm_mix, BF16)
    proj = in_proj(h, w_main, tm, tn)
    g = gla_gate(h, w_glr, gla_w_gate, gla_b_gate, tm)
    u = proj[:, OFF_U:OFF_U + S5_WIDTH]

    if prompt_shape is not None:
        bsz, length = prompt_shape
        y5, x_fin = s5_prompt(u, bsz, length, t_mat, p_mat, q_mat, s5_log_dt, s5_a_re, s5_a_im, s5_d)
        x_fin = x_fin.transpose(1, 0, 2)
        new_s5_re, new_s5_im = x_fin[..., :S5_STATE], x_fin[..., S5_STATE:]
        o_gla, new_gla = gla_prompt(proj, g, gla_norm, bsz, length)
        o_ret, new_ret = ret_prompt(proj, bsz, length)
    else:
        st_s5_re, st_s5_im, st_gla, st_ret = states
        y5, new_s5_re, new_s5_im = s5_sample(u, st_s5_re, st_s5_im, bbt, w0, s5_log_dt, s5_a_re,
                                             s5_a_im, s5_d)
        new_gla, o_gla = gla_sample(proj, g, st_gla, gla_norm)
        new_ret, o_ret = ret_sample(proj, st_ret)

    a5 = s5_glu(y5, s5_w_glu, s5_b_glu, tm, tn)
    merged = merge_branches(a5, o_gla, o_ret, s5_w_out, gla_w_out, ret_w_out, proj, tm, tn)
    x = residual_matmul(merged, w_mix_out, x, tm, tn)
    h2 = rmsnorm(x, norm_ffn, BF16)
    ff = ffn_up(h2, w_ffn_gate, w_ffn_up, tm, tn)
    x = residual_matmul(ff, w_ffn_down, x, tm, tn)
    return x, new_s5_re, new_s5_im, new_gla, new_ret


def kernel(x_prompt, x_sample, state_s5_re, state_s5_im, state_gla, state_ret, norm_mix, w_in, s5_a_re, s5_a_im, s5_log_dt, s5_b_re, s5_b_im, s5_c_re, s5_c_im, s5_d, s5_w_glu, s5_b_glu, s5_w_out, gla_w_gate, gla_b_gate, gla_norm, gla_w_out, ret_w_out, w_mix_out, norm_ffn, w_ffn_gate, w_ffn_up, w_ffn_down, norm_final):
    bp, lp, d = x_prompt.shape
    bs = x_sample.shape[0]
    depth = w_in.shape[0]
    hp = x_prompt.reshape(bp * lp, d)
    hs = x_sample.reshape(bs, d)
    outs_p, outs_s = [], []
    for l in range(depth):
        w_main = jnp.concatenate([w_in[l][:, :GLR_COL], w_in[l][:, GLR_COL + GLA_RANK:]],
                                 axis=1).astype(BF16)
        w_glr = jnp.pad(w_in[l][:, GLR_COL:GLR_COL + GLA_RANK], ((0, 0), (0, LANES - GLA_RANK)))
        w_gate = jnp.pad(gla_w_gate[l], ((0, LANES - GLA_RANK), (0, 0)))
        lw = (norm_mix[l], w_main, w_glr, s5_log_dt[l], s5_a_re[l], s5_a_im[l], s5_d[l], s5_w_glu[l],
              s5_b_glu[l], s5_w_out[l], w_gate, gla_b_gate[l], gla_norm[l], gla_w_out[l],
              ret_w_out[l], w_mix_out[l], norm_ffn[l], w_ffn_gate[l], w_ffn_up[l], w_ffn_down[l])
        s5_mats = s5_prep(s5_a_re[l], s5_a_im[l], s5_log_dt[l], s5_b_re[l], s5_b_im[l], s5_c_re[l],
                          s5_c_im[l])
        hp, *new_p = _layer(hp, (bp, lp), None, lw, s5_mats)
        hs, *new_s = _layer(hs, None, (state_s5_re[l], state_s5_im[l], state_gla[l], state_ret[l]),
                            lw, s5_mats)
        outs_p.append(new_p)
        outs_s.append(new_s)
    y_prompt = rmsnorm(hp, norm_final, F32).reshape(bp, lp, d)
    y_sample = rmsnorm(hs, norm_final, F32).reshape(bs, 1, d)
    stack = lambda outs, i: jnp.stack([o[i] for o in outs])
    return (y_prompt, y_sample,
            stack(outs_p, 0), stack(outs_p, 1), stack(outs_p, 2), stack(outs_p, 3),
            stack(outs_s, 0), stack(outs_s, 1), stack(outs_s, 2), stack(outs_s, 3))
```

```python
import functools
import math

import jax
import jax.numpy as jnp
import numpy as np
from jax import lax
from jax.experimental import pallas as pl
from jax.experimental.pallas import tpu as pltpu

F32 = jnp.float32
BF16 = jnp.bfloat16
HIGHEST = lax.Precision.HIGHEST

D_MODEL = 2048
S5_WIDTH = D_MODEL // 2
S5_GROUP = 16
S5_GROUPS = S5_WIDTH // S5_GROUP
S5_STATE = 64
S5_CHUNK = 16
GLA_HEADS = 4
GLA_DK = D_MODEL // 16
GLA_DV = D_MODEL // 8
GLA_RANK = 16
GLA_GATE_NORM = 16.0
RET_HEADS = 8
RET_DK = D_MODEL // 16
RET_DV = D_MODEL // 16
ROPE_BASE = 10000.0
PAST_LEN = 16384
N_BRANCH = 3
D_FF = -(-8 * D_MODEL // (3 * 256)) * 256
EPS = 1e-6
CHUNK = 128
LANES = 128
SAMPLE_BLOCK = 8

OFF_U = 0
OFF_GQ = OFF_U + S5_WIDTH
OFF_GK = OFF_GQ + GLA_HEADS * GLA_DK
OFF_GV = OFF_GK + GLA_HEADS * GLA_DK
OFF_GR = OFF_GV + GLA_HEADS * GLA_DV
OFF_RQ = OFF_GR + GLA_HEADS * GLA_DV
OFF_RK = OFF_RQ + RET_HEADS * RET_DK
OFF_RV = OFF_RK + RET_HEADS * RET_DK
OFF_RG = OFF_RV + RET_HEADS * RET_DV
OFF_MG = OFF_RG + RET_HEADS * RET_DV
W_MAIN = OFF_MG + N_BRANCH * D_MODEL
GLR_COL = OFF_RQ

VMEM_LIMIT = 56 * 1024 * 1024

NT = (((1,), (1,)), ((), ()))
TN = (((0,), (0,)), ((), ()))


def _params(n_grid):
    return pltpu.CompilerParams(dimension_semantics=("arbitrary",) * n_grid,
                                vmem_limit_bytes=VMEM_LIMIT)


def _silu(x):
    return x * jax.nn.sigmoid(x)


def _rmsnorm_kernel(x_ref, g_ref, o_ref):
    x = x_ref[...]
    y = x * lax.rsqrt(jnp.mean(x * x, axis=-1, keepdims=True) + EPS)
    o_ref[...] = (y * g_ref[...]).astype(o_ref.dtype)


def rmsnorm(x, gain, out_dtype):
    m, d = x.shape
    tm = min(m, 512)
    return pl.pallas_call(
        _rmsnorm_kernel,
        grid=(m // tm,),
        in_specs=[pl.BlockSpec((tm, d), lambda i: (i, 0)),
                  pl.BlockSpec((1, d), lambda i: (0, 0))],
        out_specs=pl.BlockSpec((tm, d), lambda i: (i, 0)),
        out_shape=jax.ShapeDtypeStruct((m, d), out_dtype),
        compiler_params=_params(1),
        name="rmsnorm",
    )(x, gain.reshape(1, d))


def _stage_weights(pairs):
    @pl.when(pl.program_id(1) == 0)
    def _():
        for w_ref, wb_ref in pairs:
            wb_ref[...] = w_ref[...].astype(BF16)


def _dot(a, b):
    return jnp.dot(a, b, preferred_element_type=F32)


def _in_proj_kernel(a_ref, w_ref, o_ref, wb_ref):
    @pl.when(pl.program_id(1) == 0)
    def _():
        wb_ref[...] = w_ref[...].reshape(wb_ref.shape).astype(BF16)

    o_ref[...] = lax.dot_general(a_ref[...], wb_ref[...], NT, preferred_element_type=F32)


def in_proj(h, w_rows, layer, tm, tn):
    m, k = h.shape
    rb = tn // GLA_RANK
    skip_from = GLR_COL // tn

    def w_index(j, i):
        return (layer, rb * j + jnp.where(j >= skip_from, 1, 0), 0, 0)

    return pl.pallas_call(
        _in_proj_kernel,
        grid=(W_MAIN // tn, m // tm),
        in_specs=[pl.BlockSpec((tm, k), lambda j, i: (i, 0)),
                  pl.BlockSpec((pl.Element(1), pl.Element(rb), pl.Element(GLA_RANK), pl.Element(k)),
                               w_index)],
        out_specs=pl.BlockSpec((tm, tn), lambda j, i: (i, j)),
        out_shape=jax.ShapeDtypeStruct((m, W_MAIN), F32),
        scratch_shapes=[pltpu.VMEM((tn, k), BF16)],
        compiler_params=_params(2),
        name="in_proj",
    )(h, w_rows)


def _gla_gate_kernel(h_ref, wl_ref, wg_ref, b_ref, o_ref):
    pad = LANES - GLA_RANK
    wl = jnp.concatenate([wl_ref[0], jnp.zeros((pad, wl_ref.shape[2]), F32)], axis=0).astype(BF16)
    wg = jnp.concatenate([wg_ref[...], jnp.zeros((pad, wg_ref.shape[1]), F32)], axis=0).astype(BF16)
    glr = lax.dot_general(h_ref[...], wl, NT, preferred_element_type=F32)
    logit = _dot(glr.astype(BF16), wg) + b_ref[...]
    log_sig = jnp.minimum(logit, 0.0) - jnp.log1p(jnp.exp(-jnp.abs(logit)))
    o_ref[...] = log_sig / GLA_GATE_NORM


def gla_gate(h, w_rows, w_gate, b_gate, layer, tm):
    m, k = h.shape
    n = w_gate.shape[2]
    return pl.pallas_call(
        _gla_gate_kernel,
        grid=(m // tm,),
        in_specs=[pl.BlockSpec((tm, k), lambda i: (i, 0)),
                  pl.BlockSpec((None, 1, GLA_RANK, k), lambda i: (layer, GLR_COL // GLA_RANK, 0, 0)),
                  pl.BlockSpec((None, GLA_RANK, n), lambda i: (layer, 0, 0)),
                  pl.BlockSpec((None, 1, n), lambda i: (layer, 0, 0))],
        out_specs=pl.BlockSpec((tm, n), lambda i: (i, 0)),
        out_shape=jax.ShapeDtypeStruct((m, n), F32),
        compiler_params=_params(1),
        name="gla_gate",
    )(h, w_rows, w_gate, b_gate.reshape(b_gate.shape[0], 1, n))


def _s5_glu_kernel(a_ref, w_ref, y_ref, b_ref, o_ref, wb_ref):
    _stage_weights([(w_ref, wb_ref)])
    acc = _dot(a_ref[...].astype(BF16), wb_ref[...])
    o_ref[...] = (y_ref[...] * jax.nn.sigmoid(acc + b_ref[...])).astype(o_ref.dtype)


def s5_glu(y5, w, b, layer, tm, tn):
    m, k = y5.shape
    n = w.shape[2]
    return pl.pallas_call(
        _s5_glu_kernel,
        grid=(n // tn, m // tm),
        in_specs=[pl.BlockSpec((tm, k), lambda j, i: (i, 0)),
                  pl.BlockSpec((None, k, tn), lambda j, i: (layer, 0, j)),
                  pl.BlockSpec((tm, tn), lambda j, i: (i, j)),
                  pl.BlockSpec((None, 1, tn), lambda j, i: (layer, 0, j))],
        out_specs=pl.BlockSpec((tm, tn), lambda j, i: (i, j)),
        out_shape=jax.ShapeDtypeStruct((m, n), BF16),
        scratch_shapes=[pltpu.VMEM((k, tn), BF16)],
        compiler_params=_params(2),
        name="s5_glu",
    )(y5, w, y5, b.reshape(b.shape[0], 1, n))


def _merge_kernel(a5_ref, ag_ref, ar_ref, w5_ref, wg_ref, wr_ref, m5_ref, mg_ref, mr_ref,
                  o_ref, w5b_ref, wgb_ref, wrb_ref):
    _stage_weights([(w5_ref, w5b_ref), (wg_ref, wgb_ref), (wr_ref, wrb_ref)])
    merged = (jax.nn.sigmoid(m5_ref[...]) * _dot(a5_ref[...], w5b_ref[...])
              + jax.nn.sigmoid(mg_ref[...]) * _dot(ag_ref[...], wgb_ref[...])
              + jax.nn.sigmoid(mr_ref[...]) * _dot(ar_ref[...], wrb_ref[...]))
    o_ref[...] = merged.astype(o_ref.dtype)


def merge_branches(a5, ag, ar, w5, wg, wr, proj, layer, tm, tn):
    m, k = a5.shape
    n = w5.shape[2]
    a_spec = pl.BlockSpec((tm, k), lambda j, i: (i, 0))
    w_spec = pl.BlockSpec((None, k, tn), lambda j, i: (layer, 0, j))

    def gate_spec(branch):
        first = (OFF_MG + branch * D_MODEL) // tn
        return pl.BlockSpec((tm, tn), lambda j, i: (i, first + j))

    return pl.pallas_call(
        _merge_kernel,
        grid=(n // tn, m // tm),
        in_specs=[a_spec, a_spec, a_spec, w_spec, w_spec, w_spec,
                  gate_spec(0), gate_spec(1), gate_spec(2)],
        out_specs=pl.BlockSpec((tm, tn), lambda j, i: (i, j)),
        out_shape=jax.ShapeDtypeStruct((m, n), BF16),
        scratch_shapes=[pltpu.VMEM((k, tn), BF16)] * 3,
        compiler_params=_params(2),
        name="merge_branches",
    )(a5, ag, ar, w5, wg, wr, proj, proj, proj)


def _residual_kernel(a_ref, w_ref, x_ref, o_ref, wb_ref):
    _stage_weights([(w_ref, wb_ref)])
    o_ref[...] = x_ref[...] + _dot(a_ref[...], wb_ref[...])


def residual_matmul(a, w, x, layer, tm, tn):
    m, k = a.shape
    n = w.shape[2]
    return pl.pallas_call(
        _residual_kernel,
        grid=(n // tn, m // tm),
        in_specs=[pl.BlockSpec((tm, k), lambda j, i: (i, 0)),
                  pl.BlockSpec((None, k, tn), lambda j, i: (layer, 0, j)),
                  pl.BlockSpec((tm, tn), lambda j, i: (i, j))],
        out_specs=pl.BlockSpec((tm, tn), lambda j, i: (i, j)),
        out_shape=jax.ShapeDtypeStruct((m, n), F32),
        scratch_shapes=[pltpu.VMEM((k, tn), BF16)],
        compiler_params=_params(2),
        name="residual_matmul",
    )(a, w, x)


def _ffn_up_kernel(a_ref, wg_ref, wu_ref, o_ref, wgb_ref, wub_ref):
    _stage_weights([(wg_ref, wgb_ref), (wu_ref, wub_ref)])
    a = a_ref[...]
    o_ref[...] = (_silu(_dot(a, wgb_ref[...])) * _dot(a, wub_ref[...])).astype(o_ref.dtype)


def ffn_up(h, wg, wu, layer, tm, tn):
    m, k = h.shape
    n = wg.shape[2]
    w_spec = pl.BlockSpec((None, k, tn), lambda j, i: (layer, 0, j))
    return pl.pallas_call(
        _ffn_up_kernel,
        grid=(n // tn, m // tm),
        in_specs=[pl.BlockSpec((tm, k), lambda j, i: (i, 0)), w_spec, w_spec],
        out_specs=pl.BlockSpec((tm, tn), lambda j, i: (i, j)),
        out_shape=jax.ShapeDtypeStruct((m, n), BF16),
        scratch_shapes=[pltpu.VMEM((k, tn), BF16)] * 2,
        compiler_params=_params(2),
        name="ffn_up",
    )(h, wg, wu)


def _s5_prep_kernel(ldt_ref, arc_ref, aic_ref, arr_ref, air_ref, b1_ref, crt_ref, cit_ref,
                    t_ref, p_ref, q_ref, bbt_ref, w0_ref):
    s = S5_CHUNK
    dt = jnp.exp(ldt_ref[0])
    ar_row, ai_row = arr_ref[0], air_ref[0]
    lam_r_row, lam_i_row = ar_row * dt, ai_row * dt
    lam_r_col, lam_i_col = arc_ref[0] * dt, aic_ref[0] * dt

    mag = jnp.exp(lam_r_row)
    abr, abi = mag * jnp.cos(lam_i_row), mag * jnp.sin(lam_i_row)
    den = ar_row * ar_row + ai_row * ai_row
    nr = abr - 1.0
    fr = (nr * ar_row + abi * ai_row) / den
    fi = (abi * ar_row - nr * ai_row) / den

    b1 = b1_ref[0]
    lane = lax.broadcasted_iota(jnp.int32, b1.shape, 1)
    b2 = pltpu.roll(b1, S5_STATE, axis=1) * jnp.where(lane < S5_STATE, -1.0, 1.0)
    bbt = fr * b1 + fi * b2
    bbt_sw = fr * b2 - fi * b1
    bbt_ref[0] = bbt

    rows = lax.broadcasted_iota(jnp.int32, (s * S5_GROUP, LANES), 0)
    tau_p = (s - 1 - jnp.right_shift(rows, 4)).astype(F32)
    mag_p = jnp.exp(lam_r_row * tau_p)
    ar_p, ai_p = mag_p * jnp.cos(lam_i_row * tau_p), mag_p * jnp.sin(lam_i_row * tau_p)
    p_ref[0] = (jnp.concatenate([bbt] * s, axis=0) * ar_p
                + jnp.concatenate([bbt_sw] * s, axis=0) * ai_p).astype(p_ref.dtype)

    crt, cit = crt_ref[0], cit_ref[0]
    lanes_t = jnp.right_shift(lax.broadcasted_iota(jnp.int32, crt.shape, 1), 4).astype(F32)

    def coeffs(tau):
        mag_t = jnp.exp(lam_r_col * tau)
        ar_t, ai_t = mag_t * jnp.cos(lam_i_col * tau), mag_t * jnp.sin(lam_i_col * tau)
        return jnp.concatenate([ar_t * crt - ai_t * cit, -(ar_t * cit + ai_t * crt)], axis=0)

    w0 = coeffs(lanes_t)
    w0_ref[0] = w0
    q_ref[0] = coeffs(lanes_t + 1.0).astype(q_ref.dtype)

    t0 = jnp.dot(bbt, w0, precision=HIGHEST, preferred_element_type=F32)
    lane_t = lax.broadcasted_iota(jnp.int32, t0.shape, 1)
    blocks = [t0]
    for i in range(1, s):
        blocks.append(jnp.where(lane_t >= i * S5_GROUP, pltpu.roll(t0, i * S5_GROUP, axis=1), 0.0))
    t_ref[0] = jnp.concatenate(blocks, axis=0).astype(t_ref.dtype)


def s5_prep(a_re, a_im, log_dt, b_re, b_im, c_re, c_im):
    g, p, n, s = S5_GROUPS, S5_STATE, S5_GROUP, S5_CHUNK
    dup = lambda a: jnp.concatenate([a, a], axis=-1)[:, None, :]
    b1 = jnp.concatenate([b_re.transpose(0, 2, 1), b_im.transpose(0, 2, 1)], axis=-1)
    crt = jnp.tile(c_re.transpose(0, 2, 1), (1, 1, s))
    cit = jnp.tile(c_im.transpose(0, 2, 1), (1, 1, s))
    spec = lambda *shape: pl.BlockSpec((1,) + shape, lambda i: (i, 0, 0))
    return pl.pallas_call(
        _s5_prep_kernel,
        grid=(g,),
        in_specs=[spec(1, 1), spec(p, 1), spec(p, 1), spec(1, 2 * p), spec(1, 2 * p),
                  spec(n, 2 * p), spec(p, s * n), spec(p, s * n)],
        out_specs=[spec(s * n, s * n), spec(s * n, 2 * p), spec(2 * p, s * n),
                   spec(n, 2 * p), spec(2 * p, s * n)],
        out_shape=[jax.ShapeDtypeStruct((g, s * n, s * n), BF16),
                   jax.ShapeDtypeStruct((g, s * n, 2 * p), BF16),
                   jax.ShapeDtypeStruct((g, 2 * p, s * n), BF16),
                   jax.ShapeDtypeStruct((g, n, 2 * p), F32),
                   jax.ShapeDtypeStruct((g, 2 * p, s * n), F32)],
        compiler_params=_params(1),
        name="s5_prep",
    )(log_dt[:, None, None], a_re[:, :, None], a_im[:, :, None], dup(a_re), dup(a_im), b1, crt, cit)


def _s5_z_kernel(u_ref, p_ref, z_ref):
    z_ref[0] = _dot(u_ref[0].astype(BF16), p_ref[0])


def _abar_power(ldt_ref, arr_ref, air_ref, power):
    dt = jnp.exp(ldt_ref[...])
    mag = jnp.exp(arr_ref[...] * dt * power)
    ang = air_ref[...] * dt * power
    lane = lax.broadcasted_iota(jnp.int32, mag.shape, 2)
    return mag * jnp.cos(ang), mag * jnp.sin(ang) * jnp.where(lane < S5_STATE, -1.0, 1.0)


def _complex_scale(x, a_r, a_i_signed):
    return a_r * x + a_i_signed * pltpu.roll(x, S5_STATE, axis=x.ndim - 1)


def _s5_scan_kernel(z_ref, ldt_ref, arr_ref, air_ref, x_ref, fin_ref, *, bsz):
    a_r, a_i = _abar_power(ldt_ref, arr_ref, air_ref, float(S5_CHUNK))
    n_chunks = z_ref.shape[1] // bsz

    def step(j, x):
        rows = pl.ds(pl.multiple_of(j * bsz, bsz), bsz)
        x_ref[:, rows, :] = x
        return _complex_scale(x, a_r, a_i) + z_ref[:, rows, :]

    fin_ref[...] = lax.fori_loop(0, n_chunks, step, jnp.zeros(fin_ref.shape, F32))


def _s5_y_kernel(u_ref, x_ref, t_ref, q_ref, d_ref, y_ref):
    u = u_ref[0]
    y = _dot(u.astype(BF16), t_ref[0]) + _dot(x_ref[0].astype(BF16), q_ref[0]) + d_ref[0] * u
    y_ref[0] = jax.nn.gelu(y)


def s5_prompt(u, bsz, length, t_mat, p_mat, q_mat, log_dt, a_re, a_im, d_skip):
    g, n, s, p2 = S5_GROUPS, S5_GROUP, S5_CHUNK, 2 * S5_STATE
    n_chunks = length // s
    rows = n_chunks * bsz
    ug = u.reshape(bsz, n_chunks, s, g, n).transpose(3, 1, 0, 2, 4).reshape(g, rows, s * n)
    gspec = lambda *shape: pl.BlockSpec((1,) + shape, lambda i: (i, 0, 0))
    z = pl.pallas_call(
        _s5_z_kernel,
        grid=(g,),
        in_specs=[gspec(rows, s * n), gspec(s * n, p2)],
        out_specs=gspec(rows, p2),
        out_shape=jax.ShapeDtypeStruct((g, rows, p2), F32),
        compiler_params=_params(1),
        name="s5_z",
    )(ug, p_mat)
    dup = lambda a: jnp.concatenate([a, a], axis=-1)[:, None, :]
    x_start, x_fin = pl.pallas_call(
        functools.partial(_s5_scan_kernel, bsz=bsz),
        out_shape=[jax.ShapeDtypeStruct((g, rows, p2), F32),
                   jax.ShapeDtypeStruct((g, bsz, p2), F32)],
        compiler_params=pltpu.CompilerParams(vmem_limit_bytes=VMEM_LIMIT),
        name="s5_scan",
    )(z, log_dt[:, None, None], dup(a_re), dup(a_im))
    d_tile = jnp.tile(d_skip.reshape(g, 1, n), (1, 1, s))
    y = pl.pallas_call(
        _s5_y_kernel,
        grid=(g,),
        in_specs=[gspec(rows, s * n), gspec(rows, p2), gspec(s * n, s * n), gspec(p2, s * n),
                  gspec(1, s * n)],
        out_specs=gspec(rows, s * n),
        out_shape=jax.ShapeDtypeStruct((g, rows, s * n), F32),
        compiler_params=_params(1),
        name="s5_y",
    )(ug, x_start, t_mat, q_mat, d_tile)
    y = y.reshape(g, n_chunks, bsz, s, n).transpose(2, 1, 3, 0, 4).reshape(bsz * length, g * n)
    return y, x_fin


def _s5_sample_kernel(u_ref, xr_ref, xi_ref, bbt_ref, w0_ref, ldt_ref, arr_ref, air_ref, d_ref,
                      y_ref, nr_ref, ni_ref):
    p = S5_STATE
    dt = jnp.exp(ldt_ref[0])
    mag = jnp.exp(arr_ref[0][:, :p] * dt)
    ang = air_ref[0][:, :p] * dt
    abr, abi = mag * jnp.cos(ang), mag * jnp.sin(ang)
    u = u_ref[0]
    z = _dot(u.astype(BF16), bbt_ref[0].astype(BF16))
    xr, xi = xr_ref[0], xi_ref[0]
    new_r = abr * xr - abi * xi + z[:, :p]
    new_i = abr * xi + abi * xr + z[:, p:]
    nr_ref[0] = new_r
    ni_ref[0] = new_i
    c = w0_ref[0][:, :S5_GROUP].astype(BF16)
    y = _dot(new_r.astype(BF16), c[:p]) + _dot(new_i.astype(BF16), c[p:]) + d_ref[0] * u
    y_ref[0] = jax.nn.gelu(y)


def s5_sample(u, x_re, x_im, bbt, w0, log_dt, a_re, a_im, d_skip):
    g, n, p = S5_GROUPS, S5_GROUP, S5_STATE
    bsz = u.shape[0]
    dup = lambda a: jnp.concatenate([a, a], axis=-1)[:, None, :]
    gspec = lambda *shape: pl.BlockSpec((1,) + shape, lambda i: (i, 0, 0))
    y, new_r, new_i = pl.pallas_call(
        _s5_sample_kernel,
        grid=(g,),
        in_specs=[gspec(bsz, n), gspec(bsz, p), gspec(bsz, p), gspec(n, 2 * p),
                  gspec(2 * p, S5_CHUNK * n), gspec(1, 1), gspec(1, 2 * p), gspec(1, 2 * p),
                  gspec(1, n)],
        out_specs=[gspec(bsz, n), gspec(bsz, p), gspec(bsz, p)],
        out_shape=[jax.ShapeDtypeStruct((g, bsz, n), F32),
                   jax.ShapeDtypeStruct((g, bsz, p), F32),
                   jax.ShapeDtypeStruct((g, bsz, p), F32)],
        compiler_params=_params(1),
        name="s5_sample",
    )(u.reshape(bsz, g, n).transpose(1, 0, 2), x_re.transpose(1, 0, 2), x_im.transpose(1, 0, 2),
      bbt, w0, log_dt[:, None, None], dup(a_re), dup(a_im), d_skip.reshape(g, 1, n))
    return (y.transpose(1, 0, 2).reshape(bsz, g * n), new_r.transpose(1, 0, 2),
            new_i.transpose(1, 0, 2))


def _gla_kernel(q_ref, k_ref, v_ref, gr_ref, g_ref, nrm_ref, o_ref, st_ref, s_ref):
    t = pl.program_id(1)

    @pl.when(t == 0)
    def _():
        s_ref[...] = jnp.zeros(s_ref.shape, F32)

    c = q_ref.shape[0]
    row = lax.broadcasted_iota(jnp.int32, (c, c), 0)
    col = lax.broadcasted_iota(jnp.int32, (c, c), 1)
    causal = col <= row
    ones_tri = causal.astype(F32)
    for h in range(GLA_HEADS):
        ks = slice(h * GLA_DK, (h + 1) * GLA_DK)
        vs = slice(h * GLA_DV, (h + 1) * GLA_DV)
        b = jnp.dot(ones_tri, g_ref[:, ks], precision=HIGHEST, preferred_element_type=F32)
        b_end = b[c - 1:c, :]
        q_in = (q_ref[:, ks] * (GLA_DK ** -0.5) * jnp.exp(b)).astype(BF16)
        k_in = (k_ref[:, ks] * jnp.exp(-b)).astype(BF16)
        k_end = (k_ref[:, ks] * jnp.exp(b_end - b)).astype(BF16)
        v = v_ref[:, vs].astype(BF16)
        scores = jnp.where(causal, lax.dot_general(q_in, k_in, NT, preferred_element_type=F32), 0.0)
        s_t = s_ref[h]
        o = (_dot(scores.astype(BF16), v)
             + lax.dot_general(q_in, s_t.astype(BF16), NT, preferred_element_type=F32))
        s_ref[h] = jnp.exp(b_end) * s_t + lax.dot_general(v, k_end, TN, preferred_element_type=F32)
        o = o * lax.rsqrt(jnp.mean(o * o, axis=-1, keepdims=True) + EPS) * nrm_ref[...]
        o_ref[:, vs] = (o * _silu(gr_ref[:, vs])).astype(o_ref.dtype)

    @pl.when(t == pl.num_programs(1) - 1)
    def _():
        st_ref[0] = s_ref[...]


def gla_prompt(proj, g, gla_norm, bsz, length):
    c = CHUNK
    nc = length // c
    hk, hv = GLA_HEADS * GLA_DK, GLA_HEADS * GLA_DV
    rows = lambda width, off: pl.BlockSpec((c, width), lambda b, t: (b * nc + t, off // width))
    o, st = pl.pallas_call(
        _gla_kernel,
        grid=(bsz, nc),
        in_specs=[rows(hk, OFF_GQ), rows(hk, OFF_GK), rows(hv, OFF_GV), rows(hv, OFF_GR),
                  rows(hk, 0), pl.BlockSpec((1, GLA_DV), lambda b, t: (0, 0))],
        out_specs=[rows(hv, 0),
                   pl.BlockSpec((1, GLA_HEADS, GLA_DV, GLA_DK), lambda b, t: (b, 0, 0, 0))],
        out_shape=[jax.ShapeDtypeStruct((bsz * length, hv), BF16),
                   jax.ShapeDtypeStruct((bsz, GLA_HEADS, GLA_DV, GLA_DK), F32)],
        scratch_shapes=[pltpu.VMEM((GLA_HEADS, GLA_DV, GLA_DK), F32)],
        compiler_params=_params(2),
        name="gla_prompt",
    )(proj, proj, proj, proj, g, gla_norm.reshape(1, GLA_DV))
    return o, st.swapaxes(2, 3)


def _rotate_pairs(x, cos2, sin2):
    lane = lax.broadcasted_iota(jnp.int32, x.shape, 1)
    partner = jnp.where(lane % 2 == 0, pltpu.roll(x, x.shape[1] - 1, axis=1), pltpu.roll(x, 1, axis=1))
    return x * cos2 + partner * sin2


def _ret_kernel(q_ref, k_ref, v_ref, rg_ref, cos_ref, sin_ref, dm_ref, qd_ref, kd_ref, gc_ref,
                o_ref, st_ref, s_ref):
    t = pl.program_id(1)

    @pl.when(t == 0)
    def _():
        s_ref[...] = jnp.zeros(s_ref.shape, F32)

    cos2, sin2 = cos_ref[...], sin_ref[...]
    for h in range(RET_HEADS):
        ks = slice(h * RET_DK, (h + 1) * RET_DK)
        vs = slice(h * RET_DV, (h + 1) * RET_DV)
        q = _rotate_pairs(q_ref[:, ks], cos2, sin2)
        k = _rotate_pairs(k_ref[:, ks], cos2, sin2) * (RET_DK ** -0.5)
        v = v_ref[:, vs].astype(BF16)
        scores = lax.dot_general(q.astype(BF16), k.astype(BF16), NT, preferred_element_type=F32) * dm_ref[h]
        s_t = s_ref[h]
        o = (_dot(scores.astype(BF16), v)
             + lax.dot_general((q * qd_ref[h]).astype(BF16), s_t.astype(BF16), NT,
                               preferred_element_type=F32))
        s_ref[h] = gc_ref[h] * s_t + lax.dot_general(v, (k * kd_ref[h]).astype(BF16), TN,
                                                     preferred_element_type=F32)
        oc = o - jnp.mean(o, axis=-1, keepdims=True)
        o = oc * lax.rsqrt(jnp.mean(oc * oc, axis=-1, keepdims=True) + EPS)
        o_ref[:, vs] = (o * _silu(rg_ref[:, vs])).astype(o_ref.dtype)

    @pl.when(t == pl.num_programs(1) - 1)
    def _():
        st_ref[0] = s_ref[...]


def _rotary_tables(pos):
    half = RET_DK // 2
    inv = 1.0 / (ROPE_BASE ** jnp.linspace(0.0, 1.0, half, dtype=F32))
    ang = pos[:, None] * inv[None, :]
    cos2 = jnp.repeat(jnp.cos(ang), 2, axis=-1)
    sin2 = jnp.stack([-jnp.sin(ang), jnp.sin(ang)], axis=-1).reshape(pos.shape[0], RET_DK)
    return cos2, sin2


def _log_gamma():
    return jnp.log1p(-jnp.power(2.0, -5.0 - jnp.arange(RET_HEADS, dtype=F32)))


def ret_prompt(proj, bsz, length):
    c = CHUNK
    nc = length // c
    hk, hv = RET_HEADS * RET_DK, RET_HEADS * RET_DV
    cos2, sin2 = _rotary_tables(jnp.arange(length, dtype=F32))
    lg = _log_gamma()[:, None, None]
    steps = jnp.arange(c, dtype=F32)
    lag = steps[:, None] - steps[None, :]
    dmat = jnp.where(lag >= 0, jnp.exp(lg * jnp.maximum(lag, 0.0)), 0.0)
    wide = lambda a: jnp.broadcast_to(a, (RET_HEADS, a.shape[1], LANES))
    qdec = wide(jnp.exp(lg * (steps + 1.0)[None, :, None]))
    kdec = wide(jnp.exp(lg * (c - 1.0 - steps)[None, :, None]))
    gchunk = wide(jnp.exp(lg * float(c)))
    rows = lambda width, off: pl.BlockSpec((c, width), lambda b, t: (b * nc + t, off // width))
    table = pl.BlockSpec((c, RET_DK), lambda b, t: (t, 0))
    const = lambda a: pl.BlockSpec(a.shape, lambda b, t: (0, 0, 0))
    o, st = pl.pallas_call(
        _ret_kernel,
        grid=(bsz, nc),
        in_specs=[rows(hk, OFF_RQ), rows(hk, OFF_RK), rows(hv, OFF_RV), rows(hv, OFF_RG),
                  table, table, const(dmat), const(qdec), const(kdec), const(gchunk)],
        out_specs=[rows(hv, 0),
                   pl.BlockSpec((1, RET_HEADS, RET_DV, RET_DK), lambda b, t: (b, 0, 0, 0))],
        out_shape=[jax.ShapeDtypeStruct((bsz * length, hv), BF16),
                   jax.ShapeDtypeStruct((bsz, RET_HEADS, RET_DV, RET_DK), F32)],
        scratch_shapes=[pltpu.VMEM((RET_HEADS, RET_DV, RET_DK), F32)],
        compiler_params=_params(2),
        name="ret_prompt",
    )(proj, proj, proj, proj, cos2, sin2, dmat, qdec, kdec, gchunk)
    return o, st.swapaxes(2, 3)


def _gla_sample_kernel(s_ref, qt_ref, kt_ref, gt_ref, v_ref, gr_ref, nrm_ref, *rest):
    so_ref, o_ref = rest[-2:]
    for i in range(SAMPLE_BLOCK):
        for h in range(GLA_HEADS):
            vs = slice(h * GLA_DV, (h + 1) * GLA_DV)
            q = qt_ref[0, h][:, i:i + 1] * (GLA_DK ** -0.5)
            k = kt_ref[0, h][:, i:i + 1]
            decay = jnp.exp(gt_ref[0, h][:, i:i + 1])
            s_new = decay * s_ref[i, h] + k * v_ref[i:i + 1, vs]
            so_ref[i, h] = s_new
            o = jnp.sum(q * s_new, axis=0, keepdims=True)
            o = o * lax.rsqrt(jnp.mean(o * o, axis=-1, keepdims=True) + EPS) * nrm_ref[...]
            o_ref[i:i + 1, vs] = (o * _silu(gr_ref[i:i + 1, vs])).astype(o_ref.dtype)


def _columns(a, heads, dk):
    bsz = a.shape[0]
    return a.reshape(bsz // SAMPLE_BLOCK, SAMPLE_BLOCK, heads, dk).transpose(0, 2, 3, 1)


def _stacked_state_call(kernel_fn, name, state, layer, prev, heads, dk, dv, in_specs, operands,
                        o_spec, o_shape):
    bb = SAMPLE_BLOCK
    st_spec = pl.BlockSpec((None, bb, heads, dk, dv), lambda i: (layer, i, 0, 0, 0))
    aliases = {}
    if prev is not None:
        in_specs = in_specs + [pl.BlockSpec(memory_space=pl.ANY)]
        operands = operands + (prev,)
        aliases = {len(in_specs): 0}
    return pl.pallas_call(
        kernel_fn,
        grid=(state.shape[1] // bb,),
        in_specs=[st_spec] + in_specs,
        out_specs=[st_spec, o_spec],
        out_shape=[jax.ShapeDtypeStruct(state.shape, F32), o_shape],
        input_output_aliases=aliases,
        compiler_params=_params(1),
        name=name,
    )(state, *operands)


def gla_sample(proj, g, state, gla_norm, layer, prev):
    bsz = proj.shape[0]
    bb = SAMPLE_BLOCK
    hk, hv = GLA_HEADS * GLA_DK, GLA_HEADS * GLA_DV
    col_spec = pl.BlockSpec((1, GLA_HEADS, GLA_DK, bb), lambda i: (i, 0, 0, 0))
    rows = lambda off: pl.BlockSpec((bb, hv), lambda i: (i, off // hv))
    return _stacked_state_call(
        _gla_sample_kernel, "gla_sample", state, layer, prev, GLA_HEADS, GLA_DK, GLA_DV,
        [col_spec, col_spec, col_spec, rows(OFF_GV), rows(OFF_GR),
         pl.BlockSpec((None, 1, GLA_DV), lambda i: (layer, 0, 0))],
        (_columns(proj[:, OFF_GQ:OFF_GQ + hk], GLA_HEADS, GLA_DK),
         _columns(proj[:, OFF_GK:OFF_GK + hk], GLA_HEADS, GLA_DK), _columns(g, GLA_HEADS, GLA_DK),
         proj, proj, gla_norm.reshape(gla_norm.shape[0], 1, GLA_DV)),
        rows(0), jax.ShapeDtypeStruct((bsz, hv), BF16))


def _rot_kernel(q_ref, k_ref, cos_ref, sin_ref, qo_ref, ko_ref):
    cos2, sin2 = cos_ref[...], sin_ref[...]
    for h in range(RET_HEADS):
        ks = slice(h * RET_DK, (h + 1) * RET_DK)
        qo_ref[:, ks] = _rotate_pairs(q_ref[:, ks], cos2, sin2)
        ko_ref[:, ks] = _rotate_pairs(k_ref[:, ks], cos2, sin2) * (RET_DK ** -0.5)


def _ret_sample_kernel(s_ref, qt_ref, kt_ref, gam_ref, v_ref, rg_ref, *rest):
    so_ref, o_ref = rest[-2:]
    for i in range(SAMPLE_BLOCK):
        for h in range(RET_HEADS):
            vs = slice(h * RET_DV, (h + 1) * RET_DV)
            q = qt_ref[0, h][:, i:i + 1]
            k = kt_ref[0, h][:, i:i + 1]
            s_new = gam_ref[h] * s_ref[i, h] + k * v_ref[i:i + 1, vs]
            so_ref[i, h] = s_new
            o = jnp.sum(q * s_new, axis=0, keepdims=True)
            oc = o - jnp.mean(o, axis=-1, keepdims=True)
            o = oc * lax.rsqrt(jnp.mean(oc * oc, axis=-1, keepdims=True) + EPS)
            o_ref[i:i + 1, vs] = (o * _silu(rg_ref[i:i + 1, vs])).astype(o_ref.dtype)


def ret_sample(proj, state, layer, prev):
    bsz = proj.shape[0]
    bb = SAMPLE_BLOCK
    hk, hv = RET_HEADS * RET_DK, RET_HEADS * RET_DV
    cos2, sin2 = _rotary_tables(PAST_LEN + jnp.arange(1, dtype=F32))
    full = pl.BlockSpec((bsz, hk), lambda i: (0, 0))
    q_rot, k_rot = pl.pallas_call(
        _rot_kernel,
        grid=(1,),
        in_specs=[pl.BlockSpec((bsz, hk), lambda i: (0, OFF_RQ // hk)),
                  pl.BlockSpec((bsz, hk), lambda i: (0, OFF_RK // hk)),
                  pl.BlockSpec((1, RET_DK), lambda i: (0, 0)), pl.BlockSpec((1, RET_DK), lambda i: (0, 0))],
        out_specs=[full, full],
        out_shape=[jax.ShapeDtypeStruct((bsz, hk), F32)] * 2,
        compiler_params=_params(1),
        name="ret_rotary",
    )(proj, proj, cos2, sin2)
    gamma = jnp.broadcast_to(jnp.exp(_log_gamma())[:, None, None], (RET_HEADS, 1, LANES))
    col_spec = pl.BlockSpec((1, RET_HEADS, RET_DK, bb), lambda i: (i, 0, 0, 0))
    rows = lambda off: pl.BlockSpec((bb, hv), lambda i: (i, off // hv))
    return _stacked_state_call(
        _ret_sample_kernel, "ret_sample", state, layer, prev, RET_HEADS, RET_DK, RET_DV,
        [col_spec, col_spec, pl.BlockSpec(gamma.shape, lambda i: (0, 0, 0)), rows(OFF_RV), rows(OFF_RG)],
        (_columns(q_rot, RET_HEADS, RET_DK), _columns(k_rot, RET_HEADS, RET_DK), gamma, proj, proj),
        rows(0), jax.ShapeDtypeStruct((bsz, hv), BF16))


def _layer(x, layer, prompt_shape, states, w, s5_mats):
    t_mat, p_mat, q_mat, bbt, w0 = s5_mats
    m = x.shape[0]
    tm = min(m, 512)
    tn = 512
    s5_par = (w["s5_log_dt"][layer], w["s5_a_re"][layer], w["s5_a_im"][layer], w["s5_d"][layer])

    h = rmsnorm(x, w["norm_mix"][layer], BF16)
    proj = in_proj(h, w["w_in_rows"], layer, tm, 2 * tn)
    g = gla_gate(h, w["w_in_rows"], w["gla_w_gate"], w["gla_b_gate"], layer, tm)
    u = proj[:, OFF_U:OFF_U + S5_WIDTH]

    if prompt_shape is not None:
        bsz, length = prompt_shape
        y5, x_fin = s5_prompt(u, bsz, length, t_mat, p_mat, q_mat, *s5_par)
        x_fin = x_fin.transpose(1, 0, 2)
        new_states = (x_fin[..., :S5_STATE], x_fin[..., S5_STATE:])
        o_gla, new_gla = gla_prompt(proj, g, w["gla_norm"][layer], bsz, length)
        o_ret, new_ret = ret_prompt(proj, bsz, length)
    else:
        st_s5_re, st_s5_im, st_gla, st_ret, prev_gla, prev_ret = states
        y5, *new_states = s5_sample(u, st_s5_re[layer], st_s5_im[layer], bbt, w0, *s5_par)
        new_gla, o_gla = gla_sample(proj, g, st_gla, w["gla_norm"], layer, prev_gla)
        new_ret, o_ret = ret_sample(proj, st_ret, layer, prev_ret)

    a5 = s5_glu(y5, w["s5_w_glu"], w["s5_b_glu"], layer, tm, tn)
    merged = merge_branches(a5, o_gla, o_ret, w["s5_w_out"], w["gla_w_out"], w["ret_w_out"], proj,
                            layer, tm, tn)
    x = residual_matmul(merged, w["w_mix_out"], x, layer, tm, tn)
    h2 = rmsnorm(x, w["norm_ffn"][layer], BF16)
    ff = ffn_up(h2, w["w_ffn_gate"], w["w_ffn_up"], layer, tm, tn)
    x = residual_matmul(ff, w["w_ffn_down"], x, layer, tm, tn)
    return x, new_states[0], new_states[1], new_gla, new_ret


def kernel(x_prompt, x_sample, state_s5_re, state_s5_im, state_gla, state_ret, norm_mix, w_in, s5_a_re, s5_a_im, s5_log_dt, s5_b_re, s5_b_im, s5_c_re, s5_c_im, s5_d, s5_w_glu, s5_b_glu, s5_w_out, gla_w_gate, gla_b_gate, gla_norm, gla_w_out, ret_w_out, w_mix_out, norm_ffn, w_ffn_gate, w_ffn_up, w_ffn_down, norm_final):
    bp, lp, d = x_prompt.shape
    bs = x_sample.shape[0]
    depth = w_in.shape[0]
    hp = x_prompt.reshape(bp * lp, d)
    hs = x_sample.reshape(bs, d)
    w = dict(norm_mix=norm_mix, s5_log_dt=s5_log_dt, s5_a_re=s5_a_re, s5_a_im=s5_a_im, s5_d=s5_d,
             s5_w_glu=s5_w_glu, s5_b_glu=s5_b_glu, s5_w_out=s5_w_out, gla_w_gate=gla_w_gate,
             gla_b_gate=gla_b_gate, gla_norm=gla_norm, gla_w_out=gla_w_out, ret_w_out=ret_w_out,
             w_mix_out=w_mix_out, norm_ffn=norm_ffn, w_ffn_gate=w_ffn_gate, w_ffn_up=w_ffn_up,
             w_ffn_down=w_ffn_down,
             w_in_rows=jnp.swapaxes(w_in, 1, 2).reshape(depth, w_in.shape[2] // GLA_RANK, GLA_RANK, d))
    outs_p, outs_s = [], []
    new_gla_s = new_ret_s = None
    for l in range(depth):
        s5_mats = s5_prep(s5_a_re[l], s5_a_im[l], s5_log_dt[l], s5_b_re[l], s5_b_im[l], s5_c_re[l],
                          s5_c_im[l])
        hp, *new_p = _layer(hp, l, (bp, lp), None, w, s5_mats)
        hs, s5r, s5i, new_gla_s, new_ret_s = _layer(
            hs, l, None, (state_s5_re, state_s5_im, state_gla, state_ret, new_gla_s, new_ret_s),
            w, s5_mats)
        outs_p.append(new_p)
        outs_s.append((s5r, s5i))
    y_prompt = rmsnorm(hp, norm_final, F32).reshape(bp, lp, d)
    y_sample = rmsnorm(hs, norm_final, F32).reshape(bs, 1, d)
    stack = lambda outs, i: jnp.stack([o[i] for o in outs])
    return (y_prompt, y_sample,
            stack(outs_p, 0), stack(outs_p, 1), stack(outs_p, 2), stack(outs_p, 3),
            stack(outs_s, 0), stack(outs_s, 1), new_gla_s, new_ret_s)
```

```python
import functools
import math

import jax
import jax.numpy as jnp
import numpy as np
from jax import lax
from jax.experimental import pallas as pl
from jax.experimental.pallas import tpu as pltpu

F32 = jnp.float32
BF16 = jnp.bfloat16
HIGHEST = lax.Precision.HIGHEST

D_MODEL = 2048
S5_WIDTH = D_MODEL // 2
S5_GROUP = 16
S5_GROUPS = S5_WIDTH // S5_GROUP
S5_STATE = 64
S5_CHUNK = 16
GLA_HEADS = 4
GLA_DK = D_MODEL // 16
GLA_DV = D_MODEL // 8
GLA_RANK = 16
GLA_GATE_NORM = 16.0
RET_HEADS = 8
RET_DK = D_MODEL // 16
RET_DV = D_MODEL // 16
ROPE_BASE = 10000.0
PAST_LEN = 16384
N_BRANCH = 3
D_FF = -(-8 * D_MODEL // (3 * 256)) * 256
EPS = 1e-6
CHUNK = 128
LANES = 128
SUBLANES = 8
SAMPLE_BLOCK = 16

OFF_U = 0
OFF_GQ = OFF_U + S5_WIDTH
OFF_GK = OFF_GQ + GLA_HEADS * GLA_DK
OFF_GV = OFF_GK + GLA_HEADS * GLA_DK
OFF_GR = OFF_GV + GLA_HEADS * GLA_DV
OFF_RQ = OFF_GR + GLA_HEADS * GLA_DV
OFF_RK = OFF_RQ + RET_HEADS * RET_DK
OFF_RV = OFF_RK + RET_HEADS * RET_DK
OFF_RG = OFF_RV + RET_HEADS * RET_DV
OFF_MG = OFF_RG + RET_HEADS * RET_DV
W_MAIN = OFF_MG + N_BRANCH * D_MODEL
GLR_COL = OFF_RQ

VMEM_LIMIT = 56 * 1024 * 1024

NT = (((1,), (1,)), ((), ()))
TN = (((0,), (0,)), ((), ()))


def _params(n_grid):
    return pltpu.CompilerParams(dimension_semantics=("arbitrary",) * n_grid,
                                vmem_limit_bytes=VMEM_LIMIT)


def _silu(x):
    return x * jax.nn.sigmoid(x)


def _rmsnorm_kernel(x_ref, g_ref, o_ref):
    x = x_ref[...]
    y = x * lax.rsqrt(jnp.mean(x * x, axis=-1, keepdims=True) + EPS)
    o_ref[...] = (y * g_ref[...]).astype(o_ref.dtype)


def rmsnorm(x, gain, out_dtype):
    m, d = x.shape
    tm = min(m, 512)
    return pl.pallas_call(
        _rmsnorm_kernel,
        grid=(m // tm,),
        in_specs=[pl.BlockSpec((tm, d), lambda i: (i, 0)),
                  pl.BlockSpec((1, d), lambda i: (0, 0))],
        out_specs=pl.BlockSpec((tm, d), lambda i: (i, 0)),
        out_shape=jax.ShapeDtypeStruct((m, d), out_dtype),
        compiler_params=_params(1),
        name="rmsnorm",
    )(x, gain.reshape(1, d))


def _stage_weights(pairs):
    @pl.when(pl.program_id(1) == 0)
    def _():
        for w_ref, wb_ref in pairs:
            wb_ref[...] = w_ref[...].astype(BF16)


def _dot(a, b):
    return jnp.dot(a, b, preferred_element_type=F32)


def _in_proj_kernel(a_ref, w_ref, o_ref, wb_ref):
    @pl.when(pl.program_id(1) == 0)
    def _():
        wb_ref[...] = w_ref[...].reshape(wb_ref.shape).astype(BF16)

    o_ref[...] = lax.dot_general(a_ref[...], wb_ref[...], NT, preferred_element_type=F32)


def in_proj(h, w_rows, layer, tm, tn):
    m, k = h.shape
    rb = tn // GLA_RANK
    skip_from = GLR_COL // tn

    def w_index(j, i):
        return (layer, rb * j + jnp.where(j >= skip_from, 1, 0), 0, 0)

    return pl.pallas_call(
        _in_proj_kernel,
        grid=(W_MAIN // tn, m // tm),
        in_specs=[pl.BlockSpec((tm, k), lambda j, i: (i, 0)),
                  pl.BlockSpec((pl.Element(1), pl.Element(rb), pl.Element(GLA_RANK), pl.Element(k)),
                               w_index)],
        out_specs=pl.BlockSpec((tm, tn), lambda j, i: (i, j)),
        out_shape=jax.ShapeDtypeStruct((m, W_MAIN), F32),
        scratch_shapes=[pltpu.VMEM((tn, k), BF16)],
        compiler_params=_params(2),
        name="in_proj",
    )(h, w_rows)


def _gla_gate_kernel(h_ref, wl_ref, wg_ref, b_ref, o_ref):
    pad = LANES - GLA_RANK
    wl = jnp.concatenate([wl_ref[0], jnp.zeros((pad, wl_ref.shape[2]), F32)], axis=0).astype(BF16)
    wg = jnp.concatenate([wg_ref[...], jnp.zeros((pad, wg_ref.shape[1]), F32)], axis=0).astype(BF16)
    glr = lax.dot_general(h_ref[...], wl, NT, preferred_element_type=F32)
    logit = _dot(glr.astype(BF16), wg) + b_ref[...]
    log_sig = jnp.minimum(logit, 0.0) - jnp.log1p(jnp.exp(-jnp.abs(logit)))
    o_ref[...] = log_sig / GLA_GATE_NORM


def gla_gate(h, w_rows, w_gate, b_gate, layer, tm):
    m, k = h.shape
    n = w_gate.shape[2]
    return pl.pallas_call(
        _gla_gate_kernel,
        grid=(m // tm,),
        in_specs=[pl.BlockSpec((tm, k), lambda i: (i, 0)),
                  pl.BlockSpec((None, 1, GLA_RANK, k), lambda i: (layer, GLR_COL // GLA_RANK, 0, 0)),
                  pl.BlockSpec((None, GLA_RANK, n), lambda i: (layer, 0, 0)),
                  pl.BlockSpec((None, 1, n), lambda i: (layer, 0, 0))],
        out_specs=pl.BlockSpec((tm, n), lambda i: (i, 0)),
        out_shape=jax.ShapeDtypeStruct((m, n), F32),
        compiler_params=_params(1),
        name="gla_gate",
    )(h, w_rows, w_gate, b_gate.reshape(b_gate.shape[0], 1, n))


def _s5_glu_kernel(a_ref, w_ref, y_ref, b_ref, o_ref, wb_ref):
    _stage_weights([(w_ref, wb_ref)])
    acc = _dot(a_ref[...].astype(BF16), wb_ref[...])
    o_ref[...] = (y_ref[...] * jax.nn.sigmoid(acc + b_ref[...])).astype(o_ref.dtype)


def s5_glu(y5, w, b, layer, tm, tn):
    m, k = y5.shape
    n = w.shape[2]
    return pl.pallas_call(
        _s5_glu_kernel,
        grid=(n // tn, m // tm),
        in_specs=[pl.BlockSpec((tm, k), lambda j, i: (i, 0)),
                  pl.BlockSpec((None, k, tn), lambda j, i: (layer, 0, j)),
                  pl.BlockSpec((tm, tn), lambda j, i: (i, j)),
                  pl.BlockSpec((None, 1, tn), lambda j, i: (layer, 0, j))],
        out_specs=pl.BlockSpec((tm, tn), lambda j, i: (i, j)),
        out_shape=jax.ShapeDtypeStruct((m, n), BF16),
        scratch_shapes=[pltpu.VMEM((k, tn), BF16)],
        compiler_params=_params(2),
        name="s5_glu",
    )(y5, w, y5, b.reshape(b.shape[0], 1, n))


def _merge_kernel(a5_ref, ag_ref, ar_ref, w5_ref, wg_ref, wr_ref, m5_ref, mg_ref, mr_ref,
                  o_ref, w5b_ref, wgb_ref, wrb_ref):
    _stage_weights([(w5_ref, w5b_ref), (wg_ref, wgb_ref), (wr_ref, wrb_ref)])
    merged = (jax.nn.sigmoid(m5_ref[...]) * _dot(a5_ref[...], w5b_ref[...])
              + jax.nn.sigmoid(mg_ref[...]) * _dot(ag_ref[...], wgb_ref[...])
              + jax.nn.sigmoid(mr_ref[...]) * _dot(ar_ref[...], wrb_ref[...]))
    o_ref[...] = merged.astype(o_ref.dtype)


def merge_branches(a5, ag, ar, w5, wg, wr, proj, layer, tm, tn):
    m, k = a5.shape
    n = w5.shape[2]
    a_spec = pl.BlockSpec((tm, k), lambda j, i: (i, 0))
    w_spec = pl.BlockSpec((None, k, tn), lambda j, i: (layer, 0, j))

    def gate_spec(branch):
        first = (OFF_MG + branch * D_MODEL) // tn
        return pl.BlockSpec((tm, tn), lambda j, i: (i, first + j))

    return pl.pallas_call(
        _merge_kernel,
        grid=(n // tn, m // tm),
        in_specs=[a_spec, a_spec, a_spec, w_spec, w_spec, w_spec,
                  gate_spec(0), gate_spec(1), gate_spec(2)],
        out_specs=pl.BlockSpec((tm, tn), lambda j, i: (i, j)),
        out_shape=jax.ShapeDtypeStruct((m, n), BF16),
        scratch_shapes=[pltpu.VMEM((k, tn), BF16)] * 3,
        compiler_params=_params(2),
        name="merge_branches",
    )(a5, ag, ar, w5, wg, wr, proj, proj, proj)


def _residual_kernel(a_ref, w_ref, x_ref, o_ref, wb_ref):
    _stage_weights([(w_ref, wb_ref)])
    o_ref[...] = x_ref[...] + _dot(a_ref[...], wb_ref[...])


def residual_matmul(a, w, x, layer, tm, tn):
    m, k = a.shape
    n = w.shape[2]
    return pl.pallas_call(
        _residual_kernel,
        grid=(n // tn, m // tm),
        in_specs=[pl.BlockSpec((tm, k), lambda j, i: (i, 0)),
                  pl.BlockSpec((None, k, tn), lambda j, i: (layer, 0, j)),
                  pl.BlockSpec((tm, tn), lambda j, i: (i, j))],
        out_specs=pl.BlockSpec((tm, tn), lambda j, i: (i, j)),
        out_shape=jax.ShapeDtypeStruct((m, n), F32),
        scratch_shapes=[pltpu.VMEM((k, tn), BF16)],
        compiler_params=_params(2),
        name="residual_matmul",
    )(a, w, x)


def _ffn_up_kernel(a_ref, wg_ref, wu_ref, o_ref, wgb_ref, wub_ref):
    _stage_weights([(wg_ref, wgb_ref), (wu_ref, wub_ref)])
    a = a_ref[...]
    o_ref[...] = (_silu(_dot(a, wgb_ref[...])) * _dot(a, wub_ref[...])).astype(o_ref.dtype)


def ffn_up(h, wg, wu, layer, tm, tn):
    m, k = h.shape
    n = wg.shape[2]
    w_spec = pl.BlockSpec((None, k, tn), lambda j, i: (layer, 0, j))
    return pl.pallas_call(
        _ffn_up_kernel,
        grid=(n // tn, m // tm),
        in_specs=[pl.BlockSpec((tm, k), lambda j, i: (i, 0)), w_spec, w_spec],
        out_specs=pl.BlockSpec((tm, tn), lambda j, i: (i, j)),
        out_shape=jax.ShapeDtypeStruct((m, n), BF16),
        scratch_shapes=[pltpu.VMEM((k, tn), BF16)] * 2,
        compiler_params=_params(2),
        name="ffn_up",
    )(h, wg, wu)


def _s5_prep_kernel(ldt_ref, arc_ref, aic_ref, arr_ref, air_ref, b1_ref, crt_ref, cit_ref,
                    t_ref, p_ref, q_ref, bbt_ref, w0_ref):
    s = S5_CHUNK
    dt = jnp.exp(ldt_ref[0])
    ar_row, ai_row = arr_ref[0], air_ref[0]
    lam_r_row, lam_i_row = ar_row * dt, ai_row * dt
    lam_r_col, lam_i_col = arc_ref[0] * dt, aic_ref[0] * dt

    mag = jnp.exp(lam_r_row)
    abr, abi = mag * jnp.cos(lam_i_row), mag * jnp.sin(lam_i_row)
    den = ar_row * ar_row + ai_row * ai_row
    nr = abr - 1.0
    fr = (nr * ar_row + abi * ai_row) / den
    fi = (abi * ar_row - nr * ai_row) / den

    b1 = b1_ref[0]
    lane = lax.broadcasted_iota(jnp.int32, b1.shape, 1)
    b2 = pltpu.roll(b1, S5_STATE, axis=1) * jnp.where(lane < S5_STATE, -1.0, 1.0)
    bbt = fr * b1 + fi * b2
    bbt_sw = fr * b2 - fi * b1
    bbt_ref[0] = bbt

    rows = lax.broadcasted_iota(jnp.int32, (s * S5_GROUP, LANES), 0)
    tau_p = (s - 1 - jnp.right_shift(rows, 4)).astype(F32)
    mag_p = jnp.exp(lam_r_row * tau_p)
    ar_p, ai_p = mag_p * jnp.cos(lam_i_row * tau_p), mag_p * jnp.sin(lam_i_row * tau_p)
    p_ref[0] = (jnp.concatenate([bbt] * s, axis=0) * ar_p
                + jnp.concatenate([bbt_sw] * s, axis=0) * ai_p).astype(p_ref.dtype)

    crt, cit = crt_ref[0], cit_ref[0]
    lanes_t = jnp.right_shift(lax.broadcasted_iota(jnp.int32, crt.shape, 1), 4).astype(F32)

    def coeffs(tau):
        mag_t = jnp.exp(lam_r_col * tau)
        ar_t, ai_t = mag_t * jnp.cos(lam_i_col * tau), mag_t * jnp.sin(lam_i_col * tau)
        return jnp.concatenate([ar_t * crt - ai_t * cit, -(ar_t * cit + ai_t * crt)], axis=0)

    w0 = coeffs(lanes_t)
    w0_ref[0] = w0
    q_ref[0] = coeffs(lanes_t + 1.0).astype(q_ref.dtype)

    t0 = jnp.dot(bbt, w0, precision=HIGHEST, preferred_element_type=F32)
    lane_t = lax.broadcasted_iota(jnp.int32, t0.shape, 1)
    blocks = [t0]
    for i in range(1, s):
        blocks.append(jnp.where(lane_t >= i * S5_GROUP, pltpu.roll(t0, i * S5_GROUP, axis=1), 0.0))
    t_ref[0] = jnp.concatenate(blocks, axis=0).astype(t_ref.dtype)


def s5_prep(a_re, a_im, log_dt, b_re, b_im, c_re, c_im):
    g, p, n, s = S5_GROUPS, S5_STATE, S5_GROUP, S5_CHUNK
    dup = lambda a: jnp.concatenate([a, a], axis=-1)[:, None, :]
    b1 = jnp.concatenate([b_re.transpose(0, 2, 1), b_im.transpose(0, 2, 1)], axis=-1)
    crt = jnp.tile(c_re.transpose(0, 2, 1), (1, 1, s))
    cit = jnp.tile(c_im.transpose(0, 2, 1), (1, 1, s))
    spec = lambda *shape: pl.BlockSpec((1,) + shape, lambda i: (i, 0, 0))
    return pl.pallas_call(
        _s5_prep_kernel,
        grid=(g,),
        in_specs=[spec(1, 1), spec(p, 1), spec(p, 1), spec(1, 2 * p), spec(1, 2 * p),
                  spec(n, 2 * p), spec(p, s * n), spec(p, s * n)],
        out_specs=[spec(s * n, s * n), spec(s * n, 2 * p), spec(2 * p, s * n),
                   spec(n, 2 * p), spec(2 * p, s * n)],
        out_shape=[jax.ShapeDtypeStruct((g, s * n, s * n), BF16),
                   jax.ShapeDtypeStruct((g, s * n, 2 * p), BF16),
                   jax.ShapeDtypeStruct((g, 2 * p, s * n), BF16),
                   jax.ShapeDtypeStruct((g, n, 2 * p), F32),
                   jax.ShapeDtypeStruct((g, 2 * p, s * n), F32)],
        compiler_params=_params(1),
        name="s5_prep",
    )(log_dt[:, None, None], a_re[:, :, None], a_im[:, :, None], dup(a_re), dup(a_im), b1, crt, cit)


def _abar_power(ldt_ref, arr_ref, air_ref, power):
    dt = jnp.exp(ldt_ref[...])
    mag = jnp.exp(arr_ref[...] * dt * power)
    ang = air_ref[...] * dt * power
    lane = lax.broadcasted_iota(jnp.int32, mag.shape, 2)
    return mag * jnp.cos(ang), mag * jnp.sin(ang) * jnp.where(lane < S5_STATE, -1.0, 1.0)


def _complex_scale(x, a_r, a_i_signed):
    return a_r * x + a_i_signed * pltpu.roll(x, S5_STATE, axis=x.ndim - 1)


GROUPS_PER_BLOCK = LANES // S5_GROUP


def _s5_prompt_kernel(u_ref, t_ref, p_ref, q_ref, ldt_ref, arr_ref, air_ref, d_ref,
                      y_ref, fin_ref, ug_ref, z_ref, x_ref, yg_ref, *, bsz):
    s, n, gp = S5_CHUNK, S5_GROUP, GROUPS_PER_BLOCK
    rows = u_ref.shape[1]
    n_chunks = rows // bsz
    lane_block = jnp.right_shift(lax.broadcasted_iota(jnp.int32, (rows, LANES), 1), 4)

    for g in range(gp):
        for half in range(s // gp):
            acc = None
            for t8 in range(gp):
                piece = u_ref[half * gp + t8]
                shift = (n * (t8 - g)) % LANES
                if shift:
                    piece = pltpu.roll(piece, shift, axis=1)
                acc = piece if acc is None else jnp.where(lane_block == t8, piece, acc)
            ug_ref[g, :, half * LANES:(half + 1) * LANES] = acc.astype(BF16)

    for g in range(gp):
        z_ref[g] = _dot(ug_ref[g], p_ref[g])

    a_r, a_i = _abar_power(ldt_ref, arr_ref, air_ref, float(s))

    def step(j, x):
        idx = pl.ds(j, bsz, stride=n_chunks)
        x_ref[:, idx, :] = x
        return _complex_scale(x, a_r, a_i) + z_ref[:, idx, :]

    fin_ref[...] = lax.fori_loop(0, n_chunks, step, jnp.zeros(fin_ref.shape, F32))

    for g in range(gp):
        yg_ref[g] = _dot(ug_ref[g], t_ref[g]) + _dot(x_ref[g].astype(BF16), q_ref[g])

    for t in range(s):
        half, t8 = divmod(t, gp)
        acc = None
        for g in range(gp):
            piece = yg_ref[g, :, half * LANES:(half + 1) * LANES]
            shift = (n * (g - t8)) % LANES
            if shift:
                piece = pltpu.roll(piece, shift, axis=1)
            acc = piece if acc is None else jnp.where(lane_block == g, piece, acc)
        y_ref[t] = jax.nn.gelu(acc + d_ref[...] * u_ref[t])


def _steps_major(a, s):
    return a.reshape(a.shape[0] // s, s, a.shape[1]).swapaxes(0, 1)


def s5_prompt(u, bsz, length, t_mat, p_mat, q_mat, log_dt, a_re, a_im, d_skip):
    g, n, s, p2, gp = S5_GROUPS, S5_GROUP, S5_CHUNK, 2 * S5_STATE, GROUPS_PER_BLOCK
    rows = bsz * length // s
    dup = lambda a: jnp.concatenate([a, a], axis=-1)[:, None, :]
    gspec = lambda *shape: pl.BlockSpec((gp,) + shape, lambda k: (k, 0, 0))
    io_spec = pl.BlockSpec((s, rows, LANES), lambda k: (0, 0, k))
    return pl.pallas_call(
        functools.partial(_s5_prompt_kernel, bsz=bsz),
        grid=(g // gp,),
        in_specs=[io_spec,
                  gspec(s * n, s * n), gspec(s * n, p2), gspec(p2, s * n),
                  gspec(1, 1), gspec(1, p2), gspec(1, p2),
                  pl.BlockSpec((1, LANES), lambda k: (0, k))],
        out_specs=[io_spec, gspec(bsz, p2)],
        out_shape=[jax.ShapeDtypeStruct((s, rows, g * n), F32),
                   jax.ShapeDtypeStruct((g, bsz, p2), F32)],
        scratch_shapes=[pltpu.VMEM((gp, rows, s * n), BF16),
                        pltpu.VMEM((gp, rows, p2), F32),
                        pltpu.VMEM((gp, rows, p2), F32),
                        pltpu.VMEM((gp, rows, s * n), F32)],
        compiler_params=_params(1),
        name="s5_prompt",
    )(_steps_major(u, s), t_mat, p_mat, q_mat, log_dt[:, None, None], dup(a_re), dup(a_im),
      d_skip.reshape(1, g * n))


def _s5_sample_kernel(u_ref, xr_ref, xi_ref, bbt_ref, w0_ref, ldt_ref, arr_ref, air_ref, d_ref,
                      y_ref, nr_ref, ni_ref):
    p = S5_STATE
    dt = jnp.exp(ldt_ref[0])
    mag = jnp.exp(arr_ref[0][:, :p] * dt)
    ang = air_ref[0][:, :p] * dt
    abr, abi = mag * jnp.cos(ang), mag * jnp.sin(ang)
    u = u_ref[0]
    z = _dot(u.astype(BF16), bbt_ref[0].astype(BF16))
    xr, xi = xr_ref[0], xi_ref[0]
    new_r = abr * xr - abi * xi + z[:, :p]
    new_i = abr * xi + abi * xr + z[:, p:]
    nr_ref[0] = new_r
    ni_ref[0] = new_i
    c = w0_ref[0][:, :S5_GROUP].astype(BF16)
    y = _dot(new_r.astype(BF16), c[:p]) + _dot(new_i.astype(BF16), c[p:]) + d_ref[0] * u
    y_ref[0] = jax.nn.gelu(y)


def s5_sample(u, x_re, x_im, bbt, w0, log_dt, a_re, a_im, d_skip):
    g, n, p = S5_GROUPS, S5_GROUP, S5_STATE
    bsz = u.shape[0]
    dup = lambda a: jnp.concatenate([a, a], axis=-1)[:, None, :]
    gspec = lambda *shape: pl.BlockSpec((1,) + shape, lambda i: (i, 0, 0))
    y, new_r, new_i = pl.pallas_call(
        _s5_sample_kernel,
        grid=(g,),
        in_specs=[gspec(bsz, n), gspec(bsz, p), gspec(bsz, p), gspec(n, 2 * p),
                  gspec(2 * p, S5_CHUNK * n), gspec(1, 1), gspec(1, 2 * p), gspec(1, 2 * p),
                  gspec(1, n)],
        out_specs=[gspec(bsz, n), gspec(bsz, p), gspec(bsz, p)],
        out_shape=[jax.ShapeDtypeStruct((g, bsz, n), F32),
                   jax.ShapeDtypeStruct((g, bsz, p), F32),
                   jax.ShapeDtypeStruct((g, bsz, p), F32)],
        compiler_params=_params(1),
        name="s5_sample",
    )(u.reshape(bsz, g, n).transpose(1, 0, 2), x_re.transpose(1, 0, 2), x_im.transpose(1, 0, 2),
      bbt, w0, log_dt[:, None, None], dup(a_re), dup(a_im), d_skip.reshape(g, 1, n))
    return (y.transpose(1, 0, 2).reshape(bsz, g * n), new_r.transpose(1, 0, 2),
            new_i.transpose(1, 0, 2))


FACTORED_DECAY_LIMIT = -60.0


def _gla_kernel(q_ref, k_ref, v_ref, gr_ref, g_ref, nrm_ref, o_ref, st_ref, s_ref, b_ref):
    t = pl.program_id(1)

    @pl.when(t == 0)
    def _():
        s_ref[...] = jnp.zeros(s_ref.shape, F32)

    c = q_ref.shape[0]
    row = lax.broadcasted_iota(jnp.int32, (c, c), 0)
    col = lax.broadcasted_iota(jnp.int32, (c, c), 1)
    causal = col <= row
    ones_tri = causal.astype(F32)
    step = lax.broadcasted_iota(jnp.int32, (c, GLA_DK), 0)
    for h in range(GLA_HEADS):
        ks = slice(h * GLA_DK, (h + 1) * GLA_DK)
        vs = slice(h * GLA_DV, (h + 1) * GLA_DV)
        b = jnp.dot(ones_tri, g_ref[:, ks], precision=HIGHEST, preferred_element_type=F32)
        b_end = b[c - 1:c, :]
        q_in = (q_ref[:, ks] * (GLA_DK ** -0.5) * jnp.exp(b)).astype(BF16)
        k_end = (k_ref[:, ks] * jnp.exp(b_end - b)).astype(BF16)
        v = v_ref[:, vs].astype(BF16)

        def factored():
            k_in = (k_ref[:, ks] * jnp.exp(-b)).astype(BF16)
            scores = jnp.where(causal, lax.dot_general(q_in, k_in, NT, preferred_element_type=F32), 0.0)
            return _dot(scores.astype(BF16), v)

        def stepwise():
            b_ref[...] = b
            q = q_ref[:, ks] * (GLA_DK ** -0.5)

            def add_sources(tile, acc):
                first = pl.multiple_of(tile * SUBLANES, SUBLANES)
                rows8 = pl.ds(first, SUBLANES)
                b_s, k_s, v_s = b_ref[rows8, :], k_ref[rows8, ks], v_ref[rows8, vs]
                for r in range(SUBLANES):
                    decay = jnp.exp(jnp.where(step >= first + r, b - b_s[r:r + 1], -jnp.inf))
                    score = jnp.sum(q * k_s[r:r + 1] * decay, axis=1, keepdims=True)
                    acc = acc + score * v_s[r:r + 1]
                return acc

            return lax.fori_loop(0, c // SUBLANES, add_sources, jnp.zeros((c, GLA_DV), F32))

        o_intra = lax.cond(jnp.min(b_end) >= FACTORED_DECAY_LIMIT, factored, stepwise)
        s_t = s_ref[h]
        o = o_intra + lax.dot_general(q_in, s_t.astype(BF16), NT, preferred_element_type=F32)
        s_ref[h] = jnp.exp(b_end) * s_t + lax.dot_general(v, k_end, TN, preferred_element_type=F32)
        o = o * lax.rsqrt(jnp.mean(o * o, axis=-1, keepdims=True) + EPS) * nrm_ref[...]
        o_ref[:, vs] = (o * _silu(gr_ref[:, vs])).astype(o_ref.dtype)

    @pl.when(t == pl.num_programs(1) - 1)
    def _():
        st_ref[0] = s_ref[...]


def gla_prompt(proj, g, gla_norm, bsz, length):
    c = CHUNK
    nc = length // c
    hk, hv = GLA_HEADS * GLA_DK, GLA_HEADS * GLA_DV
    rows = lambda width, off: pl.BlockSpec((c, width), lambda b, t: (b * nc + t, off // width))
    o, st = pl.pallas_call(
        _gla_kernel,
        grid=(bsz, nc),
        in_specs=[rows(hk, OFF_GQ), rows(hk, OFF_GK), rows(hv, OFF_GV), rows(hv, OFF_GR),
                  rows(hk, 0), pl.BlockSpec((1, GLA_DV), lambda b, t: (0, 0))],
        out_specs=[rows(hv, 0),
                   pl.BlockSpec((1, GLA_HEADS, GLA_DV, GLA_DK), lambda b, t: (b, 0, 0, 0))],
        out_shape=[jax.ShapeDtypeStruct((bsz * length, hv), BF16),
                   jax.ShapeDtypeStruct((bsz, GLA_HEADS, GLA_DV, GLA_DK), F32)],
        scratch_shapes=[pltpu.VMEM((GLA_HEADS, GLA_DV, GLA_DK), F32),
                        pltpu.VMEM((c, GLA_DK), F32)],
        compiler_params=_params(2),
        name="gla_prompt",
    )(proj, proj, proj, proj, g, gla_norm.reshape(1, GLA_DV))
    return o, st.swapaxes(2, 3)


def _rotate_pairs(x, cos2, sin2):
    lane = lax.broadcasted_iota(jnp.int32, x.shape, 1)
    partner = jnp.where(lane % 2 == 0, pltpu.roll(x, x.shape[1] - 1, axis=1), pltpu.roll(x, 1, axis=1))
    return x * cos2 + partner * sin2


def _ret_kernel(q_ref, k_ref, v_ref, rg_ref, cos_ref, sin_ref, dm_ref, qd_ref, kd_ref, gc_ref,
                o_ref, st_ref, s_ref):
    t = pl.program_id(1)

    @pl.when(t == 0)
    def _():
        s_ref[...] = jnp.zeros(s_ref.shape, F32)

    cos2, sin2 = cos_ref[...], sin_ref[...]
    for h in range(RET_HEADS):
        ks = slice(h * RET_DK, (h + 1) * RET_DK)
        vs = slice(h * RET_DV, (h + 1) * RET_DV)
        q = _rotate_pairs(q_ref[:, ks], cos2, sin2)
        k = _rotate_pairs(k_ref[:, ks], cos2, sin2) * (RET_DK ** -0.5)
        v = v_ref[:, vs].astype(BF16)
        scores = lax.dot_general(q.astype(BF16), k.astype(BF16), NT, preferred_element_type=F32) * dm_ref[h]
        s_t = s_ref[h]
        o = (_dot(scores.astype(BF16), v)
             + lax.dot_general((q * qd_ref[h]).astype(BF16), s_t.astype(BF16), NT,
                               preferred_element_type=F32))
        s_ref[h] = gc_ref[h] * s_t + lax.dot_general(v, (k * kd_ref[h]).astype(BF16), TN,
                                                     preferred_element_type=F32)
        oc = o - jnp.mean(o, axis=-1, keepdims=True)
        o = oc * lax.rsqrt(jnp.mean(oc * oc, axis=-1, keepdims=True) + EPS)
        o_ref[:, vs] = (o * _silu(rg_ref[:, vs])).astype(o_ref.dtype)

    @pl.when(t == pl.num_programs(1) - 1)
    def _():
        st_ref[0] = s_ref[...]


def _rotary_tables(pos):
    half = RET_DK // 2
    inv = 1.0 / (ROPE_BASE ** jnp.linspace(0.0, 1.0, half, dtype=F32))
    ang = pos[:, None] * inv[None, :]
    cos2 = jnp.repeat(jnp.cos(ang), 2, axis=-1)
    sin2 = jnp.stack([-jnp.sin(ang), jnp.sin(ang)], axis=-1).reshape(pos.shape[0], RET_DK)
    return cos2, sin2


def _log_gamma():
    return jnp.log1p(-jnp.power(2.0, -5.0 - jnp.arange(RET_HEADS, dtype=F32)))


def ret_prompt(proj, bsz, length):
    c = CHUNK
    nc = length // c
    hk, hv = RET_HEADS * RET_DK, RET_HEADS * RET_DV
    cos2, sin2 = _rotary_tables(jnp.arange(length, dtype=F32))
    lg = _log_gamma()[:, None, None]
    steps = jnp.arange(c, dtype=F32)
    lag = steps[:, None] - steps[None, :]
    dmat = jnp.where(lag >= 0, jnp.exp(lg * jnp.maximum(lag, 0.0)), 0.0)
    wide = lambda a: jnp.broadcast_to(a, (RET_HEADS, a.shape[1], LANES))
    qdec = wide(jnp.exp(lg * (steps + 1.0)[None, :, None]))
    kdec = wide(jnp.exp(lg * (c - 1.0 - steps)[None, :, None]))
    gchunk = wide(jnp.exp(lg * float(c)))
    rows = lambda width, off: pl.BlockSpec((c, width), lambda b, t: (b * nc + t, off // width))
    table = pl.BlockSpec((c, RET_DK), lambda b, t: (t, 0))
    const = lambda a: pl.BlockSpec(a.shape, lambda b, t: (0, 0, 0))
    o, st = pl.pallas_call(
        _ret_kernel,
        grid=(bsz, nc),
        in_specs=[rows(hk, OFF_RQ), rows(hk, OFF_RK), rows(hv, OFF_RV), rows(hv, OFF_RG),
                  table, table, const(dmat), const(qdec), const(kdec), const(gchunk)],
        out_specs=[rows(hv, 0),
                   pl.BlockSpec((1, RET_HEADS, RET_DV, RET_DK), lambda b, t: (b, 0, 0, 0))],
        out_shape=[jax.ShapeDtypeStruct((bsz * length, hv), BF16),
                   jax.ShapeDtypeStruct((bsz, RET_HEADS, RET_DV, RET_DK), F32)],
        scratch_shapes=[pltpu.VMEM((RET_HEADS, RET_DV, RET_DK), F32)],
        compiler_params=_params(2),
        name="ret_prompt",
    )(proj, proj, proj, proj, cos2, sin2, dmat, qdec, kdec, gchunk)
    return o, st.swapaxes(2, 3)


def _one_column(a_t, i):
    col = lax.broadcasted_iota(jnp.int32, a_t.shape, 1)
    return jnp.where(col == i, a_t, 0.0).astype(BF16)


def _gla_sample_kernel(s_ref, q_ref, kt_ref, gt_ref, v_ref, gr_ref, nrm_ref, *rest):
    so_ref, o_ref = rest[-2:]
    ones = jnp.ones((SAMPLE_BLOCK, GLA_DV), BF16)
    for h in range(GLA_HEADS):
        ks = slice(h * GLA_DK, (h + 1) * GLA_DK)
        vs = slice(h * GLA_DV, (h + 1) * GLA_DV)
        q = (q_ref[:, ks] * (GLA_DK ** -0.5)).astype(BF16)
        v = v_ref[:, vs].astype(BF16)
        k_t = kt_ref[0, h]
        decay_t = jnp.exp(gt_ref[0, h])
        decay_hi = decay_t.astype(BF16).astype(F32)
        decay_lo = decay_t - decay_hi
        for i in range(SAMPLE_BLOCK):
            decay = _dot(_one_column(decay_hi, i), ones) + _dot(_one_column(decay_lo, i), ones)
            s_new = decay * s_ref[i, h] + _dot(_one_column(k_t, i), v)
            so_ref[i, h] = s_new
            o = _dot(q, s_new.astype(BF16))[i:i + 1]
            o = o * lax.rsqrt(jnp.mean(o * o, axis=-1, keepdims=True) + EPS) * nrm_ref[...]
            o_ref[i:i + 1, vs] = (o * _silu(gr_ref[i:i + 1, vs])).astype(o_ref.dtype)


def _columns(a, heads, dk):
    bsz = a.shape[0]
    return a.reshape(bsz // SAMPLE_BLOCK, SAMPLE_BLOCK, heads, dk).transpose(0, 2, 3, 1)


def _stacked_state_call(kernel_fn, name, state, layer, prev, heads, dk, dv, in_specs, operands,
                        o_spec, o_shape):
    bb = SAMPLE_BLOCK
    st_spec = pl.BlockSpec((None, bb, heads, dk, dv), lambda i: (layer, i, 0, 0, 0))
    aliases = {}
    if prev is not None:
        in_specs = in_specs + [pl.BlockSpec(memory_space=pl.ANY)]
        operands = operands + (prev,)
        aliases = {len(in_specs): 0}
    return pl.pallas_call(
        kernel_fn,
        grid=(state.shape[1] // bb,),
        in_specs=[st_spec] + in_specs,
        out_specs=[st_spec, o_spec],
        out_shape=[jax.ShapeDtypeStruct(state.shape, F32), o_shape],
        input_output_aliases=aliases,
        compiler_params=_params(1),
        name=name,
    )(state, *operands)


def gla_sample(proj, g, state, gla_norm, layer, prev):
    bsz = proj.shape[0]
    bb = SAMPLE_BLOCK
    hk, hv = GLA_HEADS * GLA_DK, GLA_HEADS * GLA_DV
    col_spec = pl.BlockSpec((1, GLA_HEADS, GLA_DK, bb), lambda i: (i, 0, 0, 0))
    rows = lambda off: pl.BlockSpec((bb, hv), lambda i: (i, off // hv))
    return _stacked_state_call(
        _gla_sample_kernel, "gla_sample", state, layer, prev, GLA_HEADS, GLA_DK, GLA_DV,
        [pl.BlockSpec((bb, hk), lambda i: (i, OFF_GQ // hk)), col_spec, col_spec, rows(OFF_GV),
         rows(OFF_GR), pl.BlockSpec((None, 1, GLA_DV), lambda i: (layer, 0, 0))],
        (proj, _columns(proj[:, OFF_GK:OFF_GK + hk], GLA_HEADS, GLA_DK), _columns(g, GLA_HEADS, GLA_DK),
         proj, proj, gla_norm.reshape(gla_norm.shape[0], 1, GLA_DV)),
        rows(0), jax.ShapeDtypeStruct((bsz, hv), BF16))


def _rot_kernel(q_ref, k_ref, cos_ref, sin_ref, qo_ref, ko_ref):
    cos2, sin2 = cos_ref[...], sin_ref[...]
    for h in range(RET_HEADS):
        ks = slice(h * RET_DK, (h + 1) * RET_DK)
        qo_ref[:, ks] = _rotate_pairs(q_ref[:, ks], cos2, sin2)
        ko_ref[:, ks] = _rotate_pairs(k_ref[:, ks], cos2, sin2) * (RET_DK ** -0.5)


def _ret_sample_kernel(s_ref, q_ref, kt_ref, gam_ref, v_ref, rg_ref, *rest):
    so_ref, o_ref = rest[-2:]
    for h in range(RET_HEADS):
        ks = slice(h * RET_DK, (h + 1) * RET_DK)
        vs = slice(h * RET_DV, (h + 1) * RET_DV)
        q = q_ref[:, ks].astype(BF16)
        v = v_ref[:, vs].astype(BF16)
        k_t = kt_ref[0, h]
        for i in range(SAMPLE_BLOCK):
            s_new = gam_ref[h] * s_ref[i, h] + _dot(_one_column(k_t, i), v)
            so_ref[i, h] = s_new
            o = _dot(q, s_new.astype(BF16))[i:i + 1]
            oc = o - jnp.mean(o, axis=-1, keepdims=True)
            o = oc * lax.rsqrt(jnp.mean(oc * oc, axis=-1, keepdims=True) + EPS)
            o_ref[i:i + 1, vs] = (o * _silu(rg_ref[i:i + 1, vs])).astype(o_ref.dtype)


def ret_sample(proj, state, layer, prev):
    bsz = proj.shape[0]
    bb = SAMPLE_BLOCK
    hk, hv = RET_HEADS * RET_DK, RET_HEADS * RET_DV
    cos2, sin2 = _rotary_tables(PAST_LEN + jnp.arange(1, dtype=F32))
    full = pl.BlockSpec((bsz, hk), lambda i: (0, 0))
    q_rot, k_rot = pl.pallas_call(
        _rot_kernel,
        grid=(1,),
        in_specs=[pl.BlockSpec((bsz, hk), lambda i: (0, OFF_RQ // hk)),
                  pl.BlockSpec((bsz, hk), lambda i: (0, OFF_RK // hk)),
                  pl.BlockSpec((1, RET_DK), lambda i: (0, 0)), pl.BlockSpec((1, RET_DK), lambda i: (0, 0))],
        out_specs=[full, full],
        out_shape=[jax.ShapeDtypeStruct((bsz, hk), F32)] * 2,
        compiler_params=_params(1),
        name="ret_rotary",
    )(proj, proj, cos2, sin2)
    gamma = jnp.broadcast_to(jnp.exp(_log_gamma())[:, None, None], (RET_HEADS, 1, LANES))
    col_spec = pl.BlockSpec((1, RET_HEADS, RET_DK, bb), lambda i: (i, 0, 0, 0))
    rows = lambda off: pl.BlockSpec((bb, hv), lambda i: (i, off // hv))
    return _stacked_state_call(
        _ret_sample_kernel, "ret_sample", state, layer, prev, RET_HEADS, RET_DK, RET_DV,
        [rows(0), col_spec, pl.BlockSpec(gamma.shape, lambda i: (0, 0, 0)), rows(OFF_RV), rows(OFF_RG)],
        (q_rot, _columns(k_rot, RET_HEADS, RET_DK), gamma, proj, proj),
        rows(0), jax.ShapeDtypeStruct((bsz, hv), BF16))


def _layer(x, layer, prompt_shape, states, w, s5_mats):
    t_mat, p_mat, q_mat, bbt, w0 = s5_mats
    m = x.shape[0]
    tm = min(m, 512)
    tn = 512
    s5_par = (w["s5_log_dt"][layer], w["s5_a_re"][layer], w["s5_a_im"][layer], w["s5_d"][layer])

    h = rmsnorm(x, w["norm_mix"][layer], BF16)
    proj = in_proj(h, w["w_in_rows"], layer, tm, 2 * tn)
    g = gla_gate(h, w["w_in_rows"], w["gla_w_gate"], w["gla_b_gate"], layer, tm)
    u = proj[:, OFF_U:OFF_U + S5_WIDTH]

    if prompt_shape is not None:
        bsz, length = prompt_shape
        y5, x_fin = s5_prompt(u, bsz, length, t_mat, p_mat, q_mat, *s5_par)
        y5 = y5.reshape(m, S5_WIDTH)
        x_fin = x_fin.transpose(1, 0, 2)
        new_states = (x_fin[..., :S5_STATE], x_fin[..., S5_STATE:])
        o_gla, new_gla = gla_prompt(proj, g, w["gla_norm"][layer], bsz, length)
        o_ret, new_ret = ret_prompt(proj, bsz, length)
    else:
        st_s5_re, st_s5_im, st_gla, st_ret, prev_gla, prev_ret = states
        y5, *new_states = s5_sample(u, st_s5_re[layer], st_s5_im[layer], bbt, w0, *s5_par)
        new_gla, o_gla = gla_sample(proj, g, st_gla, w["gla_norm"], layer, prev_gla)
        new_ret, o_ret = ret_sample(proj, st_ret, layer, prev_ret)

    a5 = s5_glu(y5, w["s5_w_glu"], w["s5_b_glu"], layer, tm, tn)
    if prompt_shape is not None:
        a5 = a5.reshape(S5_CHUNK, m // S5_CHUNK, S5_WIDTH).swapaxes(0, 1).reshape(m, S5_WIDTH)
    merged = merge_branches(a5, o_gla, o_ret, w["s5_w_out"], w["gla_w_out"], w["ret_w_out"], proj,
                            layer, tm, tn)
    x = residual_matmul(merged, w["w_mix_out"], x, layer, tm, tn)
    h2 = rmsnorm(x, w["norm_ffn"][layer], BF16)
    ff = ffn_up(h2, w["w_ffn_gate"], w["w_ffn_up"], layer, tm, tn)
    x = residual_matmul(ff, w["w_ffn_down"], x, layer, tm, tn)
    return x, new_states[0], new_states[1], new_gla, new_ret


def kernel(x_prompt, x_sample, state_s5_re, state_s5_im, state_gla, state_ret, norm_mix, w_in, s5_a_re, s5_a_im, s5_log_dt, s5_b_re, s5_b_im, s5_c_re, s5_c_im, s5_d, s5_w_glu, s5_b_glu, s5_w_out, gla_w_gate, gla_b_gate, gla_norm, gla_w_out, ret_w_out, w_mix_out, norm_ffn, w_ffn_gate, w_ffn_up, w_ffn_down, norm_final):
    bp, lp, d = x_prompt.shape
    bs = x_sample.shape[0]
    depth = w_in.shape[0]
    hp = x_prompt.reshape(bp * lp, d)
    hs = x_sample.reshape(bs, d)
    w = dict(norm_mix=norm_mix, s5_log_dt=s5_log_dt, s5_a_re=s5_a_re, s5_a_im=s5_a_im, s5_d=s5_d,
             s5_w_glu=s5_w_glu, s5_b_glu=s5_b_glu, s5_w_out=s5_w_out, gla_w_gate=gla_w_gate,
             gla_b_gate=gla_b_gate, gla_norm=gla_norm, gla_w_out=gla_w_out, ret_w_out=ret_w_out,
             w_mix_out=w_mix_out, norm_ffn=norm_ffn, w_ffn_gate=w_ffn_gate, w_ffn_up=w_ffn_up,
             w_ffn_down=w_ffn_down,
             w_in_rows=jnp.swapaxes(w_in, 1, 2).reshape(depth, w_in.shape[2] // GLA_RANK, GLA_RANK, d))
    outs_p, outs_s = [], []
    new_gla_s = new_ret_s = None
    for l in range(depth):
        s5_mats = s5_prep(s5_a_re[l], s5_a_im[l], s5_log_dt[l], s5_b_re[l], s5_b_im[l], s5_c_re[l],
                          s5_c_im[l])
        hp, *new_p = _layer(hp, l, (bp, lp), None, w, s5_mats)
        hs, s5r, s5i, new_gla_s, new_ret_s = _layer(
            hs, l, None, (state_s5_re, state_s5_im, state_gla, state_ret, new_gla_s, new_ret_s),
            w, s5_mats)
        outs_p.append(new_p)
        outs_s.append((s5r, s5i))
    y_prompt = rmsnorm(hp, norm_final, F32).reshape(bp, lp, d)
    y_sample = rmsnorm(hs, norm_final, F32).reshape(bs, 1, d)
    stack = lambda outs, i: jnp.stack([o[i] for o in outs])
    return (y_prompt, y_sample,
            stack(outs_p, 0), stack(outs_p, 1), stack(outs_p, 2), stack(outs_p, 3),
            stack(outs_s, 0), stack(outs_s, 1), new_gla_s, new_ret_s)
```

```python
import functools
import math

import jax
import jax.numpy as jnp
import numpy as np
from jax import lax
from jax.experimental import pallas as pl
from jax.experimental.pallas import tpu as pltpu

F32 = jnp.float32
BF16 = jnp.bfloat16
HIGHEST = lax.Precision.HIGHEST

D_MODEL = 2048
S5_WIDTH = D_MODEL // 2
S5_GROUP = 16
S5_GROUPS = S5_WIDTH // S5_GROUP
S5_STATE = 64
S5_CHUNK = 16
GLA_HEADS = 4
GLA_DK = D_MODEL // 16
GLA_DV = D_MODEL // 8
GLA_RANK = 16
GLA_GATE_NORM = 16.0
RET_HEADS = 8
RET_DK = D_MODEL // 16
RET_DV = D_MODEL // 16
ROPE_BASE = 10000.0
PAST_LEN = 16384
N_BRANCH = 3
D_FF = -(-8 * D_MODEL // (3 * 256)) * 256
EPS = 1e-6
CHUNK = 128
LANES = 128
SUBLANES = 8
SAMPLE_BLOCK = 16

OFF_U = 0
OFF_GQ = OFF_U + S5_WIDTH
OFF_GK = OFF_GQ + GLA_HEADS * GLA_DK
OFF_GV = OFF_GK + GLA_HEADS * GLA_DK
OFF_GR = OFF_GV + GLA_HEADS * GLA_DV
OFF_RQ = OFF_GR + GLA_HEADS * GLA_DV
OFF_RK = OFF_RQ + RET_HEADS * RET_DK
OFF_RV = OFF_RK + RET_HEADS * RET_DK
OFF_RG = OFF_RV + RET_HEADS * RET_DV
OFF_MG = OFF_RG + RET_HEADS * RET_DV
W_MAIN = OFF_MG + N_BRANCH * D_MODEL
GLR_COL = OFF_RQ

VMEM_LIMIT = 56 * 1024 * 1024

NT = (((1,), (1,)), ((), ()))
TN = (((0,), (0,)), ((), ()))


def _params(n_grid):
    return pltpu.CompilerParams(dimension_semantics=("arbitrary",) * n_grid,
                                vmem_limit_bytes=VMEM_LIMIT)


def _silu(x):
    return x * jax.nn.sigmoid(x)


def _rmsnorm_kernel(x_ref, g_ref, o_ref):
    x = x_ref[...]
    y = x * lax.rsqrt(jnp.mean(x * x, axis=-1, keepdims=True) + EPS)
    o_ref[...] = (y * g_ref[...]).astype(o_ref.dtype)


def rmsnorm(x, gain, out_dtype):
    m, d = x.shape
    tm = min(m, 512)
    return pl.pallas_call(
        _rmsnorm_kernel,
        grid=(m // tm,),
        in_specs=[pl.BlockSpec((tm, d), lambda i: (i, 0)),
                  pl.BlockSpec((1, d), lambda i: (0, 0))],
        out_specs=pl.BlockSpec((tm, d), lambda i: (i, 0)),
        out_shape=jax.ShapeDtypeStruct((m, d), out_dtype),
        compiler_params=_params(1),
        name="rmsnorm",
    )(x, gain.reshape(1, d))


def _stage_weights(pairs):
    @pl.when(pl.program_id(1) == 0)
    def _():
        for w_ref, wb_ref in pairs:
            wb_ref[...] = w_ref[...].astype(BF16)


def _dot(a, b):
    return jnp.dot(a, b, preferred_element_type=F32)


def _in_proj_kernel(a_ref, w_ref, o_ref, wb_ref):
    @pl.when(pl.program_id(1) == 0)
    def _():
        wb_ref[...] = w_ref[...].reshape(wb_ref.shape).astype(BF16)

    o_ref[...] = lax.dot_general(a_ref[...], wb_ref[...], NT, preferred_element_type=F32)


def in_proj(h, w_rows, layer, tm, tn):
    m, k = h.shape
    rb = tn // GLA_RANK
    skip_from = GLR_COL // tn

    def w_index(j, i):
        return (layer, rb * j + jnp.where(j >= skip_from, 1, 0), 0, 0)

    return pl.pallas_call(
        _in_proj_kernel,
        grid=(W_MAIN // tn, m // tm),
        in_specs=[pl.BlockSpec((tm, k), lambda j, i: (i, 0)),
                  pl.BlockSpec((pl.Element(1), pl.Element(rb), pl.Element(GLA_RANK), pl.Element(k)),
                               w_index)],
        out_specs=pl.BlockSpec((tm, tn), lambda j, i: (i, j)),
        out_shape=jax.ShapeDtypeStruct((m, W_MAIN), F32),
        scratch_shapes=[pltpu.VMEM((tn, k), BF16)],
        compiler_params=_params(2),
        name="in_proj",
    )(h, w_rows)


def _gla_gate_kernel(h_ref, wl_ref, wg_ref, b_ref, o_ref):
    pad = LANES - GLA_RANK
    wl = jnp.concatenate([wl_ref[0], jnp.zeros((pad, wl_ref.shape[2]), F32)], axis=0).astype(BF16)
    wg = jnp.concatenate([wg_ref[...], jnp.zeros((pad, wg_ref.shape[1]), F32)], axis=0).astype(BF16)
    glr = lax.dot_general(h_ref[...], wl, NT, preferred_element_type=F32)
    logit = _dot(glr.astype(BF16), wg) + b_ref[...]
    log_sig = jnp.minimum(logit, 0.0) - jnp.log1p(jnp.exp(-jnp.abs(logit)))
    o_ref[...] = log_sig / GLA_GATE_NORM


def gla_gate(h, w_rows, w_gate, b_gate, layer, tm):
    m, k = h.shape
    n = w_gate.shape[2]
    return pl.pallas_call(
        _gla_gate_kernel,
        grid=(m // tm,),
        in_specs=[pl.BlockSpec((tm, k), lambda i: (i, 0)),
                  pl.BlockSpec((None, 1, GLA_RANK, k), lambda i: (layer, GLR_COL // GLA_RANK, 0, 0)),
                  pl.BlockSpec((None, GLA_RANK, n), lambda i: (layer, 0, 0)),
                  pl.BlockSpec((None, 1, n), lambda i: (layer, 0, 0))],
        out_specs=pl.BlockSpec((tm, n), lambda i: (i, 0)),
        out_shape=jax.ShapeDtypeStruct((m, n), F32),
        compiler_params=_params(1),
        name="gla_gate",
    )(h, w_rows, w_gate, b_gate.reshape(b_gate.shape[0], 1, n))


def _s5_glu_kernel(a_ref, w_ref, y_ref, b_ref, o_ref, wb_ref):
    _stage_weights([(w_ref, wb_ref)])
    acc = _dot(a_ref[...].astype(BF16), wb_ref[...])
    o_ref[...] = (y_ref[...] * jax.nn.sigmoid(acc + b_ref[...])).astype(o_ref.dtype)


def s5_glu(y5, w, b, layer, tm, tn):
    m, k = y5.shape
    n = w.shape[2]
    return pl.pallas_call(
        _s5_glu_kernel,
        grid=(n // tn, m // tm),
        in_specs=[pl.BlockSpec((tm, k), lambda j, i: (i, 0)),
                  pl.BlockSpec((None, k, tn), lambda j, i: (layer, 0, j)),
                  pl.BlockSpec((tm, tn), lambda j, i: (i, j)),
                  pl.BlockSpec((None, 1, tn), lambda j, i: (layer, 0, j))],
        out_specs=pl.BlockSpec((tm, tn), lambda j, i: (i, j)),
        out_shape=jax.ShapeDtypeStruct((m, n), BF16),
        scratch_shapes=[pltpu.VMEM((k, tn), BF16)],
        compiler_params=_params(2),
        name="s5_glu",
    )(y5, w, y5, b.reshape(b.shape[0], 1, n))


def _merge_kernel(a5_ref, ag_ref, ar_ref, w5_ref, wg_ref, wr_ref, m5_ref, mg_ref, mr_ref,
                  o_ref, w5b_ref, wgb_ref, wrb_ref):
    _stage_weights([(w5_ref, w5b_ref), (wg_ref, wgb_ref), (wr_ref, wrb_ref)])
    merged = (jax.nn.sigmoid(m5_ref[...]) * _dot(a5_ref[...], w5b_ref[...])
              + jax.nn.sigmoid(mg_ref[...]) * _dot(ag_ref[...], wgb_ref[...])
              + jax.nn.sigmoid(mr_ref[...]) * _dot(ar_ref[...], wrb_ref[...]))
    o_ref[...] = merged.astype(o_ref.dtype)


def merge_branches(a5, ag, ar, w5, wg, wr, proj, layer, tm, tn):
    m, k = a5.shape
    n = w5.shape[2]
    a_spec = pl.BlockSpec((tm, k), lambda j, i: (i, 0))
    w_spec = pl.BlockSpec((None, k, tn), lambda j, i: (layer, 0, j))

    def gate_spec(branch):
        first = (OFF_MG + branch * D_MODEL) // tn
        return pl.BlockSpec((tm, tn), lambda j, i: (i, first + j))

    return pl.pallas_call(
        _merge_kernel,
        grid=(n // tn, m // tm),
        in_specs=[a_spec, a_spec, a_spec, w_spec, w_spec, w_spec,
                  gate_spec(0), gate_spec(1), gate_spec(2)],
        out_specs=pl.BlockSpec((tm, tn), lambda j, i: (i, j)),
        out_shape=jax.ShapeDtypeStruct((m, n), BF16),
        scratch_shapes=[pltpu.VMEM((k, tn), BF16)] * 3,
        compiler_params=_params(2),
        name="merge_branches",
    )(a5, ag, ar, w5, wg, wr, proj, proj, proj)


def _residual_kernel(a_ref, w_ref, x_ref, o_ref, wb_ref):
    _stage_weights([(w_ref, wb_ref)])
    o_ref[...] = x_ref[...] + _dot(a_ref[...], wb_ref[...])


def residual_matmul(a, w, x, layer, tm, tn):
    m, k = a.shape
    n = w.shape[2]
    return pl.pallas_call(
        _residual_kernel,
        grid=(n // tn, m // tm),
        in_specs=[pl.BlockSpec((tm, k), lambda j, i: (i, 0)),
                  pl.BlockSpec((None, k, tn), lambda j, i: (layer, 0, j)),
                  pl.BlockSpec((tm, tn), lambda j, i: (i, j))],
        out_specs=pl.BlockSpec((tm, tn), lambda j, i: (i, j)),
        out_shape=jax.ShapeDtypeStruct((m, n), F32),
        scratch_shapes=[pltpu.VMEM((k, tn), BF16)],
        compiler_params=_params(2),
        name="residual_matmul",
    )(a, w, x)


def _ffn_up_kernel(a_ref, wg_ref, wu_ref, o_ref, wgb_ref, wub_ref):
    _stage_weights([(wg_ref, wgb_ref), (wu_ref, wub_ref)])
    a = a_ref[...]
    o_ref[...] = (_silu(_dot(a, wgb_ref[...])) * _dot(a, wub_ref[...])).astype(o_ref.dtype)


def ffn_up(h, wg, wu, layer, tm, tn):
    m, k = h.shape
    n = wg.shape[2]
    w_spec = pl.BlockSpec((None, k, tn), lambda j, i: (layer, 0, j))
    return pl.pallas_call(
        _ffn_up_kernel,
        grid=(n // tn, m // tm),
        in_specs=[pl.BlockSpec((tm, k), lambda j, i: (i, 0)), w_spec, w_spec],
        out_specs=pl.BlockSpec((tm, tn), lambda j, i: (i, j)),
        out_shape=jax.ShapeDtypeStruct((m, n), BF16),
        scratch_shapes=[pltpu.VMEM((k, tn), BF16)] * 2,
        compiler_params=_params(2),
        name="ffn_up",
    )(h, wg, wu)


def _s5_prep_kernel(ldt_ref, arc_ref, aic_ref, arr_ref, air_ref, b1_ref, crt_ref, cit_ref,
                    t_ref, p_ref, q_ref, bbt_ref, w0_ref):
    s = S5_CHUNK
    dt = jnp.exp(ldt_ref[0])
    ar_row, ai_row = arr_ref[0], air_ref[0]
    lam_r_row, lam_i_row = ar_row * dt, ai_row * dt
    lam_r_col, lam_i_col = arc_ref[0] * dt, aic_ref[0] * dt

    mag = jnp.exp(lam_r_row)
    abr, abi = mag * jnp.cos(lam_i_row), mag * jnp.sin(lam_i_row)
    den = ar_row * ar_row + ai_row * ai_row
    nr = abr - 1.0
    fr = (nr * ar_row + abi * ai_row) / den
    fi = (abi * ar_row - nr * ai_row) / den

    b1 = b1_ref[0]
    lane = lax.broadcasted_iota(jnp.int32, b1.shape, 1)
    b2 = pltpu.roll(b1, S5_STATE, axis=1) * jnp.where(lane < S5_STATE, -1.0, 1.0)
    bbt = fr * b1 + fi * b2
    bbt_sw = fr * b2 - fi * b1
    bbt_ref[0] = bbt

    tau_p = (s - 1 - lax.broadcasted_iota(jnp.int32, (s, LANES), 0)).astype(F32)
    mag_p = jnp.exp(lam_r_row * tau_p)
    ar_p, ai_p = mag_p * jnp.cos(lam_i_row * tau_p), mag_p * jnp.sin(lam_i_row * tau_p)
    repeat_rows = lambda a: jnp.concatenate(
        [jnp.broadcast_to(a[i:i + 1], (S5_GROUP, LANES)) for i in range(s)], axis=0)
    p_ref[0] = (jnp.concatenate([bbt] * s, axis=0) * repeat_rows(ar_p)
                + jnp.concatenate([bbt_sw] * s, axis=0) * repeat_rows(ai_p)).astype(p_ref.dtype)

    crt, cit = crt_ref[0], cit_ref[0]
    power = jnp.minimum(lax.broadcasted_iota(jnp.int32, (S5_STATE, LANES), 1), s).astype(F32)
    mag_c = jnp.exp(lam_r_col * power)
    ar_c, ai_c = mag_c * jnp.cos(lam_i_col * power), mag_c * jnp.sin(lam_i_col * power)
    src = lax.broadcasted_iota(jnp.int32, (LANES, s * S5_GROUP), 0)
    dst_block = jnp.right_shift(lax.broadcasted_iota(jnp.int32, (LANES, s * S5_GROUP), 1), 4)

    def coeffs(tau):
        spread = (src == dst_block + tau).astype(F32)
        ar_t = jnp.dot(ar_c, spread, precision=HIGHEST, preferred_element_type=F32)
        ai_t = jnp.dot(ai_c, spread, precision=HIGHEST, preferred_element_type=F32)
        return jnp.concatenate([ar_t * crt - ai_t * cit, -(ar_t * cit + ai_t * crt)], axis=0)

    w0 = coeffs(0)
    w0_ref[0] = w0
    q_ref[0] = coeffs(1).astype(q_ref.dtype)

    t0 = jnp.dot(bbt, w0, precision=HIGHEST, preferred_element_type=F32)
    lane_t = lax.broadcasted_iota(jnp.int32, t0.shape, 1)
    blocks = [t0]
    for i in range(1, s):
        blocks.append(jnp.where(lane_t >= i * S5_GROUP, pltpu.roll(t0, i * S5_GROUP, axis=1), 0.0))
    t_ref[0] = jnp.concatenate(blocks, axis=0).astype(t_ref.dtype)


def s5_prep(a_re, a_im, log_dt, b_re, b_im, c_re, c_im):
    g, p, n, s = S5_GROUPS, S5_STATE, S5_GROUP, S5_CHUNK
    dup = lambda a: jnp.concatenate([a, a], axis=-1)[:, None, :]
    b1 = jnp.concatenate([b_re.transpose(0, 2, 1), b_im.transpose(0, 2, 1)], axis=-1)
    crt = jnp.tile(c_re.transpose(0, 2, 1), (1, 1, s))
    cit = jnp.tile(c_im.transpose(0, 2, 1), (1, 1, s))
    spec = lambda *shape: pl.BlockSpec((1,) + shape, lambda i: (i, 0, 0))
    return pl.pallas_call(
        _s5_prep_kernel,
        grid=(g,),
        in_specs=[spec(1, 1), spec(p, 1), spec(p, 1), spec(1, 2 * p), spec(1, 2 * p),
                  spec(n, 2 * p), spec(p, s * n), spec(p, s * n)],
        out_specs=[spec(s * n, s * n), spec(s * n, 2 * p), spec(2 * p, s * n),
                   spec(n, 2 * p), spec(2 * p, s * n)],
        out_shape=[jax.ShapeDtypeStruct((g, s * n, s * n), BF16),
                   jax.ShapeDtypeStruct((g, s * n, 2 * p), BF16),
                   jax.ShapeDtypeStruct((g, 2 * p, s * n), BF16),
                   jax.ShapeDtypeStruct((g, n, 2 * p), F32),
                   jax.ShapeDtypeStruct((g, 2 * p, s * n), F32)],
        compiler_params=_params(1),
        name="s5_prep",
    )(log_dt[:, None, None], a_re[:, :, None], a_im[:, :, None], dup(a_re), dup(a_im), b1, crt, cit)


def _abar_power(ldt_ref, arr_ref, air_ref, power):
    dt = jnp.exp(ldt_ref[...])
    mag = jnp.exp(arr_ref[...] * dt * power)
    ang = air_ref[...] * dt * power
    lane = lax.broadcasted_iota(jnp.int32, mag.shape, 2)
    return mag * jnp.cos(ang), mag * jnp.sin(ang) * jnp.where(lane < S5_STATE, -1.0, 1.0)


def _complex_scale(x, a_r, a_i_signed):
    return a_r * x + a_i_signed * pltpu.roll(x, S5_STATE, axis=x.ndim - 1)


GROUPS_PER_BLOCK = LANES // S5_GROUP


def _s5_prompt_kernel(u_ref, t_ref, p_ref, q_ref, ldt_ref, arr_ref, air_ref, d_ref,
                      y_ref, fin_ref, ug_ref, z_ref, x_ref, yg_ref, *, bsz):
    s, n, gp = S5_CHUNK, S5_GROUP, GROUPS_PER_BLOCK
    rows = u_ref.shape[1]
    n_chunks = rows // bsz
    lane_block = jnp.right_shift(lax.broadcasted_iota(jnp.int32, (rows, LANES), 1), 4)

    for g in range(gp):
        for half in range(s // gp):
            acc = None
            for t8 in range(gp):
                piece = u_ref[half * gp + t8]
                shift = (n * (t8 - g)) % LANES
                if shift:
                    piece = pltpu.roll(piece, shift, axis=1)
                acc = piece if acc is None else jnp.where(lane_block == t8, piece, acc)
            ug_ref[g, :, half * LANES:(half + 1) * LANES] = acc.astype(BF16)

    for g in range(gp):
        z_ref[g] = _dot(ug_ref[g], p_ref[g])

    a_r, a_i = _abar_power(ldt_ref, arr_ref, air_ref, float(s))

    def step(j, x):
        idx = pl.ds(j, bsz, stride=n_chunks)
        x_ref[:, idx, :] = x
        return _complex_scale(x, a_r, a_i) + z_ref[:, idx, :]

    fin_ref[...] = lax.fori_loop(0, n_chunks, step, jnp.zeros(fin_ref.shape, F32))

    for g in range(gp):
        yg_ref[g] = _dot(ug_ref[g], t_ref[g]) + _dot(x_ref[g].astype(BF16), q_ref[g])

    for t in range(s):
        half, t8 = divmod(t, gp)
        acc = None
        for g in range(gp):
            piece = yg_ref[g, :, half * LANES:(half + 1) * LANES]
            shift = (n * (g - t8)) % LANES
            if shift:
                piece = pltpu.roll(piece, shift, axis=1)
            acc = piece if acc is None else jnp.where(lane_block == g, piece, acc)
        y_ref[t] = jax.nn.gelu(acc + d_ref[...] * u_ref[t])


def _steps_major(a, s):
    return a.reshape(a.shape[0] // s, s, a.shape[1]).swapaxes(0, 1)


def s5_prompt(u, bsz, length, t_mat, p_mat, q_mat, log_dt, a_re, a_im, d_skip):
    g, n, s, p2, gp = S5_GROUPS, S5_GROUP, S5_CHUNK, 2 * S5_STATE, GROUPS_PER_BLOCK
    rows = bsz * length // s
    dup = lambda a: jnp.concatenate([a, a], axis=-1)[:, None, :]
    gspec = lambda *shape: pl.BlockSpec((gp,) + shape, lambda k: (k, 0, 0))
    io_spec = pl.BlockSpec((s, rows, LANES), lambda k: (0, 0, k))
    return pl.pallas_call(
        functools.partial(_s5_prompt_kernel, bsz=bsz),
        grid=(g // gp,),
        in_specs=[io_spec,
                  gspec(s * n, s * n), gspec(s * n, p2), gspec(p2, s * n),
                  gspec(1, 1), gspec(1, p2), gspec(1, p2),
                  pl.BlockSpec((1, LANES), lambda k: (0, k))],
        out_specs=[io_spec, gspec(bsz, p2)],
        out_shape=[jax.ShapeDtypeStruct((s, rows, g * n), F32),
                   jax.ShapeDtypeStruct((g, bsz, p2), F32)],
        scratch_shapes=[pltpu.VMEM((gp, rows, s * n), BF16),
                        pltpu.VMEM((gp, rows, p2), F32),
                        pltpu.VMEM((gp, rows, p2), F32),
                        pltpu.VMEM((gp, rows, s * n), F32)],
        compiler_params=_params(1),
        name="s5_prompt",
    )(_steps_major(u, s), t_mat, p_mat, q_mat, log_dt[:, None, None], dup(a_re), dup(a_im),
      d_skip.reshape(1, g * n))


def _s5_sample_kernel(u_ref, xr_ref, xi_ref, bbt_ref, w0_ref, ldt_ref, arr_ref, air_ref, d_ref,
                      y_ref, nr_ref, ni_ref):
    p = S5_STATE
    dt = jnp.exp(ldt_ref[0])
    mag = jnp.exp(arr_ref[0][:, :p] * dt)
    ang = air_ref[0][:, :p] * dt
    abr, abi = mag * jnp.cos(ang), mag * jnp.sin(ang)
    u = u_ref[0]
    z = _dot(u.astype(BF16), bbt_ref[0].astype(BF16))
    xr, xi = xr_ref[0], xi_ref[0]
    new_r = abr * xr - abi * xi + z[:, :p]
    new_i = abr * xi + abi * xr + z[:, p:]
    nr_ref[0] = new_r
    ni_ref[0] = new_i
    c = w0_ref[0][:, :S5_GROUP].astype(BF16)
    y = _dot(new_r.astype(BF16), c[:p]) + _dot(new_i.astype(BF16), c[p:]) + d_ref[0] * u
    y_ref[0] = jax.nn.gelu(y)


def s5_sample(u, x_re, x_im, bbt, w0, log_dt, a_re, a_im, d_skip):
    g, n, p = S5_GROUPS, S5_GROUP, S5_STATE
    bsz = u.shape[0]
    dup = lambda a: jnp.concatenate([a, a], axis=-1)[:, None, :]
    gspec = lambda *shape: pl.BlockSpec((1,) + shape, lambda i: (i, 0, 0))
    y, new_r, new_i = pl.pallas_call(
        _s5_sample_kernel,
        grid=(g,),
        in_specs=[gspec(bsz, n), gspec(bsz, p), gspec(bsz, p), gspec(n, 2 * p),
                  gspec(2 * p, S5_CHUNK * n), gspec(1, 1), gspec(1, 2 * p), gspec(1, 2 * p),
                  gspec(1, n)],
        out_specs=[gspec(bsz, n), gspec(bsz, p), gspec(bsz, p)],
        out_shape=[jax.ShapeDtypeStruct((g, bsz, n), F32),
                   jax.ShapeDtypeStruct((g, bsz, p), F32),
                   jax.ShapeDtypeStruct((g, bsz, p), F32)],
        compiler_params=_params(1),
        name="s5_sample",
    )(u.reshape(bsz, g, n).transpose(1, 0, 2), x_re.transpose(1, 0, 2), x_im.transpose(1, 0, 2),
      bbt, w0, log_dt[:, None, None], dup(a_re), dup(a_im), d_skip.reshape(g, 1, n))
    return (y.transpose(1, 0, 2).reshape(bsz, g * n), new_r.transpose(1, 0, 2),
            new_i.transpose(1, 0, 2))


FACTORED_DECAY_LIMIT = -60.0


def _gla_kernel(q_ref, k_ref, v_ref, gr_ref, g_ref, nrm_ref, o_ref, st_ref, s_ref, b_ref,
                qb_ref, kb_ref, kend_ref, sc_ref, acc_ref):
    t = pl.program_id(0)

    @pl.when(t == 0)
    def _():
        s_ref[...] = jnp.zeros(s_ref.shape, F32)

    bsz, c = q_ref.shape[0], q_ref.shape[1]
    row = lax.broadcasted_iota(jnp.int32, (c, c), 0)
    col = lax.broadcasted_iota(jnp.int32, (c, c), 1)
    causal = col <= row
    ones_tri = causal.astype(F32)
    step = lax.broadcasted_iota(jnp.int32, (c, GLA_DK), 0)

    lowest = None
    for i in range(bsz):
        b_ref[i] = jnp.dot(ones_tri, g_ref[i], precision=HIGHEST, preferred_element_type=F32)
        end_min = jnp.min(b_ref[i, c - 1:c, :])
        lowest = end_min if lowest is None else jnp.minimum(lowest, end_min)
    factorable = lowest >= FACTORED_DECAY_LIMIT

    pairs = [(i, h) for i in range(bsz) for h in range(GLA_HEADS)]
    head_k = lambda h: slice(h * GLA_DK, (h + 1) * GLA_DK)
    head_v = lambda h: slice(h * GLA_DV, (h + 1) * GLA_DV)

    def finish(i, h, o):
        o = o * lax.rsqrt(jnp.mean(o * o, axis=-1, keepdims=True) + EPS) * nrm_ref[...]
        o_ref[i, :, head_v(h)] = (o * _silu(gr_ref[i, :, head_v(h)])).astype(o_ref.dtype)

    def all_chunks_factored():
        for n, (i, h) in enumerate(pairs):
            b = b_ref[i, :, head_k(h)]
            k = k_ref[i, :, head_k(h)]
            qb_ref[n] = (q_ref[i, :, head_k(h)] * (GLA_DK ** -0.5) * jnp.exp(b)).astype(BF16)
            kb_ref[n] = (k * jnp.exp(-b)).astype(BF16)
            kend_ref[n] = (k * jnp.exp(b[c - 1:c, :] - b)).astype(BF16)
        for n, (i, h) in enumerate(pairs):
            scores = lax.dot_general(qb_ref[n], kb_ref[n], NT, preferred_element_type=F32)
            sc_ref[n] = jnp.where(causal, scores, 0.0).astype(BF16)
        for n, (i, h) in enumerate(pairs):
            v = v_ref[i, :, head_v(h)].astype(BF16)
            s_t = s_ref[i, h]
            acc_ref[n] = _dot(sc_ref[n], v) + lax.dot_general(qb_ref[n], s_t.astype(BF16), NT,
                                                              preferred_element_type=F32)
            s_ref[i, h] = (jnp.exp(b_ref[i, c - 1:c, head_k(h)]) * s_t
                           + lax.dot_general(v, kend_ref[n], TN, preferred_element_type=F32))
        for n, (i, h) in enumerate(pairs):
            finish(i, h, acc_ref[n])

    def intra_stepwise(i, ks, vs, b, q_in, v):
        q = q_ref[i, :, ks] * (GLA_DK ** -0.5)

        def add_sources(tile, acc):
            first = pl.multiple_of(tile * SUBLANES, SUBLANES)
            rows8 = pl.ds(first, SUBLANES)
            b_s, k_s, v_s = b_ref[i, rows8, ks], k_ref[i, rows8, ks], v_ref[i, rows8, vs]
            for r in range(SUBLANES):
                decay = jnp.exp(jnp.where(step >= first + r, b - b_s[r:r + 1], -jnp.inf))
                score = jnp.sum(q * k_s[r:r + 1] * decay, axis=1, keepdims=True)
                acc = acc + score * v_s[r:r + 1]
            return acc

        return lax.fori_loop(0, c // SUBLANES, add_sources, jnp.zeros((c, GLA_DV), F32))

    def all_chunks_stepwise():
        for i, h in pairs:
            ks, vs = head_k(h), head_v(h)
            b = b_ref[i, :, ks]
            b_end = b[c - 1:c, :]
            q_in = (q_ref[i, :, ks] * (GLA_DK ** -0.5) * jnp.exp(b)).astype(BF16)
            k_end = (k_ref[i, :, ks] * jnp.exp(b_end - b)).astype(BF16)
            v = v_ref[i, :, vs].astype(BF16)
            s_t = s_ref[i, h]
            o = (intra_stepwise(i, ks, vs, b, q_in, v)
                 + lax.dot_general(q_in, s_t.astype(BF16), NT, preferred_element_type=F32))
            s_ref[i, h] = jnp.exp(b_end) * s_t + lax.dot_general(v, k_end, TN,
                                                                 preferred_element_type=F32)
            finish(i, h, o)

    pl.when(factorable)(all_chunks_factored)
    pl.when(jnp.logical_not(factorable))(all_chunks_stepwise)

    @pl.when(t == pl.num_programs(0) - 1)
    def _():
        st_ref[...] = s_ref[...]


def gla_prompt(proj, g, gla_norm, bsz, length):
    c = CHUNK
    nc = length // c
    hk, hv = GLA_HEADS * GLA_DK, GLA_HEADS * GLA_DV
    rows = lambda width, off: pl.BlockSpec((bsz, c, width), lambda t: (0, t, off // width))
    st_shape = (bsz, GLA_HEADS, GLA_DV, GLA_DK)
    proj3 = proj.reshape(bsz, length, proj.shape[1])
    o, st = pl.pallas_call(
        _gla_kernel,
        grid=(nc,),
        in_specs=[rows(hk, OFF_GQ), rows(hk, OFF_GK), rows(hv, OFF_GV), rows(hv, OFF_GR),
                  rows(hk, 0), pl.BlockSpec((1, GLA_DV), lambda t: (0, 0))],
        out_specs=[rows(hv, 0), pl.BlockSpec(st_shape, lambda t: (0, 0, 0, 0))],
        out_shape=[jax.ShapeDtypeStruct((bsz, length, hv), BF16),
                   jax.ShapeDtypeStruct(st_shape, F32)],
        scratch_shapes=[pltpu.VMEM(st_shape, F32),
                        pltpu.VMEM((bsz, c, hk), F32)]
        + [pltpu.VMEM((bsz * GLA_HEADS, c, GLA_DK), BF16)] * 3
        + [pltpu.VMEM((bsz * GLA_HEADS, c, c), BF16),
           pltpu.VMEM((bsz * GLA_HEADS, c, GLA_DV), F32)],
        compiler_params=_params(1),
        name="gla_prompt",
    )(proj3, proj3, proj3, proj3, g.reshape(bsz, length, hk), gla_norm.reshape(1, GLA_DV))
    return o.reshape(bsz * length, hv), st.swapaxes(2, 3)


def _rotate_pairs(x, cos2, sin2):
    lane = lax.broadcasted_iota(jnp.int32, x.shape, 1)
    partner = jnp.where(lane % 2 == 0, pltpu.roll(x, x.shape[1] - 1, axis=1), pltpu.roll(x, 1, axis=1))
    return x * cos2 + partner * sin2


def _ret_kernel(q_ref, k_ref, v_ref, rg_ref, cos_ref, sin_ref, dm_ref, qd_ref, kd_ref, gc_ref,
                o_ref, st_ref, s_ref, qb_ref, kb_ref, qdec_ref, kdec_ref, sc_ref, acc_ref):
    t = pl.program_id(0)

    @pl.when(t == 0)
    def _():
        s_ref[...] = jnp.zeros(s_ref.shape, F32)

    cos2, sin2 = cos_ref[...], sin_ref[...]
    pairs = [(i, h) for i in range(q_ref.shape[0]) for h in range(RET_HEADS)]
    head_k = lambda h: slice(h * RET_DK, (h + 1) * RET_DK)
    head_v = lambda h: slice(h * RET_DV, (h + 1) * RET_DV)
    for n, (i, h) in enumerate(pairs):
        q = _rotate_pairs(q_ref[i, :, head_k(h)], cos2, sin2)
        k = _rotate_pairs(k_ref[i, :, head_k(h)], cos2, sin2) * (RET_DK ** -0.5)
        qb_ref[n] = q.astype(BF16)
        kb_ref[n] = k.astype(BF16)
        qdec_ref[n] = (q * qd_ref[h]).astype(BF16)
        kdec_ref[n] = (k * kd_ref[h]).astype(BF16)
    for n, (i, h) in enumerate(pairs):
        scores = lax.dot_general(qb_ref[n], kb_ref[n], NT, preferred_element_type=F32) * dm_ref[h]
        sc_ref[n] = scores.astype(BF16)
    for n, (i, h) in enumerate(pairs):
        v = v_ref[i, :, head_v(h)].astype(BF16)
        s_t = s_ref[i, h]
        acc_ref[n] = _dot(sc_ref[n], v) + lax.dot_general(qdec_ref[n], s_t.astype(BF16), NT,
                                                          preferred_element_type=F32)
        s_ref[i, h] = gc_ref[h] * s_t + lax.dot_general(v, kdec_ref[n], TN, preferred_element_type=F32)
    for n, (i, h) in enumerate(pairs):
        o = acc_ref[n]
        oc = o - jnp.mean(o, axis=-1, keepdims=True)
        o = oc * lax.rsqrt(jnp.mean(oc * oc, axis=-1, keepdims=True) + EPS)
        o_ref[i, :, head_v(h)] = (o * _silu(rg_ref[i, :, head_v(h)])).astype(o_ref.dtype)

    @pl.when(t == pl.num_programs(0) - 1)
    def _():
        st_ref[...] = s_ref[...]


def _rotary_tables(pos):
    half = RET_DK // 2
    inv = 1.0 / (ROPE_BASE ** jnp.linspace(0.0, 1.0, half, dtype=F32))
    ang = pos[:, None] * inv[None, :]
    cos2 = jnp.repeat(jnp.cos(ang), 2, axis=-1)
    sin2 = jnp.stack([-jnp.sin(ang), jnp.sin(ang)], axis=-1).reshape(pos.shape[0], RET_DK)
    return cos2, sin2


def _log_gamma():
    return jnp.log1p(-jnp.power(2.0, -5.0 - jnp.arange(RET_HEADS, dtype=F32)))


def ret_prompt(proj, bsz, length):
    c = CHUNK
    nc = length // c
    hk, hv = RET_HEADS * RET_DK, RET_HEADS * RET_DV
    cos2, sin2 = _rotary_tables(jnp.arange(length, dtype=F32))
    lg = _log_gamma()[:, None, None]
    steps = jnp.arange(c, dtype=F32)
    lag = steps[:, None] - steps[None, :]
    dmat = jnp.where(lag >= 0, jnp.exp(lg * jnp.maximum(lag, 0.0)), 0.0)
    wide = lambda a: jnp.broadcast_to(a, (RET_HEADS, a.shape[1], LANES))
    qdec = wide(jnp.exp(lg * (steps + 1.0)[None, :, None]))
    kdec = wide(jnp.exp(lg * (c - 1.0 - steps)[None, :, None]))
    gchunk = wide(jnp.exp(lg * float(c)))
    rows = lambda width, off: pl.BlockSpec((bsz, c, width), lambda t: (0, t, off // width))
    table = pl.BlockSpec((c, RET_DK), lambda t: (t, 0))
    const = lambda a: pl.BlockSpec(a.shape, lambda t: (0, 0, 0))
    st_shape = (bsz, RET_HEADS, RET_DV, RET_DK)
    proj3 = proj.reshape(bsz, length, proj.shape[1])
    o, st = pl.pallas_call(
        _ret_kernel,
        grid=(nc,),
        in_specs=[rows(hk, OFF_RQ), rows(hk, OFF_RK), rows(hv, OFF_RV), rows(hv, OFF_RG),
                  table, table, const(dmat), const(qdec), const(kdec), const(gchunk)],
        out_specs=[rows(hv, 0), pl.BlockSpec(st_shape, lambda t: (0, 0, 0, 0))],
        out_shape=[jax.ShapeDtypeStruct((bsz, length, hv), BF16),
                   jax.ShapeDtypeStruct(st_shape, F32)],
        scratch_shapes=[pltpu.VMEM(st_shape, F32)]
        + [pltpu.VMEM((bsz * RET_HEADS, c, RET_DK), BF16)] * 4
        + [pltpu.VMEM((bsz * RET_HEADS, c, c), BF16),
           pltpu.VMEM((bsz * RET_HEADS, c, RET_DV), F32)],
        compiler_params=_params(1),
        name="ret_prompt",
    )(proj3, proj3, proj3, proj3, cos2, sin2, dmat, qdec, kdec, gchunk)
    return o.reshape(bsz * length, hv), st.swapaxes(2, 3)


def _one_column(a_t, i):
    col = lax.broadcasted_iota(jnp.int32, a_t.shape, 1)
    return jnp.where(col == i, a_t, 0.0).astype(BF16)


def _gla_sample_kernel(s_ref, q_ref, kt_ref, gt_ref, v_ref, gr_ref, nrm_ref, *rest):
    so_ref, o_ref = rest[-2:]
    ones = jnp.ones((SAMPLE_BLOCK, GLA_DV), BF16)
    for h in range(GLA_HEADS):
        ks = slice(h * GLA_DK, (h + 1) * GLA_DK)
        vs = slice(h * GLA_DV, (h + 1) * GLA_DV)
        q = (q_ref[:, ks] * (GLA_DK ** -0.5)).astype(BF16)
        v = v_ref[:, vs].astype(BF16)
        k_t = kt_ref[0, h]
        decay_t = jnp.exp(gt_ref[0, h])
        decay_hi = decay_t.astype(BF16).astype(F32)
        decay_lo = decay_t - decay_hi
        for i in range(SAMPLE_BLOCK):
            decay = _dot(_one_column(decay_hi, i), ones) + _dot(_one_column(decay_lo, i), ones)
            s_new = decay * s_ref[i, h] + _dot(_one_column(k_t, i), v)
            so_ref[i, h] = s_new
            o = _dot(q, s_new.astype(BF16))[i:i + 1]
            o = o * lax.rsqrt(jnp.mean(o * o, axis=-1, keepdims=True) + EPS) * nrm_ref[...]
            o_ref[i:i + 1, vs] = (o * _silu(gr_ref[i:i + 1, vs])).astype(o_ref.dtype)


def _columns(a, heads, dk):
    bsz = a.shape[0]
    return a.reshape(bsz // SAMPLE_BLOCK, SAMPLE_BLOCK, heads, dk).transpose(0, 2, 3, 1)


def _stacked_state_call(kernel_fn, name, state, layer, prev, heads, dk, dv, in_specs, operands,
                        o_spec, o_shape):
    bb = SAMPLE_BLOCK
    st_spec = pl.BlockSpec((None, bb, heads, dk, dv), lambda i: (layer, i, 0, 0, 0))
    aliases = {}
    if prev is not None:
        in_specs = in_specs + [pl.BlockSpec(memory_space=pl.ANY)]
        operands = operands + (prev,)
        aliases = {len(in_specs): 0}
    return pl.pallas_call(
        kernel_fn,
        grid=(state.shape[1] // bb,),
        in_specs=[st_spec] + in_specs,
        out_specs=[st_spec, o_spec],
        out_shape=[jax.ShapeDtypeStruct(state.shape, F32), o_shape],
        input_output_aliases=aliases,
        compiler_params=_params(1),
        name=name,
    )(state, *operands)


def gla_sample(proj, g, state, gla_norm, layer, prev):
    bsz = proj.shape[0]
    bb = SAMPLE_BLOCK
    hk, hv = GLA_HEADS * GLA_DK, GLA_HEADS * GLA_DV
    col_spec = pl.BlockSpec((1, GLA_HEADS, GLA_DK, bb), lambda i: (i, 0, 0, 0))
    rows = lambda off: pl.BlockSpec((bb, hv), lambda i: (i, off // hv))
    return _stacked_state_call(
        _gla_sample_kernel, "gla_sample", state, layer, prev, GLA_HEADS, GLA_DK, GLA_DV,
        [pl.BlockSpec((bb, hk), lambda i: (i, OFF_GQ // hk)), col_spec, col_spec, rows(OFF_GV),
         rows(OFF_GR), pl.BlockSpec((None, 1, GLA_DV), lambda i: (layer, 0, 0))],
        (proj, _columns(proj[:, OFF_GK:OFF_GK + hk], GLA_HEADS, GLA_DK), _columns(g, GLA_HEADS, GLA_DK),
         proj, proj, gla_norm.reshape(gla_norm.shape[0], 1, GLA_DV)),
        rows(0), jax.ShapeDtypeStruct((bsz, hv), BF16))


def _rot_kernel(q_ref, k_ref, cos_ref, sin_ref, qo_ref, ko_ref):
    cos2, sin2 = cos_ref[...], sin_ref[...]
    for h in range(RET_HEADS):
        ks = slice(h * RET_DK, (h + 1) * RET_DK)
        qo_ref[:, ks] = _rotate_pairs(q_ref[:, ks], cos2, sin2)
        ko_ref[:, ks] = _rotate_pairs(k_ref[:, ks], cos2, sin2) * (RET_DK ** -0.5)


def _ret_sample_kernel(s_ref, q_ref, kt_ref, gam_ref, v_ref, rg_ref, *rest):
    so_ref, o_ref = rest[-2:]
    for h in range(RET_HEADS):
        ks = slice(h * RET_DK, (h + 1) * RET_DK)
        vs = slice(h * RET_DV, (h + 1) * RET_DV)
        q = q_ref[:, ks].astype(BF16)
        v = v_ref[:, vs].astype(BF16)
        k_t = kt_ref[0, h]
        for i in range(SAMPLE_BLOCK):
            s_new = gam_ref[h] * s_ref[i, h] + _dot(_one_column(k_t, i), v)
            so_ref[i, h] = s_new
            o = _dot(q, s_new.astype(BF16))[i:i + 1]
            oc = o - jnp.mean(o, axis=-1, keepdims=True)
            o = oc * lax.rsqrt(jnp.mean(oc * oc, axis=-1, keepdims=True) + EPS)
            o_ref[i:i + 1, vs] = (o * _silu(rg_ref[i:i + 1, vs])).astype(o_ref.dtype)


def ret_sample(proj, state, layer, prev):
    bsz = proj.shape[0]
    bb = SAMPLE_BLOCK
    hk, hv = RET_HEADS * RET_DK, RET_HEADS * RET_DV
    cos2, sin2 = _rotary_tables(PAST_LEN + jnp.arange(1, dtype=F32))
    full = pl.BlockSpec((bsz, hk), lambda i: (0, 0))
    q_rot, k_rot = pl.pallas_call(
        _rot_kernel,
        grid=(1,),
        in_specs=[pl.BlockSpec((bsz, hk), lambda i: (0, OFF_RQ // hk)),
                  pl.BlockSpec((bsz, hk), lambda i: (0, OFF_RK // hk)),
                  pl.BlockSpec((1, RET_DK), lambda i: (0, 0)), pl.BlockSpec((1, RET_DK), lambda i: (0, 0))],
        out_specs=[full, full],
        out_shape=[jax.ShapeDtypeStruct((bsz, hk), F32)] * 2,
        compiler_params=_params(1),
        name="ret_rotary",
    )(proj, proj, cos2, sin2)
    gamma = jnp.broadcast_to(jnp.exp(_log_gamma())[:, None, None], (RET_HEADS, 1, LANES))
    col_spec = pl.BlockSpec((1, RET_HEADS, RET_DK, bb), lambda i: (i, 0, 0, 0))
    rows = lambda off: pl.BlockSpec((bb, hv), lambda i: (i, off // hv))
    return _stacked_state_call(
        _ret_sample_kernel, "ret_sample", state, layer, prev, RET_HEADS, RET_DK, RET_DV,
        [rows(0), col_spec, pl.BlockSpec(gamma.shape, lambda i: (0, 0, 0)), rows(OFF_RV), rows(OFF_RG)],
        (q_rot, _columns(k_rot, RET_HEADS, RET_DK), gamma, proj, proj),
        rows(0), jax.ShapeDtypeStruct((bsz, hv), BF16))


def _layer(x, layer, prompt_shape, states, w, s5_mats):
    t_mat, p_mat, q_mat, bbt, w0 = s5_mats
    m = x.shape[0]
    tm = min(m, 512)
    tn = 512
    s5_par = (w["s5_log_dt"][layer], w["s5_a_re"][layer], w["s5_a_im"][layer], w["s5_d"][layer])

    h = rmsnorm(x, w["norm_mix"][layer], BF16)
    proj = in_proj(h, w["w_in_rows"], layer, tm, 2 * tn)
    g = gla_gate(h, w["w_in_rows"], w["gla_w_gate"], w["gla_b_gate"], layer, tm)
    u = proj[:, OFF_U:OFF_U + S5_WIDTH]

    if prompt_shape is not None:
        bsz, length = prompt_shape
        y5, x_fin = s5_prompt(u, bsz, length, t_mat, p_mat, q_mat, *s5_par)
        y5 = y5.reshape(m, S5_WIDTH)
        x_fin = x_fin.transpose(1, 0, 2)
        new_states = (x_fin[..., :S5_STATE], x_fin[..., S5_STATE:])
        o_gla, new_gla = gla_prompt(proj, g, w["gla_norm"][layer], bsz, length)
        o_ret, new_ret = ret_prompt(proj, bsz, length)
    else:
        st_s5_re, st_s5_im, st_gla, st_ret, prev_gla, prev_ret = states
        y5, *new_states = s5_sample(u, st_s5_re[layer], st_s5_im[layer], bbt, w0, *s5_par)
        new_gla, o_gla = gla_sample(proj, g, st_gla, w["gla_norm"], layer, prev_gla)
        new_ret, o_ret = ret_sample(proj, st_ret, layer, prev_ret)

    a5 = s5_glu(y5, w["s5_w_glu"], w["s5_b_glu"], layer, tm, 2 * tn)
    if prompt_shape is not None:
        a5 = a5.reshape(S5_CHUNK, m // S5_CHUNK, S5_WIDTH).swapaxes(0, 1).reshape(m, S5_WIDTH)
    merged = merge_branches(a5, o_gla, o_ret, w["s5_w_out"], w["gla_w_out"], w["ret_w_out"], proj,
                            layer, tm, 2 * tn)
    x = residual_matmul(merged, w["w_mix_out"], x, layer, tm, 2 * tn)
    h2 = rmsnorm(x, w["norm_ffn"][layer], BF16)
    ff = ffn_up(h2, w["w_ffn_gate"], w["w_ffn_up"], layer, tm, tn)
    x = residual_matmul(ff, w["w_ffn_down"], x, layer, tm, tn)
    return x, new_states[0], new_states[1], new_gla, new_ret


def kernel(x_prompt, x_sample, state_s5_re, state_s5_im, state_gla, state_ret, norm_mix, w_in, s5_a_re, s5_a_im, s5_log_dt, s5_b_re, s5_b_im, s5_c_re, s5_c_im, s5_d, s5_w_glu, s5_b_glu, s5_w_out, gla_w_gate, gla_b_gate, gla_norm, gla_w_out, ret_w_out, w_mix_out, norm_ffn, w_ffn_gate, w_ffn_up, w_ffn_down, norm_final):
    bp, lp, d = x_prompt.shape
    bs = x_sample.shape[0]
    depth = w_in.shape[0]
    hp = x_prompt.reshape(bp * lp, d)
    hs = x_sample.reshape(bs, d)
    w = dict(norm_mix=norm_mix, s5_log_dt=s5_log_dt, s5_a_re=s5_a_re, s5_a_im=s5_a_im, s5_d=s5_d,
             s5_w_glu=s5_w_glu, s5_b_glu=s5_b_glu, s5_w_out=s5_w_out, gla_w_gate=gla_w_gate,
             gla_b_gate=gla_b_gate, gla_norm=gla_norm, gla_w_out=gla_w_out, ret_w_out=ret_w_out,
             w_mix_out=w_mix_out, norm_ffn=norm_ffn, w_ffn_gate=w_ffn_gate, w_ffn_up=w_ffn_up,
             w_ffn_down=w_ffn_down,
             w_in_rows=jnp.swapaxes(w_in, 1, 2).reshape(depth, w_in.shape[2] // GLA_RANK, GLA_RANK, d))
    outs_p, outs_s = [], []
    new_gla_s = new_ret_s = None
    for l in range(depth):
        s5_mats = s5_prep(s5_a_re[l], s5_a_im[l], s5_log_dt[l], s5_b_re[l], s5_b_im[l], s5_c_re[l],
                          s5_c_im[l])
        hp, *new_p = _layer(hp, l, (bp, lp), None, w, s5_mats)
        hs, s5r, s5i, new_gla_s, new_ret_s = _layer(
            hs, l, None, (state_s5_re, state_s5_im, state_gla, state_ret, new_gla_s, new_ret_s),
            w, s5_mats)
        outs_p.append(new_p)
        outs_s.append((s5r, s5i))
    y_prompt = rmsnorm(hp, norm_final, F32).reshape(bp, lp, d)
    y_sample = rmsnorm(hs, norm_final, F32).reshape(bs, 1, d)
    stack = lambda outs, i: jnp.stack([o[i] for o in outs])
    return (y_prompt, y_sample,
            stack(outs_p, 0), stack(outs_p, 1), stack(outs_p, 2), stack(outs_p, 3),
            stack(outs_s, 0), stack(outs_s, 1), new_gla_s, new_ret_s)
```

```python
import functools
import math

import jax
import jax.numpy as jnp
import numpy as np
from jax import lax
from jax.experimental import pallas as pl
from jax.experimental.pallas import tpu as pltpu

F32 = jnp.float32
BF16 = jnp.bfloat16
HIGHEST = lax.Precision.HIGHEST

D_MODEL = 2048
S5_WIDTH = D_MODEL // 2
S5_GROUP = 16
S5_GROUPS = S5_WIDTH // S5_GROUP
S5_STATE = 64
S5_CHUNK = 16
GLA_HEADS = 4
GLA_DK = D_MODEL // 16
GLA_DV = D_MODEL // 8
GLA_RANK = 16
GLA_GATE_NORM = 16.0
RET_HEADS = 8
RET_DK = D_MODEL // 16
RET_DV = D_MODEL // 16
ROPE_BASE = 10000.0
PAST_LEN = 16384
N_BRANCH = 3
D_FF = -(-8 * D_MODEL // (3 * 256)) * 256
EPS = 1e-6
CHUNK = 128
LANES = 128
ROW_TILE = 16
SAMPLE_BLOCK = 16

OFF_U = 0
OFF_GQ = OFF_U + S5_WIDTH
OFF_GK = OFF_GQ + GLA_HEADS * GLA_DK
OFF_GV = OFF_GK + GLA_HEADS * GLA_DK
OFF_GR = OFF_GV + GLA_HEADS * GLA_DV
OFF_RQ = OFF_GR + GLA_HEADS * GLA_DV
OFF_RK = OFF_RQ + RET_HEADS * RET_DK
OFF_RV = OFF_RK + RET_HEADS * RET_DK
OFF_RG = OFF_RV + RET_HEADS * RET_DV
OFF_MG = OFF_RG + RET_HEADS * RET_DV
W_MAIN = OFF_MG + N_BRANCH * D_MODEL
GLR_COL = OFF_RQ

VMEM_LIMIT = 56 * 1024 * 1024

NT = (((1,), (1,)), ((), ()))
TN = (((0,), (0,)), ((), ()))


def _params(n_grid):
    return pltpu.CompilerParams(dimension_semantics=("arbitrary",) * n_grid,
                                vmem_limit_bytes=VMEM_LIMIT)


def _silu(x):
    return x * jax.nn.sigmoid(x)


def _rmsnorm_kernel(x_ref, g_ref, o_ref):
    x = x_ref[...]
    y = x * lax.rsqrt(jnp.mean(x * x, axis=-1, keepdims=True) + EPS)
    o_ref[...] = (y * g_ref[...]).astype(o_ref.dtype)


def rmsnorm(x, gain, out_dtype):
    m, d = x.shape
    tm = min(m, 512)
    return pl.pallas_call(
        _rmsnorm_kernel,
        grid=(m // tm,),
        in_specs=[pl.BlockSpec((tm, d), lambda i: (i, 0)),
                  pl.BlockSpec((1, d), lambda i: (0, 0))],
        out_specs=pl.BlockSpec((tm, d), lambda i: (i, 0)),
        out_shape=jax.ShapeDtypeStruct((m, d), out_dtype),
        compiler_params=_params(1),
        name="rmsnorm",
    )(x, gain.reshape(1, d))


def _stage_weights(pairs):
    @pl.when(pl.program_id(1) == 0)
    def _():
        for w_ref, wb_ref in pairs:
            wb_ref[...] = w_ref[...].astype(BF16)


def _dot(a, b):
    return jnp.dot(a, b, preferred_element_type=F32)


def _in_proj_kernel(a_ref, w_ref, o_ref, wb_ref):
    @pl.when(pl.program_id(1) == 0)
    def _():
        wb_ref[...] = w_ref[...].reshape(wb_ref.shape).astype(BF16)

    o_ref[...] = lax.dot_general(a_ref[...], wb_ref[...], NT,
                                 preferred_element_type=F32).astype(o_ref.dtype)


def in_proj(h, w_rows, layer, tm, tn):
    m, k = h.shape
    rb = tn // GLA_RANK
    skip_from = GLR_COL // tn

    def w_index(j, i):
        return (layer, rb * j + jnp.where(j >= skip_from, 1, 0), 0, 0)

    return pl.pallas_call(
        _in_proj_kernel,
        grid=(W_MAIN // tn, m // tm),
        in_specs=[pl.BlockSpec((tm, k), lambda j, i: (i, 0)),
                  pl.BlockSpec((pl.Element(1), pl.Element(rb), pl.Element(GLA_RANK), pl.Element(k)),
                               w_index)],
        out_specs=pl.BlockSpec((tm, tn), lambda j, i: (i, j)),
        out_shape=jax.ShapeDtypeStruct((m, W_MAIN), BF16),
        scratch_shapes=[pltpu.VMEM((tn, k), BF16)],
        compiler_params=_params(2),
        name="in_proj",
    )(h, w_rows)


def _gla_gate_kernel(h_ref, wl_ref, wg_ref, b_ref, o_ref):
    pad = LANES - GLA_RANK
    wl = jnp.concatenate([wl_ref[0], jnp.zeros((pad, wl_ref.shape[2]), F32)], axis=0).astype(BF16)
    wg = jnp.concatenate([wg_ref[...], jnp.zeros((pad, wg_ref.shape[1]), F32)], axis=0).astype(BF16)
    glr = lax.dot_general(h_ref[...], wl, NT, preferred_element_type=F32)
    logit = _dot(glr.astype(BF16), wg) + b_ref[...]
    log_sig = jnp.minimum(logit, 0.0) - jnp.log1p(jnp.exp(-jnp.abs(logit)))
    o_ref[...] = log_sig / GLA_GATE_NORM


def gla_gate(h, w_rows, w_gate, b_gate, layer, tm):
    m, k = h.shape
    n = w_gate.shape[2]
    return pl.pallas_call(
        _gla_gate_kernel,
        grid=(m // tm,),
        in_specs=[pl.BlockSpec((tm, k), lambda i: (i, 0)),
                  pl.BlockSpec((None, 1, GLA_RANK, k), lambda i: (layer, GLR_COL // GLA_RANK, 0, 0)),
                  pl.BlockSpec((None, GLA_RANK, n), lambda i: (layer, 0, 0)),
                  pl.BlockSpec((None, 1, n), lambda i: (layer, 0, 0))],
        out_specs=pl.BlockSpec((tm, n), lambda i: (i, 0)),
        out_shape=jax.ShapeDtypeStruct((m, n), F32),
        compiler_params=_params(1),
        name="gla_gate",
    )(h, w_rows, w_gate, b_gate.reshape(b_gate.shape[0], 1, n))


def _s5_glu_kernel(a_ref, w_ref, y_ref, b_ref, o_ref, wb_ref):
    _stage_weights([(w_ref, wb_ref)])
    acc = _dot(a_ref[...].astype(BF16), wb_ref[...])
    o_ref[...] = (y_ref[...] * jax.nn.sigmoid(acc + b_ref[...])).astype(o_ref.dtype)


def s5_glu(y5, w, b, layer, tm, tn):
    m, k = y5.shape
    n = w.shape[2]
    return pl.pallas_call(
        _s5_glu_kernel,
        grid=(n // tn, m // tm),
        in_specs=[pl.BlockSpec((tm, k), lambda j, i: (i, 0)),
                  pl.BlockSpec((None, k, tn), lambda j, i: (layer, 0, j)),
                  pl.BlockSpec((tm, tn), lambda j, i: (i, j)),
                  pl.BlockSpec((None, 1, tn), lambda j, i: (layer, 0, j))],
        out_specs=pl.BlockSpec((tm, tn), lambda j, i: (i, j)),
        out_shape=jax.ShapeDtypeStruct((m, n), BF16),
        scratch_shapes=[pltpu.VMEM((k, tn), BF16)],
        compiler_params=_params(2),
        name="s5_glu",
    )(y5, w, y5, b.reshape(b.shape[0], 1, n))


def _merge_kernel(a5_ref, ag_ref, ar_ref, w5_ref, wg_ref, wr_ref, m5_ref, mg_ref, mr_ref,
                  o_ref, w5b_ref, wgb_ref, wrb_ref):
    _stage_weights([(w5_ref, w5b_ref), (wg_ref, wgb_ref), (wr_ref, wrb_ref)])
    gate = lambda ref: jax.nn.sigmoid(ref[...].astype(F32))
    merged = (gate(m5_ref) * _dot(a5_ref[...], w5b_ref[...])
              + gate(mg_ref) * _dot(ag_ref[...], wgb_ref[...])
              + gate(mr_ref) * _dot(ar_ref[...], wrb_ref[...]))
    o_ref[...] = merged.astype(o_ref.dtype)


def merge_branches(a5, ag, ar, w5, wg, wr, proj, layer, tm, tn):
    m, k = a5.shape
    n = w5.shape[2]
    a_spec = pl.BlockSpec((tm, k), lambda j, i: (i, 0))
    w_spec = pl.BlockSpec((None, k, tn), lambda j, i: (layer, 0, j))

    def gate_spec(branch):
        first = (OFF_MG + branch * D_MODEL) // tn
        return pl.BlockSpec((tm, tn), lambda j, i: (i, first + j))

    return pl.pallas_call(
        _merge_kernel,
        grid=(n // tn, m // tm),
        in_specs=[a_spec, a_spec, a_spec, w_spec, w_spec, w_spec,
                  gate_spec(0), gate_spec(1), gate_spec(2)],
        out_specs=pl.BlockSpec((tm, tn), lambda j, i: (i, j)),
        out_shape=jax.ShapeDtypeStruct((m, n), BF16),
        scratch_shapes=[pltpu.VMEM((k, tn), BF16)] * 3,
        compiler_params=_params(2),
        name="merge_branches",
    )(a5, ag, ar, w5, wg, wr, proj, proj, proj)


def _residual_kernel(a_ref, w_ref, x_ref, o_ref, wb_ref):
    _stage_weights([(w_ref, wb_ref)])
    o_ref[...] = x_ref[...] + _dot(a_ref[...], wb_ref[...])


def residual_matmul(a, w, x, layer, tm, tn):
    m, k = a.shape
    n = w.shape[2]
    return pl.pallas_call(
        _residual_kernel,
        grid=(n // tn, m // tm),
        in_specs=[pl.BlockSpec((tm, k), lambda j, i: (i, 0)),
                  pl.BlockSpec((None, k, tn), lambda j, i: (layer, 0, j)),
                  pl.BlockSpec((tm, tn), lambda j, i: (i, j))],
        out_specs=pl.BlockSpec((tm, tn), lambda j, i: (i, j)),
        out_shape=jax.ShapeDtypeStruct((m, n), F32),
        scratch_shapes=[pltpu.VMEM((k, tn), BF16)],
        compiler_params=_params(2),
        name="residual_matmul",
    )(a, w, x)


def _ffn_up_kernel(a_ref, wg_ref, wu_ref, o_ref, wgb_ref, wub_ref):
    _stage_weights([(wg_ref, wgb_ref), (wu_ref, wub_ref)])
    a = a_ref[...]
    o_ref[...] = (_silu(_dot(a, wgb_ref[...])) * _dot(a, wub_ref[...])).astype(o_ref.dtype)


def ffn_up(h, wg, wu, layer, tm, tn):
    m, k = h.shape
    n = wg.shape[2]
    w_spec = pl.BlockSpec((None, k, tn), lambda j, i: (layer, 0, j))
    return pl.pallas_call(
        _ffn_up_kernel,
        grid=(n // tn, m // tm),
        in_specs=[pl.BlockSpec((tm, k), lambda j, i: (i, 0)), w_spec, w_spec],
        out_specs=pl.BlockSpec((tm, tn), lambda j, i: (i, j)),
        out_shape=jax.ShapeDtypeStruct((m, n), BF16),
        scratch_shapes=[pltpu.VMEM((k, tn), BF16)] * 2,
        compiler_params=_params(2),
        name="ffn_up",
    )(h, wg, wu)


def _s5_prep_kernel(ldt_ref, arc_ref, aic_ref, arr_ref, air_ref, b1_ref, crt_ref, cit_ref,
                    t_ref, p_ref, q_ref, bbt_ref, w0_ref):
    s = S5_CHUNK
    dt = jnp.exp(ldt_ref[0])
    ar_row, ai_row = arr_ref[0], air_ref[0]
    lam_r_row, lam_i_row = ar_row * dt, ai_row * dt
    lam_r_col, lam_i_col = arc_ref[0] * dt, aic_ref[0] * dt

    mag = jnp.exp(lam_r_row)
    abr, abi = mag * jnp.cos(lam_i_row), mag * jnp.sin(lam_i_row)
    den = ar_row * ar_row + ai_row * ai_row
    nr = abr - 1.0
    fr = (nr * ar_row + abi * ai_row) / den
    fi = (abi * ar_row - nr * ai_row) / den

    b1 = b1_ref[0]
    lane = lax.broadcasted_iota(jnp.int32, b1.shape, 1)
    b2 = pltpu.roll(b1, S5_STATE, axis=1) * jnp.where(lane < S5_STATE, -1.0, 1.0)
    bbt = fr * b1 + fi * b2
    bbt_sw = fr * b2 - fi * b1
    bbt_ref[0] = bbt

    tau_p = (s - 1 - lax.broadcasted_iota(jnp.int32, (s, LANES), 0)).astype(F32)
    mag_p = jnp.exp(lam_r_row * tau_p)
    ar_p, ai_p = mag_p * jnp.cos(lam_i_row * tau_p), mag_p * jnp.sin(lam_i_row * tau_p)
    repeat_rows = lambda a: jnp.concatenate(
        [jnp.broadcast_to(a[i:i + 1], (S5_GROUP, LANES)) for i in range(s)], axis=0)
    p_ref[0] = (jnp.concatenate([bbt] * s, axis=0) * repeat_rows(ar_p)
                + jnp.concatenate([bbt_sw] * s, axis=0) * repeat_rows(ai_p)).astype(p_ref.dtype)

    crt, cit = crt_ref[0], cit_ref[0]
    power = jnp.minimum(lax.broadcasted_iota(jnp.int32, (S5_STATE, LANES), 1), s).astype(F32)
    mag_c = jnp.exp(lam_r_col * power)
    ar_c, ai_c = mag_c * jnp.cos(lam_i_col * power), mag_c * jnp.sin(lam_i_col * power)
    src = lax.broadcasted_iota(jnp.int32, (LANES, s * S5_GROUP), 0)
    dst_block = jnp.right_shift(lax.broadcasted_iota(jnp.int32, (LANES, s * S5_GROUP), 1), 4)

    def coeffs(tau):
        spread = (src == dst_block + tau).astype(F32)
        ar_t = jnp.dot(ar_c, spread, precision=HIGHEST, preferred_element_type=F32)
        ai_t = jnp.dot(ai_c, spread, precision=HIGHEST, preferred_element_type=F32)
        return jnp.concatenate([ar_t * crt - ai_t * cit, -(ar_t * cit + ai_t * crt)], axis=0)

    w0 = coeffs(0)
    w0_ref[0] = w0
    q_ref[0] = coeffs(1).astype(q_ref.dtype)

    t0 = jnp.dot(bbt, w0, precision=HIGHEST, preferred_element_type=F32)
    lane_t = lax.broadcasted_iota(jnp.int32, t0.shape, 1)
    blocks = [t0]
    for i in range(1, s):
        blocks.append(jnp.where(lane_t >= i * S5_GROUP, pltpu.roll(t0, i * S5_GROUP, axis=1), 0.0))
    t_ref[0] = jnp.concatenate(blocks, axis=0).astype(t_ref.dtype)


def s5_prep(a_re, a_im, log_dt, b_re, b_im, c_re, c_im):
    g, p, n, s = S5_GROUPS, S5_STATE, S5_GROUP, S5_CHUNK
    dup = lambda a: jnp.concatenate([a, a], axis=-1)[:, None, :]
    b1 = jnp.concatenate([b_re.transpose(0, 2, 1), b_im.transpose(0, 2, 1)], axis=-1)
    crt = jnp.tile(c_re.transpose(0, 2, 1), (1, 1, s))
    cit = jnp.tile(c_im.transpose(0, 2, 1), (1, 1, s))
    spec = lambda *shape: pl.BlockSpec((1,) + shape, lambda i: (i, 0, 0))
    return pl.pallas_call(
        _s5_prep_kernel,
        grid=(g,),
        in_specs=[spec(1, 1), spec(p, 1), spec(p, 1), spec(1, 2 * p), spec(1, 2 * p),
                  spec(n, 2 * p), spec(p, s * n), spec(p, s * n)],
        out_specs=[spec(s * n, s * n), spec(s * n, 2 * p), spec(2 * p, s * n),
                   spec(n, 2 * p), spec(2 * p, s * n)],
        out_shape=[jax.ShapeDtypeStruct((g, s * n, s * n), BF16),
                   jax.ShapeDtypeStruct((g, s * n, 2 * p), BF16),
                   jax.ShapeDtypeStruct((g, 2 * p, s * n), BF16),
                   jax.ShapeDtypeStruct((g, n, 2 * p), F32),
                   jax.ShapeDtypeStruct((g, 2 * p, s * n), F32)],
        compiler_params=_params(1),
        name="s5_prep",
    )(log_dt[:, None, None], a_re[:, :, None], a_im[:, :, None], dup(a_re), dup(a_im), b1, crt, cit)


def _abar_power(ldt_ref, arr_ref, air_ref, power):
    dt = jnp.exp(ldt_ref[...])
    mag = jnp.exp(arr_ref[...] * dt * power)
    ang = air_ref[...] * dt * power
    lane = lax.broadcasted_iota(jnp.int32, mag.shape, 2)
    return mag * jnp.cos(ang), mag * jnp.sin(ang) * jnp.where(lane < S5_STATE, -1.0, 1.0)


def _complex_scale(x, a_r, a_i_signed):
    return a_r * x + a_i_signed * pltpu.roll(x, S5_STATE, axis=x.ndim - 1)


GROUPS_PER_BLOCK = LANES // S5_GROUP


def _s5_prompt_kernel(u_ref, t_ref, p_ref, q_ref, ldt_ref, arr_ref, air_ref, d_ref,
                      y_ref, fin_ref, ug_ref, z_ref, x_ref, yg_ref, *, bsz):
    s, n, gp = S5_CHUNK, S5_GROUP, GROUPS_PER_BLOCK
    rows = u_ref.shape[1]
    n_chunks = rows // bsz
    lane_block = jnp.right_shift(lax.broadcasted_iota(jnp.int32, (rows, LANES), 1), 4)

    for g in range(gp):
        for half in range(s // gp):
            acc = None
            for t8 in range(gp):
                piece = u_ref[half * gp + t8].astype(F32)
                shift = (n * (t8 - g)) % LANES
                if shift:
                    piece = pltpu.roll(piece, shift, axis=1)
                acc = piece if acc is None else jnp.where(lane_block == t8, piece, acc)
            ug_ref[g, :, half * LANES:(half + 1) * LANES] = acc.astype(BF16)

    for g in range(gp):
        z_ref[g] = _dot(ug_ref[g], p_ref[g])

    a_r, a_i = _abar_power(ldt_ref, arr_ref, air_ref, float(s))

    def step(j, x):
        idx = pl.ds(j, bsz, stride=n_chunks)
        x_ref[:, idx, :] = x
        return _complex_scale(x, a_r, a_i) + z_ref[:, idx, :]

    fin_ref[...] = lax.fori_loop(0, n_chunks, step, jnp.zeros(fin_ref.shape, F32))

    for g in range(gp):
        yg_ref[g] = _dot(ug_ref[g], t_ref[g]) + _dot(x_ref[g].astype(BF16), q_ref[g])

    for t in range(s):
        half, t8 = divmod(t, gp)
        acc = None
        for g in range(gp):
            piece = yg_ref[g, :, half * LANES:(half + 1) * LANES]
            shift = (n * (g - t8)) % LANES
            if shift:
                piece = pltpu.roll(piece, shift, axis=1)
            acc = piece if acc is None else jnp.where(lane_block == g, piece, acc)
        y_ref[t] = jax.nn.gelu(acc + d_ref[...] * u_ref[t].astype(F32))


def _steps_major(a, s):
    return a.reshape(a.shape[0] // s, s, a.shape[1]).swapaxes(0, 1)


def s5_prompt(u, bsz, length, t_mat, p_mat, q_mat, log_dt, a_re, a_im, d_skip):
    g, n, s, p2, gp = S5_GROUPS, S5_GROUP, S5_CHUNK, 2 * S5_STATE, GROUPS_PER_BLOCK
    rows = bsz * length // s
    dup = lambda a: jnp.concatenate([a, a], axis=-1)[:, None, :]
    gspec = lambda *shape: pl.BlockSpec((gp,) + shape, lambda k: (k, 0, 0))
    io_spec = pl.BlockSpec((s, rows, LANES), lambda k: (0, 0, k))
    return pl.pallas_call(
        functools.partial(_s5_prompt_kernel, bsz=bsz),
        grid=(g // gp,),
        in_specs=[io_spec,
                  gspec(s * n, s * n), gspec(s * n, p2), gspec(p2, s * n),
                  gspec(1, 1), gspec(1, p2), gspec(1, p2),
                  pl.BlockSpec((1, LANES), lambda k: (0, k))],
        out_specs=[io_spec, gspec(bsz, p2)],
        out_shape=[jax.ShapeDtypeStruct((s, rows, g * n), F32),
                   jax.ShapeDtypeStruct((g, bsz, p2), F32)],
        scratch_shapes=[pltpu.VMEM((gp, rows, s * n), BF16),
                        pltpu.VMEM((gp, rows, p2), F32),
                        pltpu.VMEM((gp, rows, p2), F32),
                        pltpu.VMEM((gp, rows, s * n), F32)],
        compiler_params=_params(1),
        name="s5_prompt",
    )(_steps_major(u, s), t_mat, p_mat, q_mat, log_dt[:, None, None], dup(a_re), dup(a_im),
      d_skip.reshape(1, g * n))


def _s5_sample_kernel(u_ref, xr_ref, xi_ref, bbt_ref, w0_ref, ldt_ref, arr_ref, air_ref, d_ref,
                      y_ref, nr_ref, ni_ref):
    p = S5_STATE
    dt = jnp.exp(ldt_ref[0])
    mag = jnp.exp(arr_ref[0][:, :p] * dt)
    ang = air_ref[0][:, :p] * dt
    abr, abi = mag * jnp.cos(ang), mag * jnp.sin(ang)
    u = u_ref[0].astype(F32)
    z = _dot(u.astype(BF16), bbt_ref[0].astype(BF16))
    xr, xi = xr_ref[0], xi_ref[0]
    new_r = abr * xr - abi * xi + z[:, :p]
    new_i = abr * xi + abi * xr + z[:, p:]
    nr_ref[0] = new_r
    ni_ref[0] = new_i
    c = w0_ref[0][:, :S5_GROUP].astype(BF16)
    y = _dot(new_r.astype(BF16), c[:p]) + _dot(new_i.astype(BF16), c[p:]) + d_ref[0] * u
    y_ref[0] = jax.nn.gelu(y)


def s5_sample(u, x_re, x_im, bbt, w0, log_dt, a_re, a_im, d_skip):
    g, n, p = S5_GROUPS, S5_GROUP, S5_STATE
    bsz = u.shape[0]
    dup = lambda a: jnp.concatenate([a, a], axis=-1)[:, None, :]
    gspec = lambda *shape: pl.BlockSpec((1,) + shape, lambda i: (i, 0, 0))
    y, new_r, new_i = pl.pallas_call(
        _s5_sample_kernel,
        grid=(g,),
        in_specs=[gspec(bsz, n), gspec(bsz, p), gspec(bsz, p), gspec(n, 2 * p),
                  gspec(2 * p, S5_CHUNK * n), gspec(1, 1), gspec(1, 2 * p), gspec(1, 2 * p),
                  gspec(1, n)],
        out_specs=[gspec(bsz, n), gspec(bsz, p), gspec(bsz, p)],
        out_shape=[jax.ShapeDtypeStruct((g, bsz, n), F32),
                   jax.ShapeDtypeStruct((g, bsz, p), F32),
                   jax.ShapeDtypeStruct((g, bsz, p), F32)],
        compiler_params=_params(1),
        name="s5_sample",
    )(u.reshape(bsz, g, n).transpose(1, 0, 2), x_re.transpose(1, 0, 2), x_im.transpose(1, 0, 2),
      bbt, w0, log_dt[:, None, None], dup(a_re), dup(a_im), d_skip.reshape(g, 1, n))
    return (y.transpose(1, 0, 2).reshape(bsz, g * n), new_r.transpose(1, 0, 2),
            new_i.transpose(1, 0, 2))


FACTORED_DECAY_LIMIT = -60.0


def _gla_kernel(q_ref, k_ref, v_ref, gr_ref, g_ref, nrm_ref, o_ref, st_ref, s_ref, b_ref,
                qb_ref, kb_ref, kend_ref, sc_ref, acc_ref):
    t = pl.program_id(0)

    @pl.when(t == 0)
    def _():
        s_ref[...] = jnp.zeros(s_ref.shape, F32)

    bsz, c = q_ref.shape[0], q_ref.shape[1]
    row = lax.broadcasted_iota(jnp.int32, (c, c), 0)
    col = lax.broadcasted_iota(jnp.int32, (c, c), 1)
    causal = col <= row
    ones_tri = causal.astype(F32)
    step = lax.broadcasted_iota(jnp.int32, (c, GLA_DK), 0)

    lowest = None
    for i in range(bsz):
        b_ref[i] = jnp.dot(ones_tri, g_ref[i], precision=HIGHEST, preferred_element_type=F32)
        end_min = jnp.min(b_ref[i, c - 1:c, :])
        lowest = end_min if lowest is None else jnp.minimum(lowest, end_min)
    factorable = lowest >= FACTORED_DECAY_LIMIT

    pairs = [(i, h) for i in range(bsz) for h in range(GLA_HEADS)]
    head_k = lambda h: slice(h * GLA_DK, (h + 1) * GLA_DK)
    head_v = lambda h: slice(h * GLA_DV, (h + 1) * GLA_DV)

    def finish(i, h, o):
        o = o * lax.rsqrt(jnp.mean(o * o, axis=-1, keepdims=True) + EPS) * nrm_ref[...]
        o_ref[i, :, head_v(h)] = (o * _silu(gr_ref[i, :, head_v(h)].astype(F32))).astype(o_ref.dtype)

    def all_chunks_factored():
        for n, (i, h) in enumerate(pairs):
            b = b_ref[i, :, head_k(h)]
            k = k_ref[i, :, head_k(h)].astype(F32)
            qb_ref[n] = (q_ref[i, :, head_k(h)].astype(F32) * (GLA_DK ** -0.5) * jnp.exp(b)).astype(BF16)
            kb_ref[n] = (k * jnp.exp(-b)).astype(BF16)
            kend_ref[n] = (k * jnp.exp(b[c - 1:c, :] - b)).astype(BF16)
        for n, (i, h) in enumerate(pairs):
            scores = lax.dot_general(qb_ref[n], kb_ref[n], NT, preferred_element_type=F32)
            sc_ref[n] = jnp.where(causal, scores, 0.0).astype(BF16)
        for n, (i, h) in enumerate(pairs):
            v = v_ref[i, :, head_v(h)].astype(BF16)
            s_t = s_ref[i, h]
            acc_ref[n] = _dot(sc_ref[n], v) + lax.dot_general(qb_ref[n], s_t.astype(BF16), NT,
                                                              preferred_element_type=F32)
            s_ref[i, h] = (jnp.exp(b_ref[i, c - 1:c, head_k(h)]) * s_t
                           + lax.dot_general(v, kend_ref[n], TN, preferred_element_type=F32))
        for n, (i, h) in enumerate(pairs):
            finish(i, h, acc_ref[n])

    def intra_stepwise(i, ks, vs, b, q_in, v):
        q = q_ref[i, :, ks].astype(F32) * (GLA_DK ** -0.5)

        def add_sources(tile, acc):
            first = pl.multiple_of(tile * ROW_TILE, ROW_TILE)
            rows = pl.ds(first, ROW_TILE)
            b_s = b_ref[i, rows, ks]
            k_s, v_s = k_ref[i, rows, ks].astype(F32), v_ref[i, rows, vs].astype(F32)
            for r in range(ROW_TILE):
                decay = jnp.exp(jnp.where(step >= first + r, b - b_s[r:r + 1], -jnp.inf))
                score = jnp.sum(q * k_s[r:r + 1] * decay, axis=1, keepdims=True)
                acc = acc + score * v_s[r:r + 1]
            return acc

        return lax.fori_loop(0, c // ROW_TILE, add_sources, jnp.zeros((c, GLA_DV), F32))

    def all_chunks_stepwise():
        for i, h in pairs:
            ks, vs = head_k(h), head_v(h)
            b = b_ref[i, :, ks]
            b_end = b[c - 1:c, :]
            q_in = (q_ref[i, :, ks].astype(F32) * (GLA_DK ** -0.5) * jnp.exp(b)).astype(BF16)
            k_end = (k_ref[i, :, ks].astype(F32) * jnp.exp(b_end - b)).astype(BF16)
            v = v_ref[i, :, vs].astype(BF16)
            s_t = s_ref[i, h]
            o = (intra_stepwise(i, ks, vs, b, q_in, v)
                 + lax.dot_general(q_in, s_t.astype(BF16), NT, preferred_element_type=F32))
            s_ref[i, h] = jnp.exp(b_end) * s_t + lax.dot_general(v, k_end, TN,
                                                                 preferred_element_type=F32)
            finish(i, h, o)

    pl.when(factorable)(all_chunks_factored)
    pl.when(jnp.logical_not(factorable))(all_chunks_stepwise)

    @pl.when(t == pl.num_programs(0) - 1)
    def _():
        st_ref[...] = s_ref[...]


def gla_prompt(proj, g, gla_norm, bsz, length):
    c = CHUNK
    nc = length // c
    hk, hv = GLA_HEADS * GLA_DK, GLA_HEADS * GLA_DV
    rows = lambda width, off: pl.BlockSpec((bsz, c, width), lambda t: (0, t, off // width))
    st_shape = (bsz, GLA_HEADS, GLA_DV, GLA_DK)
    proj3 = proj.reshape(bsz, length, proj.shape[1])
    o, st = pl.pallas_call(
        _gla_kernel,
        grid=(nc,),
        in_specs=[rows(hk, OFF_GQ), rows(hk, OFF_GK), rows(hv, OFF_GV), rows(hv, OFF_GR),
                  rows(hk, 0), pl.BlockSpec((1, GLA_DV), lambda t: (0, 0))],
        out_specs=[rows(hv, 0), pl.BlockSpec(st_shape, lambda t: (0, 0, 0, 0))],
        out_shape=[jax.ShapeDtypeStruct((bsz, length, hv), BF16),
                   jax.ShapeDtypeStruct(st_shape, F32)],
        scratch_shapes=[pltpu.VMEM(st_shape, F32),
                        pltpu.VMEM((bsz, c, hk), F32)]
        + [pltpu.VMEM((bsz * GLA_HEADS, c, GLA_DK), BF16)] * 3
        + [pltpu.VMEM((bsz * GLA_HEADS, c, c), BF16),
           pltpu.VMEM((bsz * GLA_HEADS, c, GLA_DV), F32)],
        compiler_params=_params(1),
        name="gla_prompt",
    )(proj3, proj3, proj3, proj3, g.reshape(bsz, length, hk), gla_norm.reshape(1, GLA_DV))
    return o.reshape(bsz * length, hv), st.swapaxes(2, 3)


def _rotate_pairs(x, cos2, sin2):
    lane = lax.broadcasted_iota(jnp.int32, x.shape, 1)
    partner = jnp.where(lane % 2 == 0, pltpu.roll(x, x.shape[1] - 1, axis=1), pltpu.roll(x, 1, axis=1))
    return x * cos2 + partner * sin2


def _ret_kernel(q_ref, k_ref, v_ref, rg_ref, cos_ref, sin_ref, dm_ref, qd_ref, kd_ref, gc_ref,
                o_ref, st_ref, s_ref, qb_ref, kb_ref, qdec_ref, kdec_ref, sc_ref, acc_ref):
    t = pl.program_id(0)

    @pl.when(t == 0)
    def _():
        s_ref[...] = jnp.zeros(s_ref.shape, F32)

    cos2, sin2 = cos_ref[...], sin_ref[...]
    pairs = [(i, h) for i in range(q_ref.shape[0]) for h in range(RET_HEADS)]
    head_k = lambda h: slice(h * RET_DK, (h + 1) * RET_DK)
    head_v = lambda h: slice(h * RET_DV, (h + 1) * RET_DV)
    for n, (i, h) in enumerate(pairs):
        q = _rotate_pairs(q_ref[i, :, head_k(h)].astype(F32), cos2, sin2)
        k = _rotate_pairs(k_ref[i, :, head_k(h)].astype(F32), cos2, sin2) * (RET_DK ** -0.5)
        qb_ref[n] = q.astype(BF16)
        kb_ref[n] = k.astype(BF16)
        qdec_ref[n] = (q * qd_ref[h]).astype(BF16)
        kdec_ref[n] = (k * kd_ref[h]).astype(BF16)
    for n, (i, h) in enumerate(pairs):
        scores = lax.dot_general(qb_ref[n], kb_ref[n], NT, preferred_element_type=F32) * dm_ref[h]
        sc_ref[n] = scores.astype(BF16)
    for n, (i, h) in enumerate(pairs):
        v = v_ref[i, :, head_v(h)].astype(BF16)
        s_t = s_ref[i, h]
        acc_ref[n] = _dot(sc_ref[n], v) + lax.dot_general(qdec_ref[n], s_t.astype(BF16), NT,
                                                          preferred_element_type=F32)
        s_ref[i, h] = gc_ref[h] * s_t + lax.dot_general(v, kdec_ref[n], TN, preferred_element_type=F32)
    for n, (i, h) in enumerate(pairs):
        o = acc_ref[n]
        oc = o - jnp.mean(o, axis=-1, keepdims=True)
        o = oc * lax.rsqrt(jnp.mean(oc * oc, axis=-1, keepdims=True) + EPS)
        o_ref[i, :, head_v(h)] = (o * _silu(rg_ref[i, :, head_v(h)].astype(F32))).astype(o_ref.dtype)

    @pl.when(t == pl.num_programs(0) - 1)
    def _():
        st_ref[...] = s_ref[...]


def _rotary_tables(pos):
    half = RET_DK // 2
    inv = 1.0 / (ROPE_BASE ** jnp.linspace(0.0, 1.0, half, dtype=F32))
    ang = pos[:, None] * inv[None, :]
    cos2 = jnp.repeat(jnp.cos(ang), 2, axis=-1)
    sin2 = jnp.stack([-jnp.sin(ang), jnp.sin(ang)], axis=-1).reshape(pos.shape[0], RET_DK)
    return cos2, sin2


def _log_gamma():
    return jnp.log1p(-jnp.power(2.0, -5.0 - jnp.arange(RET_HEADS, dtype=F32)))


def ret_prompt(proj, bsz, length):
    c = CHUNK
    nc = length // c
    hk, hv = RET_HEADS * RET_DK, RET_HEADS * RET_DV
    cos2, sin2 = _rotary_tables(jnp.arange(length, dtype=F32))
    lg = _log_gamma()[:, None, None]
    steps = jnp.arange(c, dtype=F32)
    lag = steps[:, None] - steps[None, :]
    dmat = jnp.where(lag >= 0, jnp.exp(lg * jnp.maximum(lag, 0.0)), 0.0)
    wide = lambda a: jnp.broadcast_to(a, (RET_HEADS, a.shape[1], LANES))
    qdec = wide(jnp.exp(lg * (steps + 1.0)[None, :, None]))
    kdec = wide(jnp.exp(lg * (c - 1.0 - steps)[None, :, None]))
    gchunk = wide(jnp.exp(lg * float(c)))
    rows = lambda width, off: pl.BlockSpec((bsz, c, width), lambda t: (0, t, off // width))
    table = pl.BlockSpec((c, RET_DK), lambda t: (t, 0))
    const = lambda a: pl.BlockSpec(a.shape, lambda t: (0, 0, 0))
    st_shape = (bsz, RET_HEADS, RET_DV, RET_DK)
    proj3 = proj.reshape(bsz, length, proj.shape[1])
    o, st = pl.pallas_call(
        _ret_kernel,
        grid=(nc,),
        in_specs=[rows(hk, OFF_RQ), rows(hk, OFF_RK), rows(hv, OFF_RV), rows(hv, OFF_RG),
                  table, table, const(dmat), const(qdec), const(kdec), const(gchunk)],
        out_specs=[rows(hv, 0), pl.BlockSpec(st_shape, lambda t: (0, 0, 0, 0))],
        out_shape=[jax.ShapeDtypeStruct((bsz, length, hv), BF16),
                   jax.ShapeDtypeStruct(st_shape, F32)],
        scratch_shapes=[pltpu.VMEM(st_shape, F32)]
        + [pltpu.VMEM((bsz * RET_HEADS, c, RET_DK), BF16)] * 4
        + [pltpu.VMEM((bsz * RET_HEADS, c, c), BF16),
           pltpu.VMEM((bsz * RET_HEADS, c, RET_DV), F32)],
        compiler_params=_params(1),
        name="ret_prompt",
    )(proj3, proj3, proj3, proj3, cos2, sin2, dmat, qdec, kdec, gchunk)
    return o.reshape(bsz * length, hv), st.swapaxes(2, 3)


def _one_column(a_t, i):
    col = lax.broadcasted_iota(jnp.int32, a_t.shape, 1)
    return jnp.where(col == i, a_t, 0.0).astype(BF16)


def _gla_sample_kernel(s_ref, q_ref, kt_ref, gt_ref, v_ref, gr_ref, nrm_ref, *rest):
    so_ref, o_ref, acc_ref = rest[-3:]
    ones = jnp.ones((SAMPLE_BLOCK, GLA_DV), BF16)
    head_k = lambda h: slice(h * GLA_DK, (h + 1) * GLA_DK)
    head_v = lambda h: slice(h * GLA_DV, (h + 1) * GLA_DV)
    for h in range(GLA_HEADS):
        v = v_ref[:, head_v(h)].astype(BF16)
        k_t = kt_ref[0, h]
        decay_t = jnp.exp(gt_ref[0, h])
        decay_hi = decay_t.astype(BF16).astype(F32)
        decay_lo = decay_t - decay_hi
        for i in range(SAMPLE_BLOCK):
            decay = _dot(_one_column(decay_hi, i), ones) + _dot(_one_column(decay_lo, i), ones)
            so_ref[i, h] = decay * s_ref[i, h] + _dot(_one_column(k_t, i), v)
    for h in range(GLA_HEADS):
        q = (q_ref[:, head_k(h)].astype(F32) * (GLA_DK ** -0.5)).astype(BF16)
        for i in range(SAMPLE_BLOCK):
            acc_ref[i:i + 1, head_v(h)] = _dot(q, so_ref[i, h].astype(BF16))[i:i + 1]
    for h in range(GLA_HEADS):
        o = acc_ref[:, head_v(h)]
        o = o * lax.rsqrt(jnp.mean(o * o, axis=-1, keepdims=True) + EPS) * nrm_ref[...]
        o_ref[:, head_v(h)] = (o * _silu(gr_ref[:, head_v(h)].astype(F32))).astype(o_ref.dtype)


def _columns(a, heads, dk):
    bsz = a.shape[0]
    return a.reshape(bsz // SAMPLE_BLOCK, SAMPLE_BLOCK, heads, dk).transpose(0, 2, 3, 1)


def _stacked_state_call(kernel_fn, name, state, layer, prev, heads, dk, dv, in_specs, operands,
                        o_spec, o_shape):
    bb = SAMPLE_BLOCK
    st_spec = pl.BlockSpec((None, bb, heads, dk, dv), lambda i: (layer, i, 0, 0, 0))
    aliases = {}
    if prev is not None:
        in_specs = in_specs + [pl.BlockSpec(memory_space=pl.ANY)]
        operands = operands + (prev,)
        aliases = {len(in_specs): 0}
    return pl.pallas_call(
        kernel_fn,
        grid=(state.shape[1] // bb,),
        in_specs=[st_spec] + in_specs,
        out_specs=[st_spec, o_spec],
        out_shape=[jax.ShapeDtypeStruct(state.shape, F32), o_shape],
        scratch_shapes=[pltpu.VMEM((bb, heads * dv), F32)],
        input_output_aliases=aliases,
        compiler_params=_params(1),
        name=name,
    )(state, *operands)


def gla_sample(proj, g, state, gla_norm, layer, prev):
    bsz = proj.shape[0]
    bb = SAMPLE_BLOCK
    hk, hv = GLA_HEADS * GLA_DK, GLA_HEADS * GLA_DV
    col_spec = pl.BlockSpec((1, GLA_HEADS, GLA_DK, bb), lambda i: (i, 0, 0, 0))
    rows = lambda off: pl.BlockSpec((bb, hv), lambda i: (i, off // hv))
    return _stacked_state_call(
        _gla_sample_kernel, "gla_sample", state, layer, prev, GLA_HEADS, GLA_DK, GLA_DV,
        [pl.BlockSpec((bb, hk), lambda i: (i, OFF_GQ // hk)), col_spec, col_spec, rows(OFF_GV),
         rows(OFF_GR), pl.BlockSpec((None, 1, GLA_DV), lambda i: (layer, 0, 0))],
        (proj, _columns(proj[:, OFF_GK:OFF_GK + hk], GLA_HEADS, GLA_DK), _columns(g, GLA_HEADS, GLA_DK),
         proj, proj, gla_norm.reshape(gla_norm.shape[0], 1, GLA_DV)),
        rows(0), jax.ShapeDtypeStruct((bsz, hv), BF16))


def _rot_kernel(q_ref, k_ref, cos_ref, sin_ref, qo_ref, ko_ref):
    cos2, sin2 = cos_ref[...], sin_ref[...]
    for h in range(RET_HEADS):
        ks = slice(h * RET_DK, (h + 1) * RET_DK)
        qo_ref[:, ks] = _rotate_pairs(q_ref[:, ks].astype(F32), cos2, sin2)
        ko_ref[:, ks] = _rotate_pairs(k_ref[:, ks].astype(F32), cos2, sin2) * (RET_DK ** -0.5)


def _ret_sample_kernel(s_ref, q_ref, kt_ref, gam_ref, v_ref, rg_ref, *rest):
    so_ref, o_ref, acc_ref = rest[-3:]
    head_k = lambda h: slice(h * RET_DK, (h + 1) * RET_DK)
    head_v = lambda h: slice(h * RET_DV, (h + 1) * RET_DV)
    for h in range(RET_HEADS):
        v = v_ref[:, head_v(h)].astype(BF16)
        k_t = kt_ref[0, h]
        for i in range(SAMPLE_BLOCK):
            so_ref[i, h] = gam_ref[h] * s_ref[i, h] + _dot(_one_column(k_t, i), v)
    for h in range(RET_HEADS):
        q = q_ref[:, head_k(h)].astype(BF16)
        for i in range(SAMPLE_BLOCK):
            acc_ref[i:i + 1, head_v(h)] = _dot(q, so_ref[i, h].astype(BF16))[i:i + 1]
    for h in range(RET_HEADS):
        o = acc_ref[:, head_v(h)]
        oc = o - jnp.mean(o, axis=-1, keepdims=True)
        o = oc * lax.rsqrt(jnp.mean(oc * oc, axis=-1, keepdims=True) + EPS)
        o_ref[:, head_v(h)] = (o * _silu(rg_ref[:, head_v(h)].astype(F32))).astype(o_ref.dtype)


def ret_sample(proj, state, layer, prev):
    bsz = proj.shape[0]
    bb = SAMPLE_BLOCK
    hk, hv = RET_HEADS * RET_DK, RET_HEADS * RET_DV
    cos2, sin2 = _rotary_tables(PAST_LEN + jnp.arange(1, dtype=F32))
    full = pl.BlockSpec((bsz, hk), lambda i: (0, 0))
    q_rot, k_rot = pl.pallas_call(
        _rot_kernel,
        grid=(1,),
        in_specs=[pl.BlockSpec((bsz, hk), lambda i: (0, OFF_RQ // hk)),
                  pl.BlockSpec((bsz, hk), lambda i: (0, OFF_RK // hk)),
                  pl.BlockSpec((1, RET_DK), lambda i: (0, 0)), pl.BlockSpec((1, RET_DK), lambda i: (0, 0))],
        out_specs=[full, full],
        out_shape=[jax.ShapeDtypeStruct((bsz, hk), F32)] * 2,
        compiler_params=_params(1),
        name="ret_rotary",
    )(proj, proj, cos2, sin2)
    gamma = jnp.broadcast_to(jnp.exp(_log_gamma())[:, None, None], (RET_HEADS, 1, LANES))
    col_spec = pl.BlockSpec((1, RET_HEADS, RET_DK, bb), lambda i: (i, 0, 0, 0))
    rows = lambda off: pl.BlockSpec((bb, hv), lambda i: (i, off // hv))
    return _stacked_state_call(
        _ret_sample_kernel, "ret_sample", state, layer, prev, RET_HEADS, RET_DK, RET_DV,
        [rows(0), col_spec, pl.BlockSpec(gamma.shape, lambda i: (0, 0, 0)), rows(OFF_RV), rows(OFF_RG)],
        (q_rot, _columns(k_rot, RET_HEADS, RET_DK), gamma, proj, proj),
        rows(0), jax.ShapeDtypeStruct((bsz, hv), BF16))


def _layer(x, layer, prompt_shape, states, w, s5_mats):
    t_mat, p_mat, q_mat, bbt, w0 = s5_mats
    m = x.shape[0]
    tm = min(m, 512)
    tm_wide = min(m, 1024)
    tn = 512
    s5_par = (w["s5_log_dt"][layer], w["s5_a_re"][layer], w["s5_a_im"][layer], w["s5_d"][layer])

    h = rmsnorm(x, w["norm_mix"][layer], BF16)
    proj = in_proj(h, w["w_in_rows"], layer, tm_wide, 2 * tn)
    g = gla_gate(h, w["w_in_rows"], w["gla_w_gate"], w["gla_b_gate"], layer, tm)
    u = proj[:, OFF_U:OFF_U + S5_WIDTH]

    if prompt_shape is not None:
        bsz, length = prompt_shape
        y5, x_fin = s5_prompt(u, bsz, length, t_mat, p_mat, q_mat, *s5_par)
        y5 = y5.reshape(m, S5_WIDTH)
        x_fin = x_fin.transpose(1, 0, 2)
        new_states = (x_fin[..., :S5_STATE], x_fin[..., S5_STATE:])
        o_gla, new_gla = gla_prompt(proj, g, w["gla_norm"][layer], bsz, length)
        o_ret, new_ret = ret_prompt(proj, bsz, length)
    else:
        st_s5_re, st_s5_im, st_gla, st_ret, prev_gla, prev_ret = states
        y5, *new_states = s5_sample(u, st_s5_re[layer], st_s5_im[layer], bbt, w0, *s5_par)
        new_gla, o_gla = gla_sample(proj, g, st_gla, w["gla_norm"], layer, prev_gla)
        new_ret, o_ret = ret_sample(proj, st_ret, layer, prev_ret)

    a5 = s5_glu(y5, w["s5_w_glu"], w["s5_b_glu"], layer, tm, 2 * tn)
    if prompt_shape is not None:
        a5 = a5.reshape(S5_CHUNK, m // S5_CHUNK, S5_WIDTH).swapaxes(0, 1).reshape(m, S5_WIDTH)
    merged = merge_branches(a5, o_gla, o_ret, w["s5_w_out"], w["gla_w_out"], w["ret_w_out"], proj,
                            layer, tm, 2 * tn)
    x = residual_matmul(merged, w["w_mix_out"], x, layer, tm, 2 * tn)
    h2 = rmsnorm(x, w["norm_ffn"][layer], BF16)
    ff = ffn_up(h2, w["w_ffn_gate"], w["w_ffn_up"], layer, tm_wide, tn)
    x = residual_matmul(ff, w["w_ffn_down"], x, layer, tm, tn)
    return x, new_states[0], new_states[1], new_gla, new_ret


def kernel(x_prompt, x_sample, state_s5_re, state_s5_im, state_gla, state_ret, norm_mix, w_in, s5_a_re, s5_a_im, s5_log_dt, s5_b_re, s5_b_im, s5_c_re, s5_c_im, s5_d, s5_w_glu, s5_b_glu, s5_w_out, gla_w_gate, gla_b_gate, gla_norm, gla_w_out, ret_w_out, w_mix_out, norm_ffn, w_ffn_gate, w_ffn_up, w_ffn_down, norm_final):
    bp, lp, d = x_prompt.shape
    bs = x_sample.shape[0]
    depth = w_in.shape[0]
    hp = x_prompt.reshape(bp * lp, d)
    hs = x_sample.reshape(bs, d)
    w = dict(norm_mix=norm_mix, s5_log_dt=s5_log_dt, s5_a_re=s5_a_re, s5_a_im=s5_a_im, s5_d=s5_d,
             s5_w_glu=s5_w_glu, s5_b_glu=s5_b_glu, s5_w_out=s5_w_out, gla_w_gate=gla_w_gate,
             gla_b_gate=gla_b_gate, gla_norm=gla_norm, gla_w_out=gla_w_out, ret_w_out=ret_w_out,
             w_mix_out=w_mix_out, norm_ffn=norm_ffn, w_ffn_gate=w_ffn_gate, w_ffn_up=w_ffn_up,
             w_ffn_down=w_ffn_down,
             w_in_rows=jnp.swapaxes(w_in, 1, 2).reshape(depth, w_in.shape[2] // GLA_RANK, GLA_RANK, d))
    outs_p, outs_s = [], []
    new_gla_s = new_ret_s = None
    for l in range(depth):
        s5_mats = s5_prep(s5_a_re[l], s5_a_im[l], s5_log_dt[l], s5_b_re[l], s5_b_im[l], s5_c_re[l],
                          s5_c_im[l])
        hp, *new_p = _layer(hp, l, (bp, lp), None, w, s5_mats)
        hs, s5r, s5i, new_gla_s, new_ret_s = _layer(
            hs, l, None, (state_s5_re, state_s5_im, state_gla, state_ret, new_gla_s, new_ret_s),
            w, s5_mats)
        outs_p.append(new_p)
        outs_s.append((s5r, s5i))
    y_prompt = rmsnorm(hp, norm_final, F32).reshape(bp, lp, d)
    y_sample = rmsnorm(hs, norm_final, F32).reshape(bs, 1, d)
    stack = lambda outs, i: jnp.stack([o[i] for o in outs])
    return (y_prompt, y_sample,
            stack(outs_p, 0), stack(outs_p, 1), stack(outs_p, 2), stack(outs_p, 3),
            stack(outs_s, 0), stack(outs_s, 1), new_gla_s, new_ret_s)
```

```python
import functools
import math

import jax
import jax.numpy as jnp
import numpy as np
from jax import lax
from jax.experimental import pallas as pl
from jax.experimental.pallas import tpu as pltpu

F32 = jnp.float32
BF16 = jnp.bfloat16
HIGHEST = lax.Precision.HIGHEST

D_MODEL = 2048
S5_WIDTH = D_MODEL // 2
S5_GROUP = 16
S5_GROUPS = S5_WIDTH // S5_GROUP
S5_STATE = 64
S5_CHUNK = 16
GLA_HEADS = 4
GLA_DK = D_MODEL // 16
GLA_DV = D_MODEL // 8
GLA_RANK = 16
GLA_GATE_NORM = 16.0
RET_HEADS = 8
RET_DK = D_MODEL // 16
RET_DV = D_MODEL // 16
ROPE_BASE = 10000.0
PAST_LEN = 16384
N_BRANCH = 3
D_FF = -(-8 * D_MODEL // (3 * 256)) * 256
EPS = 1e-6
CHUNK = 128
LANES = 128
ROW_TILE = 16
SAMPLE_BLOCK = 16

OFF_U = 0
OFF_GQ = OFF_U + S5_WIDTH
OFF_GK = OFF_GQ + GLA_HEADS * GLA_DK
OFF_GV = OFF_GK + GLA_HEADS * GLA_DK
OFF_GR = OFF_GV + GLA_HEADS * GLA_DV
OFF_RQ = OFF_GR + GLA_HEADS * GLA_DV
OFF_RK = OFF_RQ + RET_HEADS * RET_DK
OFF_RV = OFF_RK + RET_HEADS * RET_DK
OFF_RG = OFF_RV + RET_HEADS * RET_DV
OFF_MG = OFF_RG + RET_HEADS * RET_DV
W_MAIN = OFF_MG + N_BRANCH * D_MODEL
GLR_COL = OFF_RQ

VMEM_LIMIT = 56 * 1024 * 1024

NT = (((1,), (1,)), ((), ()))
TN = (((0,), (0,)), ((), ()))


def _params(n_grid):
    return pltpu.CompilerParams(dimension_semantics=("arbitrary",) * n_grid,
                                vmem_limit_bytes=VMEM_LIMIT)


def _silu(x):
    return x * jax.nn.sigmoid(x)


def _rmsnorm_kernel(x_ref, g_ref, o_ref):
    x = x_ref[...]
    y = x * lax.rsqrt(jnp.mean(x * x, axis=-1, keepdims=True) + EPS)
    o_ref[...] = (y * g_ref[...]).astype(o_ref.dtype)


def rmsnorm(x, gain, out_dtype):
    m, d = x.shape
    tm = min(m, 512)
    return pl.pallas_call(
        _rmsnorm_kernel,
        grid=(m // tm,),
        in_specs=[pl.BlockSpec((tm, d), lambda i: (i, 0)),
                  pl.BlockSpec((1, d), lambda i: (0, 0))],
        out_specs=pl.BlockSpec((tm, d), lambda i: (i, 0)),
        out_shape=jax.ShapeDtypeStruct((m, d), out_dtype),
        compiler_params=_params(1),
        name="rmsnorm",
    )(x, gain.reshape(1, d))


def _dot(a, b):
    return jnp.dot(a, b, preferred_element_type=F32)


def _in_proj_kernel(a_ref, as_ref, w_ref, o_ref, os_ref, wb_ref, *, prompt_blocks):
    i = pl.program_id(1)

    @pl.when(i == 0)
    def _():
        wb_ref[...] = w_ref[...].reshape(wb_ref.shape).astype(BF16)

    def project(src_ref, dst_ref):
        dst_ref[...] = lax.dot_general(src_ref[...], wb_ref[...], NT,
                                       preferred_element_type=F32).astype(dst_ref.dtype)

    pl.when(i < prompt_blocks)(lambda: project(a_ref, o_ref))
    pl.when(i == prompt_blocks)(lambda: project(as_ref, os_ref))


def in_proj(h, h_s, w_rows, layer, tm, tn):
    m, k = h.shape
    ms = h_s.shape[0]
    nb = m // tm
    rb = tn // GLA_RANK
    skip_from = GLR_COL // tn

    def w_index(j, i):
        return (layer, rb * j + jnp.where(j >= skip_from, 1, 0), 0, 0)

    return pl.pallas_call(
        functools.partial(_in_proj_kernel, prompt_blocks=nb),
        grid=(W_MAIN // tn, nb + 1),
        in_specs=[pl.BlockSpec((tm, k), lambda j, i: (jnp.minimum(i, nb - 1), 0)),
                  pl.BlockSpec((ms, k), lambda j, i: (0, 0)),
                  pl.BlockSpec((pl.Element(1), pl.Element(rb), pl.Element(GLA_RANK), pl.Element(k)),
                               w_index)],
        out_specs=[pl.BlockSpec((tm, tn), lambda j, i: (jnp.minimum(i, nb - 1), j)),
                   pl.BlockSpec((ms, tn), lambda j, i: (0, j))],
        out_shape=[jax.ShapeDtypeStruct((m, W_MAIN), BF16), jax.ShapeDtypeStruct((ms, W_MAIN), BF16)],
        scratch_shapes=[pltpu.VMEM((tn, k), BF16)],
        compiler_params=_params(2),
        name="in_proj",
    )(h, h_s, w_rows)


def _fused_matmul(name, body, streams, weights, biases, layer, n, tm, tn, out_dtype):
    nb = streams[0][0][0].shape[0] // tm
    specs, operands, counts, out_specs, out_shapes = [], [], [], [], []
    for s, (row_ops, tile_ops) in enumerate(streams):
        m_s = row_ops[0].shape[0]
        rows = tm if s == 0 else m_s
        row_block = (lambda i: jnp.minimum(i, nb - 1)) if s == 0 else (lambda i: 0)
        for a in row_ops:
            specs.append(pl.BlockSpec((rows, a.shape[1]), lambda j, i, r=row_block: (r(i), 0)))
            operands.append(a)
        for a, first in tile_ops:
            specs.append(pl.BlockSpec((rows, tn), lambda j, i, r=row_block, c=first // tn: (r(i), c + j)))
            operands.append(a)
        counts.append((len(row_ops), len(tile_ops)))
        out_specs.append(pl.BlockSpec((rows, tn), lambda j, i, r=row_block: (r(i), j)))
        out_shapes.append(jax.ShapeDtypeStruct((m_s, n), out_dtype))
    for wt in weights:
        specs.append(pl.BlockSpec((None, wt.shape[1], tn), lambda j, i: (layer, 0, j),
                                  pipeline_mode=pl.Buffered(1)))
        operands.append(wt)
    for b in biases:
        specs.append(pl.BlockSpec((None, 1, tn), lambda j, i: (layer, 0, j)))
        operands.append(b)

    def kernel_fn(*refs):
        refs = list(refs)
        stream_refs = [(tuple(refs.pop(0) for _ in range(nr)), tuple(refs.pop(0) for _ in range(nt)))
                       for nr, nt in counts]
        w_refs = [refs.pop(0) for _ in weights]
        b_refs = [refs.pop(0) for _ in biases]
        o_refs = [refs.pop(0) for _ in streams]
        wb_refs = refs
        i = pl.program_id(1)

        @pl.when(i == 0)
        def _():
            for w_ref, wb_ref in zip(w_refs, wb_refs):
                wb_ref[...] = w_ref[...].astype(BF16)

        def run(s):
            row_refs, tile_refs = stream_refs[s]
            out = body(*[r[...] for r in row_refs], *[r[...] for r in wb_refs],
                       *[r[...] for r in tile_refs], *[r[...] for r in b_refs])
            o_refs[s][...] = out.astype(out_dtype)

        pl.when(i < nb)(functools.partial(run, 0))
        for s in range(1, len(streams)):
            pl.when(i == nb + s - 1)(functools.partial(run, s))

    return pl.pallas_call(
        kernel_fn,
        grid=(n // tn, nb + len(streams) - 1),
        in_specs=specs,
        out_specs=out_specs,
        out_shape=out_shapes,
        scratch_shapes=[pltpu.VMEM((wt.shape[1], tn), BF16) for wt in weights],
        compiler_params=_params(2),
        name=name,
    )(*operands)


def _gla_gate_kernel(h_ref, wl_ref, wg_ref, b_ref, o_ref):
    pad = LANES - GLA_RANK
    wl = jnp.concatenate([wl_ref[0], jnp.zeros((pad, wl_ref.shape[2]), F32)], axis=0).astype(BF16)
    wg = jnp.concatenate([wg_ref[...], jnp.zeros((pad, wg_ref.shape[1]), F32)], axis=0).astype(BF16)
    glr = lax.dot_general(h_ref[...], wl, NT, preferred_element_type=F32)
    logit = _dot(glr.astype(BF16), wg) + b_ref[...]
    log_sig = jnp.minimum(logit, 0.0) - jnp.log1p(jnp.exp(-jnp.abs(logit)))
    o_ref[...] = log_sig / GLA_GATE_NORM


def gla_gate(h, w_rows, w_gate, b_gate, layer, tm):
    m, k = h.shape
    n = w_gate.shape[2]
    return pl.pallas_call(
        _gla_gate_kernel,
        grid=(m // tm,),
        in_specs=[pl.BlockSpec((tm, k), lambda i: (i, 0)),
                  pl.BlockSpec((None, 1, GLA_RANK, k), lambda i: (layer, GLR_COL // GLA_RANK, 0, 0)),
                  pl.BlockSpec((None, GLA_RANK, n), lambda i: (layer, 0, 0)),
                  pl.BlockSpec((None, 1, n), lambda i: (layer, 0, 0))],
        out_specs=pl.BlockSpec((tm, n), lambda i: (i, 0)),
        out_shape=jax.ShapeDtypeStruct((m, n), F32),
        compiler_params=_params(1),
        name="gla_gate",
    )(h, w_rows, w_gate, b_gate.reshape(b_gate.shape[0], 1, n))


def s5_glu(y5_streams, w, b, layer, tm, tn):
    n = w.shape[2]
    body = lambda y, wb, y_tile, bias: y_tile * jax.nn.sigmoid(_dot(y.astype(BF16), wb) + bias)
    return _fused_matmul("s5_glu", body, [([y], [(y, 0)]) for y in y5_streams], [w],
                         [b.reshape(b.shape[0], 1, n)], layer, n, tm, tn, BF16)


def merge_branches(branch_streams, proj_streams, w5, wg, wr, layer, tm, tn):
    def body(a5, ag, ar, w5b, wgb, wrb, m5, mg, mr):
        gate = lambda t: jax.nn.sigmoid(t.astype(F32))
        return gate(m5) * _dot(a5, w5b) + gate(mg) * _dot(ag, wgb) + gate(mr) * _dot(ar, wrb)

    streams = [(list(branches), [(proj, OFF_MG + i * D_MODEL) for i in range(N_BRANCH)])
               for branches, proj in zip(branch_streams, proj_streams)]
    return _fused_matmul("merge_branches", body, streams, [w5, wg, wr], [], layer, w5.shape[2],
                         tm, tn, BF16)


def residual_matmul(a_streams, w, x_streams, layer, tm, tn):
    body = lambda a, wb, x: x + _dot(a, wb)
    return _fused_matmul("residual_matmul", body, [([a], [(x, 0)]) for a, x in zip(a_streams, x_streams)],
                         [w], [], layer, w.shape[2], tm, tn, F32)


def ffn_up(h_streams, wg, wu, layer, tm, tn):
    body = lambda h, wgb, wub: _silu(_dot(h, wgb)) * _dot(h, wub)
    return _fused_matmul("ffn_up", body, [([h], []) for h in h_streams], [wg, wu], [], layer,
                         wg.shape[2], tm, tn, BF16)


def _s5_prep_kernel(ldt_ref, arc_ref, aic_ref, arr_ref, air_ref, b1_ref, crt_ref, cit_ref,
                    t_ref, p_ref, q_ref, bbt_ref, w0_ref):
    s = S5_CHUNK
    dt = jnp.exp(ldt_ref[0])
    ar_row, ai_row = arr_ref[0], air_ref[0]
    lam_r_row, lam_i_row = ar_row * dt, ai_row * dt
    lam_r_col, lam_i_col = arc_ref[0] * dt, aic_ref[0] * dt

    mag = jnp.exp(lam_r_row)
    abr, abi = mag * jnp.cos(lam_i_row), mag * jnp.sin(lam_i_row)
    den = ar_row * ar_row + ai_row * ai_row
    nr = abr - 1.0
    fr = (nr * ar_row + abi * ai_row) / den
    fi = (abi * ar_row - nr * ai_row) / den

    b1 = b1_ref[0]
    lane = lax.broadcasted_iota(jnp.int32, b1.shape, 1)
    b2 = pltpu.roll(b1, S5_STATE, axis=1) * jnp.where(lane < S5_STATE, -1.0, 1.0)
    bbt = fr * b1 + fi * b2
    bbt_sw = fr * b2 - fi * b1
    bbt_ref[0] = bbt

    tau_p = (s - 1 - lax.broadcasted_iota(jnp.int32, (s, LANES), 0)).astype(F32)
    mag_p = jnp.exp(lam_r_row * tau_p)
    ar_p, ai_p = mag_p * jnp.cos(lam_i_row * tau_p), mag_p * jnp.sin(lam_i_row * tau_p)
    repeat_rows = lambda a: jnp.concatenate(
        [jnp.broadcast_to(a[i:i + 1], (S5_GROUP, LANES)) for i in range(s)], axis=0)
    p_ref[0] = (jnp.concatenate([bbt] * s, axis=0) * repeat_rows(ar_p)
                + jnp.concatenate([bbt_sw] * s, axis=0) * repeat_rows(ai_p)).astype(p_ref.dtype)

    crt, cit = crt_ref[0], cit_ref[0]
    power = jnp.minimum(lax.broadcasted_iota(jnp.int32, (S5_STATE, LANES), 1), s).astype(F32)
    mag_c = jnp.exp(lam_r_col * power)
    ar_c, ai_c = mag_c * jnp.cos(lam_i_col * power), mag_c * jnp.sin(lam_i_col * power)
    src = lax.broadcasted_iota(jnp.int32, (LANES, s * S5_GROUP), 0)
    dst_block = jnp.right_shift(lax.broadcasted_iota(jnp.int32, (LANES, s * S5_GROUP), 1), 4)

    def coeffs(tau):
        spread = (src == dst_block + tau).astype(F32)
        ar_t = jnp.dot(ar_c, spread, precision=HIGHEST, preferred_element_type=F32)
        ai_t = jnp.dot(ai_c, spread, precision=HIGHEST, preferred_element_type=F32)
        return jnp.concatenate([ar_t * crt - ai_t * cit, -(ar_t * cit + ai_t * crt)], axis=0)

    w0 = coeffs(0)
    w0_ref[0] = w0
    q_ref[0] = coeffs(1).astype(q_ref.dtype)

    t0 = jnp.dot(bbt, w0, precision=HIGHEST, preferred_element_type=F32)
    lane_t = lax.broadcasted_iota(jnp.int32, t0.shape, 1)
    blocks = [t0]
    for i in range(1, s):
        blocks.append(jnp.where(lane_t >= i * S5_GROUP, pltpu.roll(t0, i * S5_GROUP, axis=1), 0.0))
    t_ref[0] = jnp.concatenate(blocks, axis=0).astype(t_ref.dtype)


def s5_prep(a_re, a_im, log_dt, b_re, b_im, c_re, c_im):
    g, p, n, s = S5_GROUPS, S5_STATE, S5_GROUP, S5_CHUNK
    dup = lambda a: jnp.concatenate([a, a], axis=-1)[:, None, :]
    b1 = jnp.concatenate([b_re.transpose(0, 2, 1), b_im.transpose(0, 2, 1)], axis=-1)
    crt = jnp.tile(c_re.transpose(0, 2, 1), (1, 1, s))
    cit = jnp.tile(c_im.transpose(0, 2, 1), (1, 1, s))
    spec = lambda *shape: pl.BlockSpec((1,) + shape, lambda i: (i, 0, 0))
    return pl.pallas_call(
        _s5_prep_kernel,
        grid=(g,),
        in_specs=[spec(1, 1), spec(p, 1), spec(p, 1), spec(1, 2 * p), spec(1, 2 * p),
                  spec(n, 2 * p), spec(p, s * n), spec(p, s * n)],
        out_specs=[spec(s * n, s * n), spec(s * n, 2 * p), spec(2 * p, s * n),
                   spec(n, 2 * p), spec(2 * p, s * n)],
        out_shape=[jax.ShapeDtypeStruct((g, s * n, s * n), BF16),
                   jax.ShapeDtypeStruct((g, s * n, 2 * p), BF16),
                   jax.ShapeDtypeStruct((g, 2 * p, s * n), BF16),
                   jax.ShapeDtypeStruct((g, n, 2 * p), F32),
                   jax.ShapeDtypeStruct((g, 2 * p, s * n), F32)],
        compiler_params=_params(1),
        name="s5_prep",
    )(log_dt[:, None, None], a_re[:, :, None], a_im[:, :, None], dup(a_re), dup(a_im), b1, crt, cit)


def _abar_power(ldt_ref, arr_ref, air_ref, power):
    dt = jnp.exp(ldt_ref[...])
    mag = jnp.exp(arr_ref[...] * dt * power)
    ang = air_ref[...] * dt * power
    lane = lax.broadcasted_iota(jnp.int32, mag.shape, 2)
    return mag * jnp.cos(ang), mag * jnp.sin(ang) * jnp.where(lane < S5_STATE, -1.0, 1.0)


def _complex_scale(x, a_r, a_i_signed):
    return a_r * x + a_i_signed * pltpu.roll(x, S5_STATE, axis=x.ndim - 1)


GROUPS_PER_BLOCK = LANES // S5_GROUP


def _s5_prompt_kernel(u_ref, t_ref, p_ref, q_ref, ldt_ref, arr_ref, air_ref, d_ref,
                      y_ref, fin_ref, ug_ref, z_ref, x_ref, yg_ref, *, bsz):
    s, n, gp = S5_CHUNK, S5_GROUP, GROUPS_PER_BLOCK
    rows = u_ref.shape[1]
    n_chunks = rows // bsz
    lane_block = jnp.right_shift(lax.broadcasted_iota(jnp.int32, (rows, LANES), 1), 4)

    for g in range(gp):
        for half in range(s // gp):
            acc = None
            for t8 in range(gp):
                piece = u_ref[half * gp + t8].astype(F32)
                shift = (n * (t8 - g)) % LANES
                if shift:
                    piece = pltpu.roll(piece, shift, axis=1)
                acc = piece if acc is None else jnp.where(lane_block == t8, piece, acc)
            ug_ref[g, :, half * LANES:(half + 1) * LANES] = acc.astype(BF16)

    for g in range(gp):
        z_ref[g] = _dot(ug_ref[g], p_ref[g])

    a_r, a_i = _abar_power(ldt_ref, arr_ref, air_ref, float(s))

    def step(j, x):
        idx = pl.ds(j, bsz, stride=n_chunks)
        x_ref[:, idx, :] = x
        return _complex_scale(x, a_r, a_i) + z_ref[:, idx, :]

    fin_ref[...] = lax.fori_loop(0, n_chunks, step, jnp.zeros(fin_ref.shape, F32))

    for g in range(gp):
        yg_ref[g] = _dot(ug_ref[g], t_ref[g]) + _dot(x_ref[g].astype(BF16), q_ref[g])

    for t in range(s):
        half, t8 = divmod(t, gp)
        acc = None
        for g in range(gp):
            piece = yg_ref[g, :, half * LANES:(half + 1) * LANES]
            shift = (n * (g - t8)) % LANES
            if shift:
                piece = pltpu.roll(piece, shift, axis=1)
            acc = piece if acc is None else jnp.where(lane_block == g, piece, acc)
        y_ref[t] = jax.nn.gelu(acc + d_ref[...] * u_ref[t].astype(F32))


def _steps_major(a, s):
    return a.reshape(a.shape[0] // s, s, a.shape[1]).swapaxes(0, 1)


def s5_prompt(u, bsz, length, t_mat, p_mat, q_mat, log_dt, a_re, a_im, d_skip):
    g, n, s, p2, gp = S5_GROUPS, S5_GROUP, S5_CHUNK, 2 * S5_STATE, GROUPS_PER_BLOCK
    rows = bsz * length // s
    dup = lambda a: jnp.concatenate([a, a], axis=-1)[:, None, :]
    gspec = lambda *shape: pl.BlockSpec((gp,) + shape, lambda k: (k, 0, 0))
    io_spec = pl.BlockSpec((s, rows, LANES), lambda k: (0, 0, k))
    return pl.pallas_call(
        functools.partial(_s5_prompt_kernel, bsz=bsz),
        grid=(g // gp,),
        in_specs=[io_spec,
                  gspec(s * n, s * n), gspec(s * n, p2), gspec(p2, s * n),
                  gspec(1, 1), gspec(1, p2), gspec(1, p2),
                  pl.BlockSpec((1, LANES), lambda k: (0, k))],
        out_specs=[io_spec, gspec(bsz, p2)],
        out_shape=[jax.ShapeDtypeStruct((s, rows, g * n), F32),
                   jax.ShapeDtypeStruct((g, bsz, p2), F32)],
        scratch_shapes=[pltpu.VMEM((gp, rows, s * n), BF16),
                        pltpu.VMEM((gp, rows, p2), F32),
                        pltpu.VMEM((gp, rows, p2), F32),
                        pltpu.VMEM((gp, rows, s * n), F32)],
        compiler_params=_params(1),
        name="s5_prompt",
    )(_steps_major(u, s), t_mat, p_mat, q_mat, log_dt[:, None, None], dup(a_re), dup(a_im),
      d_skip.reshape(1, g * n))


def _s5_sample_kernel(u_ref, xr_ref, xi_ref, bbt_ref, w0_ref, ldt_ref, arr_ref, air_ref, d_ref,
                      y_ref, nr_ref, ni_ref):
    p = S5_STATE
    dt = jnp.exp(ldt_ref[0])
    mag = jnp.exp(arr_ref[0][:, :p] * dt)
    ang = air_ref[0][:, :p] * dt
    abr, abi = mag * jnp.cos(ang), mag * jnp.sin(ang)
    u = u_ref[0].astype(F32)
    z = _dot(u.astype(BF16), bbt_ref[0].astype(BF16))
    xr, xi = xr_ref[0], xi_ref[0]
    new_r = abr * xr - abi * xi + z[:, :p]
    new_i = abr * xi + abi * xr + z[:, p:]
    nr_ref[0] = new_r
    ni_ref[0] = new_i
    c = w0_ref[0][:, :S5_GROUP].astype(BF16)
    y = _dot(new_r.astype(BF16), c[:p]) + _dot(new_i.astype(BF16), c[p:]) + d_ref[0] * u
    y_ref[0] = jax.nn.gelu(y)


def s5_sample(u, x_re, x_im, bbt, w0, log_dt, a_re, a_im, d_skip):
    g, n, p = S5_GROUPS, S5_GROUP, S5_STATE
    bsz = u.shape[0]
    dup = lambda a: jnp.concatenate([a, a], axis=-1)[:, None, :]
    gspec = lambda *shape: pl.BlockSpec((1,) + shape, lambda i: (i, 0, 0))
    y, new_r, new_i = pl.pallas_call(
        _s5_sample_kernel,
        grid=(g,),
        in_specs=[gspec(bsz, n), gspec(bsz, p), gspec(bsz, p), gspec(n, 2 * p),
                  gspec(2 * p, S5_CHUNK * n), gspec(1, 1), gspec(1, 2 * p), gspec(1, 2 * p),
                  gspec(1, n)],
        out_specs=[gspec(bsz, n), gspec(bsz, p), gspec(bsz, p)],
        out_shape=[jax.ShapeDtypeStruct((g, bsz, n), F32),
                   jax.ShapeDtypeStruct((g, bsz, p), F32),
                   jax.ShapeDtypeStruct((g, bsz, p), F32)],
        compiler_params=_params(1),
        name="s5_sample",
    )(u.reshape(bsz, g, n).transpose(1, 0, 2), x_re.transpose(1, 0, 2), x_im.transpose(1, 0, 2),
      bbt, w0, log_dt[:, None, None], dup(a_re), dup(a_im), d_skip.reshape(g, 1, n))
    return (y.transpose(1, 0, 2).reshape(bsz, g * n), new_r.transpose(1, 0, 2),
            new_i.transpose(1, 0, 2))


FACTORED_DECAY_LIMIT = -60.0


def _gla_kernel(q_ref, k_ref, v_ref, gr_ref, g_ref, nrm_ref, o_ref, st_ref, s_ref, b_ref,
                qb_ref, kb_ref, kend_ref, sc_ref, acc_ref):
    t = pl.program_id(0)

    @pl.when(t == 0)
    def _():
        s_ref[...] = jnp.zeros(s_ref.shape, F32)

    bsz, c = q_ref.shape[0], q_ref.shape[1]
    row = lax.broadcasted_iota(jnp.int32, (c, c), 0)
    col = lax.broadcasted_iota(jnp.int32, (c, c), 1)
    causal = col <= row
    ones_tri = causal.astype(F32)
    step = lax.broadcasted_iota(jnp.int32, (c, GLA_DK), 0)

    lowest = None
    for i in range(bsz):
        b_ref[i] = jnp.dot(ones_tri, g_ref[i], precision=HIGHEST, preferred_element_type=F32)
        end_min = jnp.min(b_ref[i, c - 1:c, :])
        lowest = end_min if lowest is None else jnp.minimum(lowest, end_min)
    factorable = lowest >= FACTORED_DECAY_LIMIT

    pairs = [(i, h) for i in range(bsz) for h in range(GLA_HEADS)]
    head_k = lambda h: slice(h * GLA_DK, (h + 1) * GLA_DK)
    head_v = lambda h: slice(h * GLA_DV, (h + 1) * GLA_DV)

    def finish(i, h, o):
        o = o * lax.rsqrt(jnp.mean(o * o, axis=-1, keepdims=True) + EPS) * nrm_ref[...]
        o_ref[i, :, head_v(h)] = (o * _silu(gr_ref[i, :, head_v(h)].astype(F32))).astype(o_ref.dtype)

    def all_chunks_factored():
        for n, (i, h) in enumerate(pairs):
            b = b_ref[i, :, head_k(h)]
            k = k_ref[i, :, head_k(h)].astype(F32)
            qb_ref[n] = (q_ref[i, :, head_k(h)].astype(F32) * (GLA_DK ** -0.5) * jnp.exp(b)).astype(BF16)
            kb_ref[n] = (k * jnp.exp(-b)).astype(BF16)
            kend_ref[n] = (k * jnp.exp(b[c - 1:c, :] - b)).astype(BF16)
        for n, (i, h) in enumerate(pairs):
            scores = lax.dot_general(qb_ref[n], kb_ref[n], NT, preferred_element_type=F32)
            sc_ref[n] = jnp.where(causal, scores, 0.0).astype(BF16)
        for n, (i, h) in enumerate(pairs):
            v = v_ref[i, :, head_v(h)].astype(BF16)
            s_t = s_ref[i, h]
            acc_ref[n] = _dot(sc_ref[n], v) + lax.dot_general(qb_ref[n], s_t.astype(BF16), NT,
                                                              preferred_element_type=F32)
            s_ref[i, h] = (jnp.exp(b_ref[i, c - 1:c, head_k(h)]) * s_t
                           + lax.dot_general(v, kend_ref[n], TN, preferred_element_type=F32))
        for n, (i, h) in enumerate(pairs):
            finish(i, h, acc_ref[n])

    def intra_stepwise(i, ks, vs, b, q_in, v):
        q = q_ref[i, :, ks].astype(F32) * (GLA_DK ** -0.5)

        def add_sources(tile, acc):
            first = pl.multiple_of(tile * ROW_TILE, ROW_TILE)
            rows = pl.ds(first, ROW_TILE)
            b_s = b_ref[i, rows, ks]
            k_s, v_s = k_ref[i, rows, ks].astype(F32), v_ref[i, rows, vs].astype(F32)
            for r in range(ROW_TILE):
                decay = jnp.exp(jnp.where(step >= first + r, b - b_s[r:r + 1], -jnp.inf))
                score = jnp.sum(q * k_s[r:r + 1] * decay, axis=1, keepdims=True)
                acc = acc + score * v_s[r:r + 1]
            return acc

        return lax.fori_loop(0, c // ROW_TILE, add_sources, jnp.zeros((c, GLA_DV), F32))

    def all_chunks_stepwise():
        for i, h in pairs:
            ks, vs = head_k(h), head_v(h)
            b = b_ref[i, :, ks]
            b_end = b[c - 1:c, :]
            q_in = (q_ref[i, :, ks].astype(F32) * (GLA_DK ** -0.5) * jnp.exp(b)).astype(BF16)
            k_end = (k_ref[i, :, ks].astype(F32) * jnp.exp(b_end - b)).astype(BF16)
            v = v_ref[i, :, vs].astype(BF16)
            s_t = s_ref[i, h]
            o = (intra_stepwise(i, ks, vs, b, q_in, v)
                 + lax.dot_general(q_in, s_t.astype(BF16), NT, preferred_element_type=F32))
            s_ref[i, h] = jnp.exp(b_end) * s_t + lax.dot_general(v, k_end, TN,
                                                                 preferred_element_type=F32)
            finish(i, h, o)

    pl.when(factorable)(all_chunks_factored)
    pl.when(jnp.logical_not(factorable))(all_chunks_stepwise)

    @pl.when(t == pl.num_programs(0) - 1)
    def _():
        st_ref[...] = s_ref[...]


def gla_prompt(proj, g, gla_norm, bsz, length):
    c = CHUNK
    nc = length // c
    hk, hv = GLA_HEADS * GLA_DK, GLA_HEADS * GLA_DV
    rows = lambda width, off: pl.BlockSpec((bsz, c, width), lambda t: (0, t, off // width))
    st_shape = (bsz, GLA_HEADS, GLA_DV, GLA_DK)
    proj3 = proj.reshape(bsz, length, proj.shape[1])
    o, st = pl.pallas_call(
        _gla_kernel,
        grid=(nc,),
        in_specs=[rows(hk, OFF_GQ), rows(hk, OFF_GK), rows(hv, OFF_GV), rows(hv, OFF_GR),
                  rows(hk, 0), pl.BlockSpec((1, GLA_DV), lambda t: (0, 0))],
        out_specs=[rows(hv, 0), pl.BlockSpec(st_shape, lambda t: (0, 0, 0, 0))],
        out_shape=[jax.ShapeDtypeStruct((bsz, length, hv), BF16),
                   jax.ShapeDtypeStruct(st_shape, F32)],
        scratch_shapes=[pltpu.VMEM(st_shape, F32),
                        pltpu.VMEM((bsz, c, hk), F32)]
        + [pltpu.VMEM((bsz * GLA_HEADS, c, GLA_DK), BF16)] * 3
        + [pltpu.VMEM((bsz * GLA_HEADS, c, c), BF16),
           pltpu.VMEM((bsz * GLA_HEADS, c, GLA_DV), F32)],
        compiler_params=_params(1),
        name="gla_prompt",
    )(proj3, proj3, proj3, proj3, g.reshape(bsz, length, hk), gla_norm.reshape(1, GLA_DV))
    return o.reshape(bsz * length, hv), st.swapaxes(2, 3)


def _rotate_pairs(x, cos2, sin2):
    lane = lax.broadcasted_iota(jnp.int32, x.shape, 1)
    partner = jnp.where(lane % 2 == 0, pltpu.roll(x, x.shape[1] - 1, axis=1), pltpu.roll(x, 1, axis=1))
    return x * cos2 + partner * sin2


def _ret_kernel(q_ref, k_ref, v_ref, rg_ref, cos_ref, sin_ref, dm_ref, qd_ref, kd_ref, gc_ref,
                o_ref, st_ref, s_ref, qb_ref, kb_ref, qdec_ref, kdec_ref, sc_ref, acc_ref):
    t = pl.program_id(0)

    @pl.when(t == 0)
    def _():
        s_ref[...] = jnp.zeros(s_ref.shape, F32)

    cos2, sin2 = cos_ref[...], sin_ref[...]
    pairs = [(i, h) for i in range(q_ref.shape[0]) for h in range(RET_HEADS)]
    head_k = lambda h: slice(h * RET_DK, (h + 1) * RET_DK)
    head_v = lambda h: slice(h * RET_DV, (h + 1) * RET_DV)
    for n, (i, h) in enumerate(pairs):
        q = _rotate_pairs(q_ref[i, :, head_k(h)].astype(F32), cos2, sin2)
        k = _rotate_pairs(k_ref[i, :, head_k(h)].astype(F32), cos2, sin2) * (RET_DK ** -0.5)
        qb_ref[n] = q.astype(BF16)
        kb_ref[n] = k.astype(BF16)
        qdec_ref[n] = (q * qd_ref[h]).astype(BF16)
        kdec_ref[n] = (k * kd_ref[h]).astype(BF16)
    for n, (i, h) in enumerate(pairs):
        scores = lax.dot_general(qb_ref[n], kb_ref[n], NT, preferred_element_type=F32) * dm_ref[h]
        sc_ref[n] = scores.astype(BF16)
    for n, (i, h) in enumerate(pairs):
        v = v_ref[i, :, head_v(h)].astype(BF16)
        s_t = s_ref[i, h]
        acc_ref[n] = _dot(sc_ref[n], v) + lax.dot_general(qdec_ref[n], s_t.astype(BF16), NT,
                                                          preferred_element_type=F32)
        s_ref[i, h] = gc_ref[h] * s_t + lax.dot_general(v, kdec_ref[n], TN, preferred_element_type=F32)
    for n, (i, h) in enumerate(pairs):
        o = acc_ref[n]
        oc = o - jnp.mean(o, axis=-1, keepdims=True)
        o = oc * lax.rsqrt(jnp.mean(oc * oc, axis=-1, keepdims=True) + EPS)
        o_ref[i, :, head_v(h)] = (o * _silu(rg_ref[i, :, head_v(h)].astype(F32))).astype(o_ref.dtype)

    @pl.when(t == pl.num_programs(0) - 1)
    def _():
        st_ref[...] = s_ref[...]


def _rotary_tables(pos):
    half = RET_DK // 2
    inv = 1.0 / (ROPE_BASE ** jnp.linspace(0.0, 1.0, half, dtype=F32))
    ang = pos[:, None] * inv[None, :]
    cos2 = jnp.repeat(jnp.cos(ang), 2, axis=-1)
    sin2 = jnp.stack([-jnp.sin(ang), jnp.sin(ang)], axis=-1).reshape(pos.shape[0], RET_DK)
    return cos2, sin2


def _log_gamma():
    return jnp.log1p(-jnp.power(2.0, -5.0 - jnp.arange(RET_HEADS, dtype=F32)))


def ret_prompt(proj, bsz, length):
    c = CHUNK
    nc = length // c
    hk, hv = RET_HEADS * RET_DK, RET_HEADS * RET_DV
    cos2, sin2 = _rotary_tables(jnp.arange(length, dtype=F32))
    lg = _log_gamma()[:, None, None]
    steps = jnp.arange(c, dtype=F32)
    lag = steps[:, None] - steps[None, :]
    dmat = jnp.where(lag >= 0, jnp.exp(lg * jnp.maximum(lag, 0.0)), 0.0)
    wide = lambda a: jnp.broadcast_to(a, (RET_HEADS, a.shape[1], LANES))
    qdec = wide(jnp.exp(lg * (steps + 1.0)[None, :, None]))
    kdec = wide(jnp.exp(lg * (c - 1.0 - steps)[None, :, None]))
    gchunk = wide(jnp.exp(lg * float(c)))
    rows = lambda width, off: pl.BlockSpec((bsz, c, width), lambda t: (0, t, off // width))
    table = pl.BlockSpec((c, RET_DK), lambda t: (t, 0))
    const = lambda a: pl.BlockSpec(a.shape, lambda t: (0, 0, 0))
    st_shape = (bsz, RET_HEADS, RET_DV, RET_DK)
    proj3 = proj.reshape(bsz, length, proj.shape[1])
    o, st = pl.pallas_call(
        _ret_kernel,
        grid=(nc,),
        in_specs=[rows(hk, OFF_RQ), rows(hk, OFF_RK), rows(hv, OFF_RV), rows(hv, OFF_RG),
                  table, table, const(dmat), const(qdec), const(kdec), const(gchunk)],
        out_specs=[rows(hv, 0), pl.BlockSpec(st_shape, lambda t: (0, 0, 0, 0))],
        out_shape=[jax.ShapeDtypeStruct((bsz, length, hv), BF16),
                   jax.ShapeDtypeStruct(st_shape, F32)],
        scratch_shapes=[pltpu.VMEM(st_shape, F32)]
        + [pltpu.VMEM((bsz * RET_HEADS, c, RET_DK), BF16)] * 4
        + [pltpu.VMEM((bsz * RET_HEADS, c, c), BF16),
           pltpu.VMEM((bsz * RET_HEADS, c, RET_DV), F32)],
        compiler_params=_params(1),
        name="ret_prompt",
    )(proj3, proj3, proj3, proj3, cos2, sin2, dmat, qdec, kdec, gchunk)
    return o.reshape(bsz * length, hv), st.swapaxes(2, 3)


def _one_column(a_t, i):
    col = lax.broadcasted_iota(jnp.int32, a_t.shape, 1)
    return jnp.where(col == i, a_t, 0.0).astype(BF16)


def _gla_sample_kernel(s_ref, q_ref, kt_ref, gt_ref, v_ref, gr_ref, nrm_ref, *rest):
    so_ref, o_ref, acc_ref = rest[-3:]
    ones = jnp.ones((SAMPLE_BLOCK, GLA_DV), BF16)
    head_k = lambda h: slice(h * GLA_DK, (h + 1) * GLA_DK)
    head_v = lambda h: slice(h * GLA_DV, (h + 1) * GLA_DV)
    for h in range(GLA_HEADS):
        v = v_ref[:, head_v(h)].astype(BF16)
        k_t = kt_ref[0, h]
        decay_t = jnp.exp(gt_ref[0, h])
        decay_hi = decay_t.astype(BF16).astype(F32)
        decay_lo = decay_t - decay_hi
        for i in range(SAMPLE_BLOCK):
            decay = _dot(_one_column(decay_hi, i), ones) + _dot(_one_column(decay_lo, i), ones)
            so_ref[i, h] = decay * s_ref[i, h] + _dot(_one_column(k_t, i), v)
    for h in range(GLA_HEADS):
        q = (q_ref[:, head_k(h)].astype(F32) * (GLA_DK ** -0.5)).astype(BF16)
        for i in range(SAMPLE_BLOCK):
            acc_ref[i:i + 1, head_v(h)] = _dot(q, so_ref[i, h].astype(BF16))[i:i + 1]
    for h in range(GLA_HEADS):
        o = acc_ref[:, head_v(h)]
        o = o * lax.rsqrt(jnp.mean(o * o, axis=-1, keepdims=True) + EPS) * nrm_ref[...]
        o_ref[:, head_v(h)] = (o * _silu(gr_ref[:, head_v(h)].astype(F32))).astype(o_ref.dtype)


def _columns(a, heads, dk):
    bsz = a.shape[0]
    return a.reshape(bsz // SAMPLE_BLOCK, SAMPLE_BLOCK, heads, dk).transpose(0, 2, 3, 1)


def _stacked_state_call(kernel_fn, name, state, layer, prev, heads, dk, dv, in_specs, operands,
                        o_spec, o_shape):
    bb = SAMPLE_BLOCK
    st_spec = pl.BlockSpec((None, bb, heads, dk, dv), lambda i: (layer, i, 0, 0, 0))
    aliases = {}
    if prev is not None:
        in_specs = in_specs + [pl.BlockSpec(memory_space=pl.ANY)]
        operands = operands + (prev,)
        aliases = {len(in_specs): 0}
    return pl.pallas_call(
        kernel_fn,
        grid=(state.shape[1] // bb,),
        in_specs=[st_spec] + in_specs,
        out_specs=[st_spec, o_spec],
        out_shape=[jax.ShapeDtypeStruct(state.shape, F32), o_shape],
        scratch_shapes=[pltpu.VMEM((bb, heads * dv), F32)],
        input_output_aliases=aliases,
        compiler_params=_params(1),
        name=name,
    )(state, *operands)


def gla_sample(proj, g, state, gla_norm, layer, prev):
    bsz = proj.shape[0]
    bb = SAMPLE_BLOCK
    hk, hv = GLA_HEADS * GLA_DK, GLA_HEADS * GLA_DV
    col_spec = pl.BlockSpec((1, GLA_HEADS, GLA_DK, bb), lambda i: (i, 0, 0, 0))
    rows = lambda off: pl.BlockSpec((bb, hv), lambda i: (i, off // hv))
    return _stacked_state_call(
        _gla_sample_kernel, "gla_sample", state, layer, prev, GLA_HEADS, GLA_DK, GLA_DV,
        [pl.BlockSpec((bb, hk), lambda i: (i, OFF_GQ // hk)), col_spec, col_spec, rows(OFF_GV),
         rows(OFF_GR), pl.BlockSpec((None, 1, GLA_DV), lambda i: (layer, 0, 0))],
        (proj, _columns(proj[:, OFF_GK:OFF_GK + hk], GLA_HEADS, GLA_DK), _columns(g, GLA_HEADS, GLA_DK),
         proj, proj, gla_norm.reshape(gla_norm.shape[0], 1, GLA_DV)),
        rows(0), jax.ShapeDtypeStruct((bsz, hv), BF16))


def _rot_kernel(q_ref, k_ref, cos_ref, sin_ref, qo_ref, ko_ref):
    cos2, sin2 = cos_ref[...], sin_ref[...]
    for h in range(RET_HEADS):
        ks = slice(h * RET_DK, (h + 1) * RET_DK)
        qo_ref[:, ks] = _rotate_pairs(q_ref[:, ks].astype(F32), cos2, sin2)
        ko_ref[:, ks] = _rotate_pairs(k_ref[:, ks].astype(F32), cos2, sin2) * (RET_DK ** -0.5)


def _ret_sample_kernel(s_ref, q_ref, kt_ref, gam_ref, v_ref, rg_ref, *rest):
    so_ref, o_ref, acc_ref = rest[-3:]
    head_k = lambda h: slice(h * RET_DK, (h + 1) * RET_DK)
    head_v = lambda h: slice(h * RET_DV, (h + 1) * RET_DV)
    for h in range(RET_HEADS):
        v = v_ref[:, head_v(h)].astype(BF16)
        k_t = kt_ref[0, h]
        for i in range(SAMPLE_BLOCK):
            so_ref[i, h] = gam_ref[h] * s_ref[i, h] + _dot(_one_column(k_t, i), v)
    for h in range(RET_HEADS):
        q = q_ref[:, head_k(h)].astype(BF16)
        for i in range(SAMPLE_BLOCK):
            acc_ref[i:i + 1, head_v(h)] = _dot(q, so_ref[i, h].astype(BF16))[i:i + 1]
    for h in range(RET_HEADS):
        o = acc_ref[:, head_v(h)]
        oc = o - jnp.mean(o, axis=-1, keepdims=True)
        o = oc * lax.rsqrt(jnp.mean(oc * oc, axis=-1, keepdims=True) + EPS)
        o_ref[:, head_v(h)] = (o * _silu(rg_ref[:, head_v(h)].astype(F32))).astype(o_ref.dtype)


def ret_sample(proj, state, layer, prev):
    bsz = proj.shape[0]
    bb = SAMPLE_BLOCK
    hk, hv = RET_HEADS * RET_DK, RET_HEADS * RET_DV
    cos2, sin2 = _rotary_tables(PAST_LEN + jnp.arange(1, dtype=F32))
    full = pl.BlockSpec((bsz, hk), lambda i: (0, 0))
    q_rot, k_rot = pl.pallas_call(
        _rot_kernel,
        grid=(1,),
        in_specs=[pl.BlockSpec((bsz, hk), lambda i: (0, OFF_RQ // hk)),
                  pl.BlockSpec((bsz, hk), lambda i: (0, OFF_RK // hk)),
                  pl.BlockSpec((1, RET_DK), lambda i: (0, 0)), pl.BlockSpec((1, RET_DK), lambda i: (0, 0))],
        out_specs=[full, full],
        out_shape=[jax.ShapeDtypeStruct((bsz, hk), F32)] * 2,
        compiler_params=_params(1),
        name="ret_rotary",
    )(proj, proj, cos2, sin2)
    gamma = jnp.broadcast_to(jnp.exp(_log_gamma())[:, None, None], (RET_HEADS, 1, LANES))
    col_spec = pl.BlockSpec((1, RET_HEADS, RET_DK, bb), lambda i: (i, 0, 0, 0))
    rows = lambda off: pl.BlockSpec((bb, hv), lambda i: (i, off // hv))
    return _stacked_state_call(
        _ret_sample_kernel, "ret_sample", state, layer, prev, RET_HEADS, RET_DK, RET_DV,
        [rows(0), col_spec, pl.BlockSpec(gamma.shape, lambda i: (0, 0, 0)), rows(OFF_RV), rows(OFF_RG)],
        (q_rot, _columns(k_rot, RET_HEADS, RET_DK), gamma, proj, proj),
        rows(0), jax.ShapeDtypeStruct((bsz, hv), BF16))


def _layer(xp, xs, layer, prompt_shape, states, w, s5_mats):
    t_mat, p_mat, q_mat, bbt, w0 = s5_mats
    bsz, length = prompt_shape
    mp, ms = xp.shape[0], xs.shape[0]
    tm, tm_wide, tn = min(mp, 512), min(mp, 1024), 512
    s5_par = (w["s5_log_dt"][layer], w["s5_a_re"][layer], w["s5_a_im"][layer], w["s5_d"][layer])
    st_s5_re, st_s5_im, st_gla, st_ret, prev_gla, prev_ret = states

    hp = rmsnorm(xp, w["norm_mix"][layer], BF16)
    hs = rmsnorm(xs, w["norm_mix"][layer], BF16)
    proj_p, proj_s = in_proj(hp, hs, w["w_in_rows"], layer, tm_wide, 2 * tn)
    g_p = gla_gate(hp, w["w_in_rows"], w["gla_w_gate"], w["gla_b_gate"], layer, tm)
    g_s = gla_gate(hs, w["w_in_rows"], w["gla_w_gate"], w["gla_b_gate"], layer, ms)

    y5_p, x_fin = s5_prompt(proj_p[:, OFF_U:OFF_U + S5_WIDTH], bsz, length, t_mat, p_mat, q_mat, *s5_par)
    y5_p = y5_p.reshape(mp, S5_WIDTH)
    x_fin = x_fin.transpose(1, 0, 2)
    o_gla_p, new_gla_p = gla_prompt(proj_p, g_p, w["gla_norm"][layer], bsz, length)
    o_ret_p, new_ret_p = ret_prompt(proj_p, bsz, length)

    y5_s, new_s5_re_s, new_s5_im_s = s5_sample(proj_s[:, OFF_U:OFF_U + S5_WIDTH], st_s5_re[layer],
                                               st_s5_im[layer], bbt, w0, *s5_par)
    new_gla_s, o_gla_s = gla_sample(proj_s, g_s, st_gla, w["gla_norm"], layer, prev_gla)
    new_ret_s, o_ret_s = ret_sample(proj_s, st_ret, layer, prev_ret)

    a5_p, a5_s = s5_glu([y5_p, y5_s], w["s5_w_glu"], w["s5_b_glu"], layer, tm, 2 * tn)
    a5_p = a5_p.reshape(S5_CHUNK, mp // S5_CHUNK, S5_WIDTH).swapaxes(0, 1).reshape(mp, S5_WIDTH)
    merged = merge_branches([(a5_p, o_gla_p, o_ret_p), (a5_s, o_gla_s, o_ret_s)], [proj_p, proj_s],
                            w["s5_w_out"], w["gla_w_out"], w["ret_w_out"], layer, tm, 2 * tn)
    xp, xs = residual_matmul(merged, w["w_mix_out"], [xp, xs], layer, tm_wide, 2 * tn)
    h2 = [rmsnorm(xp, w["norm_ffn"][layer], BF16), rmsnorm(xs, w["norm_ffn"][layer], BF16)]
    ff = ffn_up(h2, w["w_ffn_gate"], w["w_ffn_up"], layer, tm_wide, tn)
    xp, xs = residual_matmul(ff, w["w_ffn_down"], [xp, xs], layer, tm_wide, tn)
    new_p = (x_fin[..., :S5_STATE], x_fin[..., S5_STATE:], new_gla_p, new_ret_p)
    return xp, xs, new_p, (new_s5_re_s, new_s5_im_s, new_gla_s, new_ret_s)


def kernel(x_prompt, x_sample, state_s5_re, state_s5_im, state_gla, state_ret, norm_mix, w_in, s5_a_re, s5_a_im, s5_log_dt, s5_b_re, s5_b_im, s5_c_re, s5_c_im, s5_d, s5_w_glu, s5_b_glu, s5_w_out, gla_w_gate, gla_b_gate, gla_norm, gla_w_out, ret_w_out, w_mix_out, norm_ffn, w_ffn_gate, w_ffn_up, w_ffn_down, norm_final):
    bp, lp, d = x_prompt.shape
    bs = x_sample.shape[0]
    depth = w_in.shape[0]
    hp = x_prompt.reshape(bp * lp, d)
    hs = x_sample.reshape(bs, d)
    w = dict(norm_mix=norm_mix, s5_log_dt=s5_log_dt, s5_a_re=s5_a_re, s5_a_im=s5_a_im, s5_d=s5_d,
             s5_w_glu=s5_w_glu, s5_b_glu=s5_b_glu, s5_w_out=s5_w_out, gla_w_gate=gla_w_gate,
             gla_b_gate=gla_b_gate, gla_norm=gla_norm, gla_w_out=gla_w_out, ret_w_out=ret_w_out,
             w_mix_out=w_mix_out, norm_ffn=norm_ffn, w_ffn_gate=w_ffn_gate, w_ffn_up=w_ffn_up,
             w_ffn_down=w_ffn_down,
             w_in_rows=jnp.swapaxes(w_in, 1, 2).reshape(depth, w_in.shape[2] // GLA_RANK, GLA_RANK, d))
    outs_p, outs_s = [], []
    new_gla_s = new_ret_s = None
    for l in range(depth):
        s5_mats = s5_prep(s5_a_re[l], s5_a_im[l], s5_log_dt[l], s5_b_re[l], s5_b_im[l], s5_c_re[l],
                          s5_c_im[l])
        hp, hs, new_p, (s5r, s5i, new_gla_s, new_ret_s) = _layer(
            hp, hs, l, (bp, lp), (state_s5_re, state_s5_im, state_gla, state_ret, new_gla_s, new_ret_s),
            w, s5_mats)
        outs_p.append(new_p)
        outs_s.append((s5r, s5i))
    y_prompt = rmsnorm(hp, norm_final, F32).reshape(bp, lp, d)
    y_sample = rmsnorm(hs, norm_final, F32).reshape(bs, 1, d)
    stack = lambda outs, i: jnp.stack([o[i] for o in outs])
    return (y_prompt, y_sample,
            stack(outs_p, 0), stack(outs_p, 1), stack(outs_p, 2), stack(outs_p, 3),
            stack(outs_s, 0), stack(outs_s, 1), new_gla_s, new_ret_s)
```

```python
import functools
import math

import jax
import jax.numpy as jnp
import numpy as np
from jax import lax
from jax.experimental import pallas as pl
from jax.experimental.pallas import tpu as pltpu

F32 = jnp.float32
BF16 = jnp.bfloat16
HIGHEST = lax.Precision.HIGHEST

D_MODEL = 2048
S5_WIDTH = D_MODEL // 2
S5_GROUP = 16
S5_GROUPS = S5_WIDTH // S5_GROUP
S5_STATE = 64
S5_CHUNK = 16
GLA_HEADS = 4
GLA_DK = D_MODEL // 16
GLA_DV = D_MODEL // 8
GLA_RANK = 16
GLA_GATE_NORM = 16.0
RET_HEADS = 8
RET_DK = D_MODEL // 16
RET_DV = D_MODEL // 16
ROPE_BASE = 10000.0
PAST_LEN = 16384
N_BRANCH = 3
D_FF = -(-8 * D_MODEL // (3 * 256)) * 256
EPS = 1e-6
CHUNK = 128
LANES = 128
ROW_TILE = 16
SAMPLE_BLOCK = 16

OFF_U = 0
OFF_GQ = OFF_U + S5_WIDTH
OFF_GK = OFF_GQ + GLA_HEADS * GLA_DK
OFF_GV = OFF_GK + GLA_HEADS * GLA_DK
OFF_GR = OFF_GV + GLA_HEADS * GLA_DV
OFF_RQ = OFF_GR + GLA_HEADS * GLA_DV
OFF_RK = OFF_RQ + RET_HEADS * RET_DK
OFF_RV = OFF_RK + RET_HEADS * RET_DK
OFF_RG = OFF_RV + RET_HEADS * RET_DV
OFF_MG = OFF_RG + RET_HEADS * RET_DV
W_MAIN = OFF_MG + N_BRANCH * D_MODEL
GLR_COL = OFF_RQ

VMEM_LIMIT = 56 * 1024 * 1024

NT = (((1,), (1,)), ((), ()))
TN = (((0,), (0,)), ((), ()))


def _params(n_grid):
    return pltpu.CompilerParams(dimension_semantics=("arbitrary",) * n_grid,
                                vmem_limit_bytes=VMEM_LIMIT)


def _silu(x):
    return x * jax.nn.sigmoid(x)


def _rmsnorm_kernel(x_ref, g_ref, o_ref):
    x = x_ref[...]
    y = x * lax.rsqrt(jnp.mean(x * x, axis=-1, keepdims=True) + EPS)
    o_ref[...] = (y * g_ref[...]).astype(o_ref.dtype)


def rmsnorm(x, gain, out_dtype):
    m, d = x.shape
    tm = min(m, 512)
    return pl.pallas_call(
        _rmsnorm_kernel,
        grid=(m // tm,),
        in_specs=[pl.BlockSpec((tm, d), lambda i: (i, 0)),
                  pl.BlockSpec((1, d), lambda i: (0, 0))],
        out_specs=pl.BlockSpec((tm, d), lambda i: (i, 0)),
        out_shape=jax.ShapeDtypeStruct((m, d), out_dtype),
        compiler_params=_params(1),
        name="rmsnorm",
    )(x, gain.reshape(1, d))


def _dot(a, b):
    return jnp.dot(a, b, preferred_element_type=F32)


def _in_proj_kernel(a_ref, as_ref, w_ref, o_ref, os_ref, wb_ref):
    i = pl.program_id(1)

    @pl.when(i == 0)
    def _():
        wb_ref[...] = w_ref[...].reshape(wb_ref.shape).astype(BF16)

    def project(src_ref, dst_ref):
        dst_ref[...] = lax.dot_general(src_ref[...], wb_ref[...], NT,
                                       preferred_element_type=F32).astype(dst_ref.dtype)

    pl.when(i == 0)(lambda: project(as_ref, os_ref))
    pl.when(i > 0)(lambda: project(a_ref, o_ref))


def in_proj(h, h_s, w_rows, layer, tm, tn):
    m, k = h.shape
    ms = h_s.shape[0]
    nb = m // tm
    rb = tn // GLA_RANK
    skip_from = GLR_COL // tn

    def w_index(j, i):
        return (layer, rb * j + jnp.where(j >= skip_from, 1, 0), 0, 0)

    return pl.pallas_call(
        _in_proj_kernel,
        grid=(W_MAIN // tn, nb + 1),
        in_specs=[pl.BlockSpec((tm, k), lambda j, i: (jnp.maximum(i - 1, 0), 0)),
                  pl.BlockSpec((ms, k), lambda j, i: (0, 0)),
                  pl.BlockSpec((pl.Element(1), pl.Element(rb), pl.Element(GLA_RANK), pl.Element(k)),
                               w_index)],
        out_specs=[pl.BlockSpec((tm, tn), lambda j, i: (jnp.maximum(i - 1, 0), j)),
                   pl.BlockSpec((ms, tn), lambda j, i: (0, j))],
        out_shape=[jax.ShapeDtypeStruct((m, W_MAIN), BF16), jax.ShapeDtypeStruct((ms, W_MAIN), BF16)],
        scratch_shapes=[pltpu.VMEM((tn, k), BF16)],
        compiler_params=_params(2),
        name="in_proj",
    )(h, h_s, w_rows)


def _fused_matmul(name, body, streams, weights, biases, layer, n, tm, tn, out_dtype):
    nb = streams[0][0][0].shape[0] // tm
    extra = len(streams) - 1
    specs, operands, counts, out_specs, out_shapes = [], [], [], [], []
    for s, (row_ops, tile_ops) in enumerate(streams):
        m_s = row_ops[0].shape[0]
        rows = tm if s == 0 else m_s
        row_block = (lambda i: jnp.maximum(i - extra, 0)) if s == 0 else (lambda i: 0)
        for a in row_ops:
            specs.append(pl.BlockSpec((rows, a.shape[1]), lambda j, i, r=row_block: (r(i), 0)))
            operands.append(a)
        for a, first in tile_ops:
            specs.append(pl.BlockSpec((rows, tn), lambda j, i, r=row_block, c=first // tn: (r(i), c + j)))
            operands.append(a)
        counts.append((len(row_ops), len(tile_ops)))
        out_specs.append(pl.BlockSpec((rows, tn), lambda j, i, r=row_block: (r(i), j)))
        out_shapes.append(jax.ShapeDtypeStruct((m_s, n), out_dtype))
    for wt in weights:
        specs.append(pl.BlockSpec((None, wt.shape[1], tn), lambda j, i: (layer, 0, j),
                                  pipeline_mode=pl.Buffered(1)))
        operands.append(wt)
    for b in biases:
        specs.append(pl.BlockSpec((None, 1, tn), lambda j, i: (layer, 0, j)))
        operands.append(b)

    def kernel_fn(*refs):
        refs = list(refs)
        stream_refs = [(tuple(refs.pop(0) for _ in range(nr)), tuple(refs.pop(0) for _ in range(nt)))
                       for nr, nt in counts]
        w_refs = [refs.pop(0) for _ in weights]
        b_refs = [refs.pop(0) for _ in biases]
        o_refs = [refs.pop(0) for _ in streams]
        wb_refs = refs
        i = pl.program_id(1)

        @pl.when(i == 0)
        def _():
            for w_ref, wb_ref in zip(w_refs, wb_refs):
                wb_ref[...] = w_ref[...].astype(BF16)

        def run(s):
            row_refs, tile_refs = stream_refs[s]
            out = body(*[r[...] for r in row_refs], *[r[...] for r in wb_refs],
                       *[r[...] for r in tile_refs], *[r[...] for r in b_refs])
            o_refs[s][...] = out.astype(out_dtype)

        for s in range(1, len(streams)):
            pl.when(i == s - 1)(functools.partial(run, s))
        pl.when(i >= extra)(functools.partial(run, 0))

    return pl.pallas_call(
        kernel_fn,
        grid=(n // tn, nb + len(streams) - 1),
        in_specs=specs,
        out_specs=out_specs,
        out_shape=out_shapes,
        scratch_shapes=[pltpu.VMEM((wt.shape[1], tn), BF16) for wt in weights],
        compiler_params=_params(2),
        name=name,
    )(*operands)


def _norm_gate_kernel(x_ref, gain_ref, wl_ref, wg_ref, b_ref, h_ref, g_ref):
    x = x_ref[...]
    h = (x * lax.rsqrt(jnp.mean(x * x, axis=-1, keepdims=True) + EPS) * gain_ref[...]).astype(BF16)
    h_ref[...] = h
    pad = LANES - GLA_RANK
    wl = jnp.concatenate([wl_ref[0], jnp.zeros((pad, wl_ref.shape[2]), F32)], axis=0).astype(BF16)
    wg = jnp.concatenate([wg_ref[...], jnp.zeros((pad, wg_ref.shape[1]), F32)], axis=0).astype(BF16)
    glr = lax.dot_general(h, wl, NT, preferred_element_type=F32)
    logit = _dot(glr.astype(BF16), wg) + b_ref[...]
    log_sig = jnp.minimum(logit, 0.0) - jnp.log1p(jnp.exp(-jnp.abs(logit)))
    g_ref[...] = log_sig / GLA_GATE_NORM


def norm_and_gate(x, gain, w_rows, w_gate, b_gate, layer):
    m, k = x.shape
    n = w_gate.shape[2]
    tm = min(m, 512)
    return pl.pallas_call(
        _norm_gate_kernel,
        grid=(m // tm,),
        in_specs=[pl.BlockSpec((tm, k), lambda i: (i, 0)),
                  pl.BlockSpec((1, k), lambda i: (0, 0)),
                  pl.BlockSpec((None, 1, GLA_RANK, k), lambda i: (layer, GLR_COL // GLA_RANK, 0, 0)),
                  pl.BlockSpec((None, GLA_RANK, n), lambda i: (layer, 0, 0)),
                  pl.BlockSpec((None, 1, n), lambda i: (layer, 0, 0))],
        out_specs=[pl.BlockSpec((tm, k), lambda i: (i, 0)), pl.BlockSpec((tm, n), lambda i: (i, 0))],
        out_shape=[jax.ShapeDtypeStruct((m, k), BF16), jax.ShapeDtypeStruct((m, n), F32)],
        compiler_params=_params(1),
        name="norm_and_gate",
    )(x, gain.reshape(1, k), w_rows, w_gate, b_gate.reshape(b_gate.shape[0], 1, n))


def s5_glu(y5_streams, w, b, layer, tm, tn):
    n = w.shape[2]
    body = lambda y, wb, y_tile, bias: y_tile * jax.nn.sigmoid(_dot(y.astype(BF16), wb) + bias)
    return _fused_matmul("s5_glu", body, [([y], [(y, 0)]) for y in y5_streams], [w],
                         [b.reshape(b.shape[0], 1, n)], layer, n, tm, tn, BF16)


def merge_branches(branch_streams, proj_streams, w5, wg, wr, layer, tm, tn):
    def body(a5, ag, ar, w5b, wgb, wrb, m5, mg, mr):
        gate = lambda t: jax.nn.sigmoid(t.astype(F32))
        return gate(m5) * _dot(a5, w5b) + gate(mg) * _dot(ag, wgb) + gate(mr) * _dot(ar, wrb)

    streams = [(list(branches), [(proj, OFF_MG + i * D_MODEL) for i in range(N_BRANCH)])
               for branches, proj in zip(branch_streams, proj_streams)]
    return _fused_matmul("merge_branches", body, streams, [w5, wg, wr], [], layer, w5.shape[2],
                         tm, tn, BF16)


def residual_matmul(a_streams, w, x_streams, layer, tm, tn):
    body = lambda a, wb, x: x + _dot(a, wb)
    return _fused_matmul("residual_matmul", body, [([a], [(x, 0)]) for a, x in zip(a_streams, x_streams)],
                         [w], [], layer, w.shape[2], tm, tn, F32)


def ffn_up(h_streams, wg, wu, layer, tm, tn):
    body = lambda h, wgb, wub: _silu(_dot(h, wgb)) * _dot(h, wub)
    return _fused_matmul("ffn_up", body, [([h], []) for h in h_streams], [wg, wu], [], layer,
                         wg.shape[2], tm, tn, BF16)


def _s5_prep_kernel(*refs):
    for g in range(refs[0].shape[0]):
        _s5_prep_group(*[r.at[g:g + 1] for r in refs])


def _s5_prep_group(ldt_ref, arc_ref, aic_ref, arr_ref, air_ref, b1_ref, crt_ref, cit_ref,
                   t_ref, p_ref, q_ref, bbt_ref, w0_ref):
    s = S5_CHUNK
    dt = jnp.exp(ldt_ref[0])
    ar_row, ai_row = arr_ref[0], air_ref[0]
    lam_r_row, lam_i_row = ar_row * dt, ai_row * dt
    lam_r_col, lam_i_col = arc_ref[0] * dt, aic_ref[0] * dt

    mag = jnp.exp(lam_r_row)
    abr, abi = mag * jnp.cos(lam_i_row), mag * jnp.sin(lam_i_row)
    den = ar_row * ar_row + ai_row * ai_row
    nr = abr - 1.0
    fr = (nr * ar_row + abi * ai_row) / den
    fi = (abi * ar_row - nr * ai_row) / den

    b1 = b1_ref[0]
    lane = lax.broadcasted_iota(jnp.int32, b1.shape, 1)
    b2 = pltpu.roll(b1, S5_STATE, axis=1) * jnp.where(lane < S5_STATE, -1.0, 1.0)
    bbt = fr * b1 + fi * b2
    bbt_sw = fr * b2 - fi * b1
    bbt_ref[0] = bbt

    tau_p = (s - 1 - lax.broadcasted_iota(jnp.int32, (s, LANES), 0)).astype(F32)
    mag_p = jnp.exp(lam_r_row * tau_p)
    ar_p, ai_p = mag_p * jnp.cos(lam_i_row * tau_p), mag_p * jnp.sin(lam_i_row * tau_p)
    repeat_rows = lambda a: jnp.concatenate(
        [jnp.broadcast_to(a[i:i + 1], (S5_GROUP, LANES)) for i in range(s)], axis=0)
    p_ref[0] = (jnp.concatenate([bbt] * s, axis=0) * repeat_rows(ar_p)
                + jnp.concatenate([bbt_sw] * s, axis=0) * repeat_rows(ai_p)).astype(p_ref.dtype)

    crt, cit = crt_ref[0], cit_ref[0]
    power = jnp.minimum(lax.broadcasted_iota(jnp.int32, (S5_STATE, LANES), 1), s).astype(F32)
    mag_c = jnp.exp(lam_r_col * power)
    ar_c, ai_c = mag_c * jnp.cos(lam_i_col * power), mag_c * jnp.sin(lam_i_col * power)
    src = lax.broadcasted_iota(jnp.int32, (LANES, s * S5_GROUP), 0)
    dst_block = jnp.right_shift(lax.broadcasted_iota(jnp.int32, (LANES, s * S5_GROUP), 1), 4)

    def coeffs(tau):
        spread = (src == dst_block + tau).astype(F32)
        ar_t = jnp.dot(ar_c, spread, precision=HIGHEST, preferred_element_type=F32)
        ai_t = jnp.dot(ai_c, spread, precision=HIGHEST, preferred_element_type=F32)
        return jnp.concatenate([ar_t * crt - ai_t * cit, -(ar_t * cit + ai_t * crt)], axis=0)

    w0 = coeffs(0)
    w0_ref[0] = w0
    q_ref[0] = coeffs(1).astype(q_ref.dtype)

    t0 = jnp.dot(bbt, w0, precision=HIGHEST, preferred_element_type=F32)
    lane_t = lax.broadcasted_iota(jnp.int32, t0.shape, 1)
    blocks = [t0]
    for i in range(1, s):
        blocks.append(jnp.where(lane_t >= i * S5_GROUP, pltpu.roll(t0, i * S5_GROUP, axis=1), 0.0))
    t_ref[0] = jnp.concatenate(blocks, axis=0).astype(t_ref.dtype)


def s5_prep(a_re, a_im, log_dt, b_re, b_im, c_re, c_im):
    g, p, n, s = S5_GROUPS, S5_STATE, S5_GROUP, S5_CHUNK
    dup = lambda a: jnp.concatenate([a, a], axis=-1)[:, None, :]
    b1 = jnp.concatenate([b_re.transpose(0, 2, 1), b_im.transpose(0, 2, 1)], axis=-1)
    crt = jnp.tile(c_re.transpose(0, 2, 1), (1, 1, s))
    cit = jnp.tile(c_im.transpose(0, 2, 1), (1, 1, s))
    spec = lambda *shape: pl.BlockSpec((GROUPS_PER_BLOCK,) + shape, lambda i: (i, 0, 0))
    return pl.pallas_call(
        _s5_prep_kernel,
        grid=(g // GROUPS_PER_BLOCK,),
        in_specs=[spec(1, 1), spec(p, 1), spec(p, 1), spec(1, 2 * p), spec(1, 2 * p),
                  spec(n, 2 * p), spec(p, s * n), spec(p, s * n)],
        out_specs=[spec(s * n, s * n), spec(s * n, 2 * p), spec(2 * p, s * n),
                   spec(n, 2 * p), spec(2 * p, s * n)],
        out_shape=[jax.ShapeDtypeStruct((g, s * n, s * n), BF16),
                   jax.ShapeDtypeStruct((g, s * n, 2 * p), BF16),
                   jax.ShapeDtypeStruct((g, 2 * p, s * n), BF16),
                   jax.ShapeDtypeStruct((g, n, 2 * p), F32),
                   jax.ShapeDtypeStruct((g, 2 * p, s * n), F32)],
        compiler_params=_params(1),
        name="s5_prep",
    )(log_dt[:, None, None], a_re[:, :, None], a_im[:, :, None], dup(a_re), dup(a_im), b1, crt, cit)


def _abar_power(ldt_ref, arr_ref, air_ref, power):
    dt = jnp.exp(ldt_ref[...])
    mag = jnp.exp(arr_ref[...] * dt * power)
    ang = air_ref[...] * dt * power
    lane = lax.broadcasted_iota(jnp.int32, mag.shape, 2)
    return mag * jnp.cos(ang), mag * jnp.sin(ang) * jnp.where(lane < S5_STATE, -1.0, 1.0)


def _complex_scale(x, a_r, a_i_signed):
    return a_r * x + a_i_signed * pltpu.roll(x, S5_STATE, axis=x.ndim - 1)


GROUPS_PER_BLOCK = LANES // S5_GROUP


def _s5_prompt_kernel(u_ref, t_ref, p_ref, q_ref, ldt_ref, arr_ref, air_ref, d_ref,
                      y_ref, fin_ref, ug_ref, z_ref, x_ref, yg_ref, *, bsz):
    s, n, gp = S5_CHUNK, S5_GROUP, GROUPS_PER_BLOCK
    rows = u_ref.shape[1]
    n_chunks = rows // bsz
    lane_block = jnp.right_shift(lax.broadcasted_iota(jnp.int32, (rows, LANES), 1), 4)

    for g in range(gp):
        for half in range(s // gp):
            acc = None
            for t8 in range(gp):
                piece = u_ref[half * gp + t8].astype(F32)
                shift = (n * (t8 - g)) % LANES
                if shift:
                    piece = pltpu.roll(piece, shift, axis=1)
                acc = piece if acc is None else jnp.where(lane_block == t8, piece, acc)
            ug_ref[g, :, half * LANES:(half + 1) * LANES] = acc.astype(BF16)

    for g in range(gp):
        z_ref[g] = _dot(ug_ref[g], p_ref[g])

    a_r, a_i = _abar_power(ldt_ref, arr_ref, air_ref, float(s))

    def step(j, x):
        idx = pl.ds(j, bsz, stride=n_chunks)
        x_ref[:, idx, :] = x
        return _complex_scale(x, a_r, a_i) + z_ref[:, idx, :]

    fin_ref[...] = lax.fori_loop(0, n_chunks, step, jnp.zeros(fin_ref.shape, F32))

    for g in range(gp):
        yg_ref[g] = _dot(ug_ref[g], t_ref[g]) + _dot(x_ref[g].astype(BF16), q_ref[g])

    for t in range(s):
        half, t8 = divmod(t, gp)
        acc = None
        for g in range(gp):
            piece = yg_ref[g, :, half * LANES:(half + 1) * LANES]
            shift = (n * (g - t8)) % LANES
            if shift:
                piece = pltpu.roll(piece, shift, axis=1)
            acc = piece if acc is None else jnp.where(lane_block == g, piece, acc)
        y_ref[t] = jax.nn.gelu(acc + d_ref[...] * u_ref[t].astype(F32))


def _steps_major(a, s):
    return a.reshape(a.shape[0] // s, s, a.shape[1]).swapaxes(0, 1)


def s5_prompt(u, bsz, length, t_mat, p_mat, q_mat, log_dt, a_re, a_im, d_skip):
    g, n, s, p2, gp = S5_GROUPS, S5_GROUP, S5_CHUNK, 2 * S5_STATE, GROUPS_PER_BLOCK
    rows = bsz * length // s
    dup = lambda a: jnp.concatenate([a, a], axis=-1)[:, None, :]
    gspec = lambda *shape: pl.BlockSpec((gp,) + shape, lambda k: (k, 0, 0))
    io_spec = pl.BlockSpec((s, rows, LANES), lambda k: (0, 0, k))
    return pl.pallas_call(
        functools.partial(_s5_prompt_kernel, bsz=bsz),
        grid=(g // gp,),
        in_specs=[io_spec,
                  gspec(s * n, s * n), gspec(s * n, p2), gspec(p2, s * n),
                  gspec(1, 1), gspec(1, p2), gspec(1, p2),
                  pl.BlockSpec((1, LANES), lambda k: (0, k))],
        out_specs=[io_spec, gspec(bsz, p2)],
        out_shape=[jax.ShapeDtypeStruct((s, rows, g * n), F32),
                   jax.ShapeDtypeStruct((g, bsz, p2), F32)],
        scratch_shapes=[pltpu.VMEM((gp, rows, s * n), BF16),
                        pltpu.VMEM((gp, rows, p2), F32),
                        pltpu.VMEM((gp, rows, p2), F32),
                        pltpu.VMEM((gp, rows, s * n), F32)],
        compiler_params=_params(1),
        name="s5_prompt",
    )(_steps_major(u, s), t_mat, p_mat, q_mat, log_dt[:, None, None], dup(a_re), dup(a_im),
      d_skip.reshape(1, g * n))


def _s5_sample_kernel(u_ref, xr_ref, xi_ref, bbt_ref, w0_ref, ldt_ref, arr_ref, air_ref, d_ref,
                      y_ref, nr_ref, ni_ref):
    p = S5_STATE
    dt = jnp.exp(ldt_ref[0])
    mag = jnp.exp(arr_ref[0][:, :p] * dt)
    ang = air_ref[0][:, :p] * dt
    abr, abi = mag * jnp.cos(ang), mag * jnp.sin(ang)
    u = u_ref[0].astype(F32)
    z = _dot(u.astype(BF16), bbt_ref[0].astype(BF16))
    xr, xi = xr_ref[0], xi_ref[0]
    new_r = abr * xr - abi * xi + z[:, :p]
    new_i = abr * xi + abi * xr + z[:, p:]
    nr_ref[0] = new_r
    ni_ref[0] = new_i
    c = w0_ref[0][:, :S5_GROUP].astype(BF16)
    y = _dot(new_r.astype(BF16), c[:p]) + _dot(new_i.astype(BF16), c[p:]) + d_ref[0] * u
    y_ref[0] = jax.nn.gelu(y)


def s5_sample(u, x_re, x_im, bbt, w0, log_dt, a_re, a_im, d_skip):
    g, n, p = S5_GROUPS, S5_GROUP, S5_STATE
    bsz = u.shape[0]
    dup = lambda a: jnp.concatenate([a, a], axis=-1)[:, None, :]
    gspec = lambda *shape: pl.BlockSpec((1,) + shape, lambda i: (i, 0, 0))
    y, new_r, new_i = pl.pallas_call(
        _s5_sample_kernel,
        grid=(g,),
        in_specs=[gspec(bsz, n), gspec(bsz, p), gspec(bsz, p), gspec(n, 2 * p),
                  gspec(2 * p, S5_CHUNK * n), gspec(1, 1), gspec(1, 2 * p), gspec(1, 2 * p),
                  gspec(1, n)],
        out_specs=[gspec(bsz, n), gspec(bsz, p), gspec(bsz, p)],
        out_shape=[jax.ShapeDtypeStruct((g, bsz, n), F32),
                   jax.ShapeDtypeStruct((g, bsz, p), F32),
                   jax.ShapeDtypeStruct((g, bsz, p), F32)],
        compiler_params=_params(1),
        name="s5_sample",
    )(u.reshape(bsz, g, n).transpose(1, 0, 2), x_re.transpose(1, 0, 2), x_im.transpose(1, 0, 2),
      bbt, w0, log_dt[:, None, None], dup(a_re), dup(a_im), d_skip.reshape(g, 1, n))
    return (y.transpose(1, 0, 2).reshape(bsz, g * n), new_r.transpose(1, 0, 2),
            new_i.transpose(1, 0, 2))


FACTORED_DECAY_LIMIT = -60.0


def _gla_kernel(q_ref, k_ref, v_ref, gr_ref, g_ref, nrm_ref, o_ref, st_ref, s_ref, b_ref,
                qb_ref, kb_ref, kend_ref, sc_ref, acc_ref):
    t = pl.program_id(0)

    @pl.when(t == 0)
    def _():
        s_ref[...] = jnp.zeros(s_ref.shape, F32)

    bsz, c = q_ref.shape[0], q_ref.shape[1]
    row = lax.broadcasted_iota(jnp.int32, (c, c), 0)
    col = lax.broadcasted_iota(jnp.int32, (c, c), 1)
    causal = col <= row
    ones_tri = causal.astype(F32)
    step = lax.broadcasted_iota(jnp.int32, (c, GLA_DK), 0)

    lowest = None
    for i in range(bsz):
        b_ref[i] = jnp.dot(ones_tri, g_ref[i], precision=HIGHEST, preferred_element_type=F32)
        end_min = jnp.min(b_ref[i, c - 1:c, :])
        lowest = end_min if lowest is None else jnp.minimum(lowest, end_min)
    factorable = lowest >= FACTORED_DECAY_LIMIT

    pairs = [(i, h) for i in range(bsz) for h in range(GLA_HEADS)]
    head_k = lambda h: slice(h * GLA_DK, (h + 1) * GLA_DK)
    head_v = lambda h: slice(h * GLA_DV, (h + 1) * GLA_DV)

    def finish(i, h, o):
        o = o * lax.rsqrt(jnp.mean(o * o, axis=-1, keepdims=True) + EPS) * nrm_ref[...]
        o_ref[i, :, head_v(h)] = (o * _silu(gr_ref[i, :, head_v(h)].astype(F32))).astype(o_ref.dtype)

    def all_chunks_factored():
        for n, (i, h) in enumerate(pairs):
            b = b_ref[i, :, head_k(h)]
            k = k_ref[i, :, head_k(h)].astype(F32)
            qb_ref[n] = (q_ref[i, :, head_k(h)].astype(F32) * (GLA_DK ** -0.5) * jnp.exp(b)).astype(BF16)
            kb_ref[n] = (k * jnp.exp(-b)).astype(BF16)
            kend_ref[n] = (k * jnp.exp(b[c - 1:c, :] - b)).astype(BF16)
        for n, (i, h) in enumerate(pairs):
            scores = lax.dot_general(qb_ref[n], kb_ref[n], NT, preferred_element_type=F32)
            sc_ref[n] = jnp.where(causal, scores, 0.0).astype(BF16)
        for n, (i, h) in enumerate(pairs):
            v = v_ref[i, :, head_v(h)].astype(BF16)
            s_t = s_ref[i, h]
            acc_ref[n] = _dot(sc_ref[n], v) + lax.dot_general(qb_ref[n], s_t.astype(BF16), NT,
                                                              preferred_element_type=F32)
            s_ref[i, h] = (jnp.exp(b_ref[i, c - 1:c, head_k(h)]) * s_t
                           + lax.dot_general(v, kend_ref[n], TN, preferred_element_type=F32))
        for n, (i, h) in enumerate(pairs):
            finish(i, h, acc_ref[n])

    def intra_stepwise(i, ks, vs, b, q_in, v):
        q = q_ref[i, :, ks].astype(F32) * (GLA_DK ** -0.5)

        def add_sources(tile, acc):
            first = pl.multiple_of(tile * ROW_TILE, ROW_TILE)
            rows = pl.ds(first, ROW_TILE)
            b_s = b_ref[i, rows, ks]
            k_s, v_s = k_ref[i, rows, ks].astype(F32), v_ref[i, rows, vs].astype(F32)
            for r in range(ROW_TILE):
                decay = jnp.exp(jnp.where(step >= first + r, b - b_s[r:r + 1], -jnp.inf))
                score = jnp.sum(q * k_s[r:r + 1] * decay, axis=1, keepdims=True)
                acc = acc + score * v_s[r:r + 1]
            return acc

        return lax.fori_loop(0, c // ROW_TILE, add_sources, jnp.zeros((c, GLA_DV), F32))

    def all_chunks_stepwise():
        for i, h in pairs:
            ks, vs = head_k(h), head_v(h)
            b = b_ref[i, :, ks]
            b_end = b[c - 1:c, :]
            q_in = (q_ref[i, :, ks].astype(F32) * (GLA_DK ** -0.5) * jnp.exp(b)).astype(BF16)
            k_end = (k_ref[i, :, ks].astype(F32) * jnp.exp(b_end - b)).astype(BF16)
            v = v_ref[i, :, vs].astype(BF16)
            s_t = s_ref[i, h]
            o = (intra_stepwise(i, ks, vs, b, q_in, v)
                 + lax.dot_general(q_in, s_t.astype(BF16), NT, preferred_element_type=F32))
            s_ref[i, h] = jnp.exp(b_end) * s_t + lax.dot_general(v, k_end, TN,
                                                                 preferred_element_type=F32)
            finish(i, h, o)

    pl.when(factorable)(all_chunks_factored)
    pl.when(jnp.logical_not(factorable))(all_chunks_stepwise)

    @pl.when(t == pl.num_programs(0) - 1)
    def _():
        st_ref[...] = s_ref[...]


def gla_prompt(proj, g, gla_norm, bsz, length):
    c = CHUNK
    nc = length // c
    hk, hv = GLA_HEADS * GLA_DK, GLA_HEADS * GLA_DV
    rows = lambda width, off: pl.BlockSpec((bsz, c, width), lambda t: (0, t, off // width))
    st_shape = (bsz, GLA_HEADS, GLA_DV, GLA_DK)
    proj3 = proj.reshape(bsz, length, proj.shape[1])
    o, st = pl.pallas_call(
        _gla_kernel,
        grid=(nc,),
        in_specs=[rows(hk, OFF_GQ), rows(hk, OFF_GK), rows(hv, OFF_GV), rows(hv, OFF_GR),
                  rows(hk, 0), pl.BlockSpec((1, GLA_DV), lambda t: (0, 0))],
        out_specs=[rows(hv, 0), pl.BlockSpec(st_shape, lambda t: (0, 0, 0, 0))],
        out_shape=[jax.ShapeDtypeStruct((bsz, length, hv), BF16),
                   jax.ShapeDtypeStruct(st_shape, F32)],
        scratch_shapes=[pltpu.VMEM(st_shape, F32),
                        pltpu.VMEM((bsz, c, hk), F32)]
        + [pltpu.VMEM((bsz * GLA_HEADS, c, GLA_DK), BF16)] * 3
        + [pltpu.VMEM((bsz * GLA_HEADS, c, c), BF16),
           pltpu.VMEM((bsz * GLA_HEADS, c, GLA_DV), F32)],
        compiler_params=_params(1),
        name="gla_prompt",
    )(proj3, proj3, proj3, proj3, g.reshape(bsz, length, hk), gla_norm.reshape(1, GLA_DV))
    return o.reshape(bsz * length, hv), st.swapaxes(2, 3)


def _rotate_pairs(x, cos2, sin2):
    lane = lax.broadcasted_iota(jnp.int32, x.shape, 1)
    partner = jnp.where(lane % 2 == 0, pltpu.roll(x, x.shape[1] - 1, axis=1), pltpu.roll(x, 1, axis=1))
    return x * cos2 + partner * sin2


def _ret_kernel(q_ref, k_ref, v_ref, rg_ref, cos_ref, sin_ref, dm_ref, qd_ref, kd_ref, gc_ref,
                o_ref, st_ref, s_ref, qb_ref, kb_ref, qdec_ref, kdec_ref, sc_ref, acc_ref):
    t = pl.program_id(0)

    @pl.when(t == 0)
    def _():
        s_ref[...] = jnp.zeros(s_ref.shape, F32)

    cos2, sin2 = cos_ref[...], sin_ref[...]
    pairs = [(i, h) for i in range(q_ref.shape[0]) for h in range(RET_HEADS)]
    head_k = lambda h: slice(h * RET_DK, (h + 1) * RET_DK)
    head_v = lambda h: slice(h * RET_DV, (h + 1) * RET_DV)
    for n, (i, h) in enumerate(pairs):
        q = _rotate_pairs(q_ref[i, :, head_k(h)].astype(F32), cos2, sin2)
        k = _rotate_pairs(k_ref[i, :, head_k(h)].astype(F32), cos2, sin2) * (RET_DK ** -0.5)
        qb_ref[n] = q.astype(BF16)
        kb_ref[n] = k.astype(BF16)
        qdec_ref[n] = (q * qd_ref[h]).astype(BF16)
        kdec_ref[n] = (k * kd_ref[h]).astype(BF16)
    for n, (i, h) in enumerate(pairs):
        scores = lax.dot_general(qb_ref[n], kb_ref[n], NT, preferred_element_type=F32) * dm_ref[h]
        sc_ref[n] = scores.astype(BF16)
    for n, (i, h) in enumerate(pairs):
        v = v_ref[i, :, head_v(h)].astype(BF16)
        s_t = s_ref[i, h]
        acc_ref[n] = _dot(sc_ref[n], v) + lax.dot_general(qdec_ref[n], s_t.astype(BF16), NT,
                                                          preferred_element_type=F32)
        s_ref[i, h] = gc_ref[h] * s_t + lax.dot_general(v, kdec_ref[n], TN, preferred_element_type=F32)
    for n, (i, h) in enumerate(pairs):
        o = acc_ref[n]
        oc = o - jnp.mean(o, axis=-1, keepdims=True)
        o = oc * lax.rsqrt(jnp.mean(oc * oc, axis=-1, keepdims=True) + EPS)
        o_ref[i, :, head_v(h)] = (o * _silu(rg_ref[i, :, head_v(h)].astype(F32))).astype(o_ref.dtype)

    @pl.when(t == pl.num_programs(0) - 1)
    def _():
        st_ref[...] = s_ref[...]


def _rotary_tables(pos):
    half = RET_DK // 2
    inv = 1.0 / (ROPE_BASE ** jnp.linspace(0.0, 1.0, half, dtype=F32))
    ang = pos[:, None] * inv[None, :]
    cos2 = jnp.repeat(jnp.cos(ang), 2, axis=-1)
    sin2 = jnp.stack([-jnp.sin(ang), jnp.sin(ang)], axis=-1).reshape(pos.shape[0], RET_DK)
    return cos2, sin2


def _log_gamma():
    return jnp.log1p(-jnp.power(2.0, -5.0 - jnp.arange(RET_HEADS, dtype=F32)))


def ret_prompt(proj, bsz, length):
    c = CHUNK
    nc = length // c
    hk, hv = RET_HEADS * RET_DK, RET_HEADS * RET_DV
    cos2, sin2 = _rotary_tables(jnp.arange(length, dtype=F32))
    lg = _log_gamma()[:, None, None]
    steps = jnp.arange(c, dtype=F32)
    lag = steps[:, None] - steps[None, :]
    dmat = jnp.where(lag >= 0, jnp.exp(lg * jnp.maximum(lag, 0.0)), 0.0)
    wide = lambda a: jnp.broadcast_to(a, (RET_HEADS, a.shape[1], LANES))
    qdec = wide(jnp.exp(lg * (steps + 1.0)[None, :, None]))
    kdec = wide(jnp.exp(lg * (c - 1.0 - steps)[None, :, None]))
    gchunk = wide(jnp.exp(lg * float(c)))
    rows = lambda width, off: pl.BlockSpec((bsz, c, width), lambda t: (0, t, off // width))
    table = pl.BlockSpec((c, RET_DK), lambda t: (t, 0))
    const = lambda a: pl.BlockSpec(a.shape, lambda t: (0, 0, 0))
    st_shape = (bsz, RET_HEADS, RET_DV, RET_DK)
    proj3 = proj.reshape(bsz, length, proj.shape[1])
    o, st = pl.pallas_call(
        _ret_kernel,
        grid=(nc,),
        in_specs=[rows(hk, OFF_RQ), rows(hk, OFF_RK), rows(hv, OFF_RV), rows(hv, OFF_RG),
                  table, table, const(dmat), const(qdec), const(kdec), const(gchunk)],
        out_specs=[rows(hv, 0), pl.BlockSpec(st_shape, lambda t: (0, 0, 0, 0))],
        out_shape=[jax.ShapeDtypeStruct((bsz, length, hv), BF16),
                   jax.ShapeDtypeStruct(st_shape, F32)],
        scratch_shapes=[pltpu.VMEM(st_shape, F32)]
        + [pltpu.VMEM((bsz * RET_HEADS, c, RET_DK), BF16)] * 4
        + [pltpu.VMEM((bsz * RET_HEADS, c, c), BF16),
           pltpu.VMEM((bsz * RET_HEADS, c, RET_DV), F32)],
        compiler_params=_params(1),
        name="ret_prompt",
    )(proj3, proj3, proj3, proj3, cos2, sin2, dmat, qdec, kdec, gchunk)
    return o.reshape(bsz * length, hv), st.swapaxes(2, 3)


def _one_column(a_t, i):
    col = lax.broadcasted_iota(jnp.int32, a_t.shape, 1)
    return jnp.where(col == i, a_t, 0.0).astype(BF16)


def _gla_sample_kernel(s_ref, q_ref, kt_ref, gt_ref, v_ref, gr_ref, nrm_ref, *rest):
    so_ref, o_ref, acc_ref = rest[-3:]
    ones = jnp.ones((SAMPLE_BLOCK, GLA_DV), BF16)
    head_k = lambda h: slice(h * GLA_DK, (h + 1) * GLA_DK)
    head_v = lambda h: slice(h * GLA_DV, (h + 1) * GLA_DV)
    for h in range(GLA_HEADS):
        v = v_ref[:, head_v(h)].astype(BF16)
        k_t = kt_ref[0, h]
        decay_t = jnp.exp(gt_ref[0, h])
        decay_hi = decay_t.astype(BF16).astype(F32)
        decay_lo = decay_t - decay_hi
        for i in range(SAMPLE_BLOCK):
            decay = _dot(_one_column(decay_hi, i), ones) + _dot(_one_column(decay_lo, i), ones)
            so_ref[i, h] = decay * s_ref[i, h] + _dot(_one_column(k_t, i), v)
    for h in range(GLA_HEADS):
        q = (q_ref[:, head_k(h)].astype(F32) * (GLA_DK ** -0.5)).astype(BF16)
        for i in range(SAMPLE_BLOCK):
            acc_ref[i:i + 1, head_v(h)] = _dot(q, so_ref[i, h].astype(BF16))[i:i + 1]
    for h in range(GLA_HEADS):
        o = acc_ref[:, head_v(h)]
        o = o * lax.rsqrt(jnp.mean(o * o, axis=-1, keepdims=True) + EPS) * nrm_ref[...]
        o_ref[:, head_v(h)] = (o * _silu(gr_ref[:, head_v(h)].astype(F32))).astype(o_ref.dtype)


def _columns(a, heads, dk):
    bsz = a.shape[0]
    return a.reshape(bsz // SAMPLE_BLOCK, SAMPLE_BLOCK, heads, dk).transpose(0, 2, 3, 1)


def _stacked_state_call(kernel_fn, name, state, layer, prev, heads, dk, dv, in_specs, operands,
                        o_spec, o_shape):
    bb = SAMPLE_BLOCK
    st_spec = pl.BlockSpec((None, bb, heads, dk, dv), lambda i: (layer, i, 0, 0, 0))
    aliases = {}
    if prev is not None:
        in_specs = in_specs + [pl.BlockSpec(memory_space=pl.ANY)]
        operands = operands + (prev,)
        aliases = {len(in_specs): 0}
    return pl.pallas_call(
        kernel_fn,
        grid=(state.shape[1] // bb,),
        in_specs=[st_spec] + in_specs,
        out_specs=[st_spec, o_spec],
        out_shape=[jax.ShapeDtypeStruct(state.shape, F32), o_shape],
        scratch_shapes=[pltpu.VMEM((bb, heads * dv), F32)],
        input_output_aliases=aliases,
        compiler_params=_params(1),
        name=name,
    )(state, *operands)


def gla_sample(proj, g, state, gla_norm, layer, prev):
    bsz = proj.shape[0]
    bb = SAMPLE_BLOCK
    hk, hv = GLA_HEADS * GLA_DK, GLA_HEADS * GLA_DV
    col_spec = pl.BlockSpec((1, GLA_HEADS, GLA_DK, bb), lambda i: (i, 0, 0, 0))
    rows = lambda off: pl.BlockSpec((bb, hv), lambda i: (i, off // hv))
    return _stacked_state_call(
        _gla_sample_kernel, "gla_sample", state, layer, prev, GLA_HEADS, GLA_DK, GLA_DV,
        [pl.BlockSpec((bb, hk), lambda i: (i, OFF_GQ // hk)), col_spec, col_spec, rows(OFF_GV),
         rows(OFF_GR), pl.BlockSpec((None, 1, GLA_DV), lambda i: (layer, 0, 0))],
        (proj, _columns(proj[:, OFF_GK:OFF_GK + hk], GLA_HEADS, GLA_DK), _columns(g, GLA_HEADS, GLA_DK),
         proj, proj, gla_norm.reshape(gla_norm.shape[0], 1, GLA_DV)),
        rows(0), jax.ShapeDtypeStruct((bsz, hv), BF16))


def _rot_kernel(q_ref, k_ref, cos_ref, sin_ref, qo_ref, ko_ref):
    cos2, sin2 = cos_ref[...], sin_ref[...]
    for h in range(RET_HEADS):
        ks = slice(h * RET_DK, (h + 1) * RET_DK)
        qo_ref[:, ks] = _rotate_pairs(q_ref[:, ks].astype(F32), cos2, sin2)
        ko_ref[:, ks] = _rotate_pairs(k_ref[:, ks].astype(F32), cos2, sin2) * (RET_DK ** -0.5)


def _ret_sample_kernel(s_ref, q_ref, kt_ref, gam_ref, v_ref, rg_ref, *rest):
    so_ref, o_ref, acc_ref = rest[-3:]
    head_k = lambda h: slice(h * RET_DK, (h + 1) * RET_DK)
    head_v = lambda h: slice(h * RET_DV, (h + 1) * RET_DV)
    for h in range(RET_HEADS):
        v = v_ref[:, head_v(h)].astype(BF16)
        k_t = kt_ref[0, h]
        for i in range(SAMPLE_BLOCK):
            so_ref[i, h] = gam_ref[h] * s_ref[i, h] + _dot(_one_column(k_t, i), v)
    for h in range(RET_HEADS):
        q = q_ref[:, head_k(h)].astype(BF16)
        for i in range(SAMPLE_BLOCK):
            acc_ref[i:i + 1, head_v(h)] = _dot(q, so_ref[i, h].astype(BF16))[i:i + 1]
    for h in range(RET_HEADS):
        o = acc_ref[:, head_v(h)]
        oc = o - jnp.mean(o, axis=-1, keepdims=True)
        o = oc * lax.rsqrt(jnp.mean(oc * oc, axis=-1, keepdims=True) + EPS)
        o_ref[:, head_v(h)] = (o * _silu(rg_ref[:, head_v(h)].astype(F32))).astype(o_ref.dtype)


def ret_sample(proj, state, layer, prev):
    bsz = proj.shape[0]
    bb = SAMPLE_BLOCK
    hk, hv = RET_HEADS * RET_DK, RET_HEADS * RET_DV
    cos2, sin2 = _rotary_tables(PAST_LEN + jnp.arange(1, dtype=F32))
    full = pl.BlockSpec((bsz, hk), lambda i: (0, 0))
    q_rot, k_rot = pl.pallas_call(
        _rot_kernel,
        grid=(1,),
        in_specs=[pl.BlockSpec((bsz, hk), lambda i: (0, OFF_RQ // hk)),
                  pl.BlockSpec((bsz, hk), lambda i: (0, OFF_RK // hk)),
                  pl.BlockSpec((1, RET_DK), lambda i: (0, 0)), pl.BlockSpec((1, RET_DK), lambda i: (0, 0))],
        out_specs=[full, full],
        out_shape=[jax.ShapeDtypeStruct((bsz, hk), F32)] * 2,
        compiler_params=_params(1),
        name="ret_rotary",
    )(proj, proj, cos2, sin2)
    gamma = jnp.broadcast_to(jnp.exp(_log_gamma())[:, None, None], (RET_HEADS, 1, LANES))
    col_spec = pl.BlockSpec((1, RET_HEADS, RET_DK, bb), lambda i: (i, 0, 0, 0))
    rows = lambda off: pl.BlockSpec((bb, hv), lambda i: (i, off // hv))
    return _stacked_state_call(
        _ret_sample_kernel, "ret_sample", state, layer, prev, RET_HEADS, RET_DK, RET_DV,
        [rows(0), col_spec, pl.BlockSpec(gamma.shape, lambda i: (0, 0, 0)), rows(OFF_RV), rows(OFF_RG)],
        (q_rot, _columns(k_rot, RET_HEADS, RET_DK), gamma, proj, proj),
        rows(0), jax.ShapeDtypeStruct((bsz, hv), BF16))


def _layer(xp, xs, layer, prompt_shape, states, w, s5_mats):
    t_mat, p_mat, q_mat, bbt, w0 = s5_mats
    bsz, length = prompt_shape
    mp = xp.shape[0]
    tm, tm_wide, tn = min(mp, 512), min(mp, 1024), 512
    s5_par = (w["s5_log_dt"][layer], w["s5_a_re"][layer], w["s5_a_im"][layer], w["s5_d"][layer])
    st_s5_re, st_s5_im, st_gla, st_ret, prev_gla, prev_ret = states

    gate_w = (w["w_in_rows"], w["gla_w_gate"], w["gla_b_gate"], layer)
    hp, g_p = norm_and_gate(xp, w["norm_mix"][layer], *gate_w)
    hs, g_s = norm_and_gate(xs, w["norm_mix"][layer], *gate_w)
    proj_p, proj_s = in_proj(hp, hs, w["w_in_rows"], layer, tm_wide, 2 * tn)

    y5_p, x_fin = s5_prompt(proj_p[:, OFF_U:OFF_U + S5_WIDTH], bsz, length, t_mat, p_mat, q_mat, *s5_par)
    y5_p = y5_p.reshape(mp, S5_WIDTH)
    x_fin = x_fin.transpose(1, 0, 2)
    o_gla_p, new_gla_p = gla_prompt(proj_p, g_p, w["gla_norm"][layer], bsz, length)
    o_ret_p, new_ret_p = ret_prompt(proj_p, bsz, length)

    y5_s, new_s5_re_s, new_s5_im_s = s5_sample(proj_s[:, OFF_U:OFF_U + S5_WIDTH], st_s5_re[layer],
                                               st_s5_im[layer], bbt, w0, *s5_par)
    new_gla_s, o_gla_s = gla_sample(proj_s, g_s, st_gla, w["gla_norm"], layer, prev_gla)
    new_ret_s, o_ret_s = ret_sample(proj_s, st_ret, layer, prev_ret)

    a5_p, a5_s = s5_glu([y5_p, y5_s], w["s5_w_glu"], w["s5_b_glu"], layer, tm, 2 * tn)
    a5_p = a5_p.reshape(S5_CHUNK, mp // S5_CHUNK, S5_WIDTH).swapaxes(0, 1).reshape(mp, S5_WIDTH)
    merged = merge_branches([(a5_p, o_gla_p, o_ret_p), (a5_s, o_gla_s, o_ret_s)], [proj_p, proj_s],
                            w["s5_w_out"], w["gla_w_out"], w["ret_w_out"], layer, tm, 2 * tn)
    xp, xs = residual_matmul(merged, w["w_mix_out"], [xp, xs], layer, tm_wide, 2 * tn)
    h2 = [rmsnorm(xp, w["norm_ffn"][layer], BF16), rmsnorm(xs, w["norm_ffn"][layer], BF16)]
    ff = ffn_up(h2, w["w_ffn_gate"], w["w_ffn_up"], layer, tm_wide, tn)
    xp, xs = residual_matmul(ff, w["w_ffn_down"], [xp, xs], layer, tm_wide, tn)
    new_p = (x_fin[..., :S5_STATE], x_fin[..., S5_STATE:], new_gla_p, new_ret_p)
    return xp, xs, new_p, (new_s5_re_s, new_s5_im_s, new_gla_s, new_ret_s)


def kernel(x_prompt, x_sample, state_s5_re, state_s5_im, state_gla, state_ret, norm_mix, w_in, s5_a_re, s5_a_im, s5_log_dt, s5_b_re, s5_b_im, s5_c_re, s5_c_im, s5_d, s5_w_glu, s5_b_glu, s5_w_out, gla_w_gate, gla_b_gate, gla_norm, gla_w_out, ret_w_out, w_mix_out, norm_ffn, w_ffn_gate, w_ffn_up, w_ffn_down, norm_final):
    bp, lp, d = x_prompt.shape
    bs = x_sample.shape[0]
    depth = w_in.shape[0]
    hp = x_prompt.reshape(bp * lp, d)
    hs = x_sample.reshape(bs, d)
    w = dict(norm_mix=norm_mix, s5_log_dt=s5_log_dt, s5_a_re=s5_a_re, s5_a_im=s5_a_im, s5_d=s5_d,
             s5_w_glu=s5_w_glu, s5_b_glu=s5_b_glu, s5_w_out=s5_w_out, gla_w_gate=gla_w_gate,
             gla_b_gate=gla_b_gate, gla_norm=gla_norm, gla_w_out=gla_w_out, ret_w_out=ret_w_out,
             w_mix_out=w_mix_out, norm_ffn=norm_ffn, w_ffn_gate=w_ffn_gate, w_ffn_up=w_ffn_up,
             w_ffn_down=w_ffn_down,
             w_in_rows=jnp.swapaxes(w_in, 1, 2).reshape(depth, w_in.shape[2] // GLA_RANK, GLA_RANK, d))
    outs_p, outs_s = [], []
    new_gla_s = new_ret_s = None
    for l in range(depth):
        s5_mats = s5_prep(s5_a_re[l], s5_a_im[l], s5_log_dt[l], s5_b_re[l], s5_b_im[l], s5_c_re[l],
                          s5_c_im[l])
        hp, hs, new_p, (s5r, s5i, new_gla_s, new_ret_s) = _layer(
            hp, hs, l, (bp, lp), (state_s5_re, state_s5_im, state_gla, state_ret, new_gla_s, new_ret_s),
            w, s5_mats)
        outs_p.append(new_p)
        outs_s.append((s5r, s5i))
    y_prompt = rmsnorm(hp, norm_final, F32).reshape(bp, lp, d)
    y_sample = rmsnorm(hs, norm_final, F32).reshape(bs, 1, d)
    stack = lambda outs, i: jnp.stack([o[i] for o in outs])
    return (y_prompt, y_sample,
            stack(outs_p, 0), stack(outs_p, 1), stack(outs_p, 2), stack(outs_p, 3),
            stack(outs_s, 0), stack(outs_s, 1), new_gla_s, new_ret_s)
```

```python
import functools
import math

import jax
import jax.numpy as jnp
import numpy as np
from jax import lax
from jax.experimental import pallas as pl
from jax.experimental.pallas import tpu as pltpu

F32 = jnp.float32
BF16 = jnp.bfloat16
HIGHEST = lax.Precision.HIGHEST

D_MODEL = 2048
S5_WIDTH = D_MODEL // 2
S5_GROUP = 16
S5_GROUPS = S5_WIDTH // S5_GROUP
S5_STATE = 64
S5_CHUNK = 16
GLA_HEADS = 4
GLA_DK = D_MODEL // 16
GLA_DV = D_MODEL // 8
GLA_RANK = 16
GLA_GATE_NORM = 16.0
RET_HEADS = 8
RET_DK = D_MODEL // 16
RET_DV = D_MODEL // 16
ROPE_BASE = 10000.0
PAST_LEN = 16384
N_BRANCH = 3
D_FF = -(-8 * D_MODEL // (3 * 256)) * 256
EPS = 1e-6
CHUNK = 128
LANES = 128
ROW_TILE = 16
SAMPLE_BLOCK = 16

OFF_U = 0
OFF_GQ = OFF_U + S5_WIDTH
OFF_GK = OFF_GQ + GLA_HEADS * GLA_DK
OFF_GV = OFF_GK + GLA_HEADS * GLA_DK
OFF_GR = OFF_GV + GLA_HEADS * GLA_DV
OFF_RQ = OFF_GR + GLA_HEADS * GLA_DV
OFF_RK = OFF_RQ + RET_HEADS * RET_DK
OFF_RV = OFF_RK + RET_HEADS * RET_DK
OFF_RG = OFF_RV + RET_HEADS * RET_DV
OFF_MG = OFF_RG + RET_HEADS * RET_DV
W_MAIN = OFF_MG + N_BRANCH * D_MODEL
GLR_COL = OFF_RQ

VMEM_LIMIT = 56 * 1024 * 1024

NT = (((1,), (1,)), ((), ()))
TN = (((0,), (0,)), ((), ()))


def _params(n_grid):
    return pltpu.CompilerParams(dimension_semantics=("arbitrary",) * n_grid,
                                vmem_limit_bytes=VMEM_LIMIT)


def _silu(x):
    return x * jax.nn.sigmoid(x)


def _rmsnorm_kernel(x_ref, g_ref, o_ref):
    x = x_ref[...]
    y = x * lax.rsqrt(jnp.mean(x * x, axis=-1, keepdims=True) + EPS)
    o_ref[...] = (y * g_ref[...]).astype(o_ref.dtype)


def rmsnorm(x, gain, out_dtype):
    m, d = x.shape
    tm = min(m, 512)
    return pl.pallas_call(
        _rmsnorm_kernel,
        grid=(m // tm,),
        in_specs=[pl.BlockSpec((tm, d), lambda i: (i, 0)),
                  pl.BlockSpec((1, d), lambda i: (0, 0))],
        out_specs=pl.BlockSpec((tm, d), lambda i: (i, 0)),
        out_shape=jax.ShapeDtypeStruct((m, d), out_dtype),
        compiler_params=_params(1),
        name="rmsnorm",
    )(x, gain.reshape(1, d))


def _dot(a, b):
    return jnp.dot(a, b, preferred_element_type=F32)


def _in_proj_kernel(a_ref, as_ref, w_ref, o_ref, os_ref, wb_ref):
    i = pl.program_id(1)

    @pl.when(i == 0)
    def _():
        wb_ref[...] = w_ref[...].reshape(wb_ref.shape).astype(BF16)

    def project(src_ref, dst_ref):
        dst_ref[...] = lax.dot_general(src_ref[...], wb_ref[...], NT,
                                       preferred_element_type=F32).astype(dst_ref.dtype)

    pl.when(i == 0)(lambda: project(as_ref, os_ref))
    pl.when(i > 0)(lambda: project(a_ref, o_ref))


def in_proj(h, h_s, w_rows, layer, tm, tn):
    m, k = h.shape
    ms = h_s.shape[0]
    nb = m // tm
    rb = tn // GLA_RANK
    skip_from = GLR_COL // tn

    def w_index(j, i):
        return (layer, rb * j + jnp.where(j >= skip_from, 1, 0), 0, 0)

    return pl.pallas_call(
        _in_proj_kernel,
        grid=(W_MAIN // tn, nb + 1),
        in_specs=[pl.BlockSpec((tm, k), lambda j, i: (jnp.maximum(i - 1, 0), 0)),
                  pl.BlockSpec((ms, k), lambda j, i: (0, 0)),
                  pl.BlockSpec((pl.Element(1), pl.Element(rb), pl.Element(GLA_RANK), pl.Element(k)),
                               w_index)],
        out_specs=[pl.BlockSpec((tm, tn), lambda j, i: (jnp.maximum(i - 1, 0), j)),
                   pl.BlockSpec((ms, tn), lambda j, i: (0, j))],
        out_shape=[jax.ShapeDtypeStruct((m, W_MAIN), BF16), jax.ShapeDtypeStruct((ms, W_MAIN), BF16)],
        scratch_shapes=[pltpu.VMEM((tn, k), BF16)],
        compiler_params=_params(2),
        name="in_proj",
    )(h, h_s, w_rows)


def _fused_matmul(name, body, streams, weights, biases, layer, n, tm, tn, out_dtype):
    nb = streams[0][0][0].shape[0] // tm
    extra = len(streams) - 1
    specs, operands, counts, out_specs, out_shapes = [], [], [], [], []
    for s, (row_ops, tile_ops) in enumerate(streams):
        m_s = row_ops[0].shape[0]
        rows = tm if s == 0 else m_s
        row_block = (lambda i: jnp.maximum(i - extra, 0)) if s == 0 else (lambda i: 0)
        for a in row_ops:
            specs.append(pl.BlockSpec((rows, a.shape[1]), lambda j, i, r=row_block: (r(i), 0)))
            operands.append(a)
        for a, first in tile_ops:
            specs.append(pl.BlockSpec((rows, tn), lambda j, i, r=row_block, c=first // tn: (r(i), c + j)))
            operands.append(a)
        counts.append((len(row_ops), len(tile_ops)))
        out_specs.append(pl.BlockSpec((rows, tn), lambda j, i, r=row_block: (r(i), j)))
        out_shapes.append(jax.ShapeDtypeStruct((m_s, n), out_dtype))
    weight_bytes = sum(wt.shape[1] * tn * 4 for wt in weights)
    weight_buffers = 2 if 2 * weight_bytes <= VMEM_LIMIT // 3 else 1
    for wt in weights:
        specs.append(pl.BlockSpec((None, wt.shape[1], tn), lambda j, i: (layer, 0, j),
                                  pipeline_mode=pl.Buffered(weight_buffers)))
        operands.append(wt)
    for b in biases:
        specs.append(pl.BlockSpec((None, 1, tn), lambda j, i: (layer, 0, j)))
        operands.append(b)

    def kernel_fn(*refs):
        refs = list(refs)
        stream_refs = [(tuple(refs.pop(0) for _ in range(nr)), tuple(refs.pop(0) for _ in range(nt)))
                       for nr, nt in counts]
        w_refs = [refs.pop(0) for _ in weights]
        b_refs = [refs.pop(0) for _ in biases]
        o_refs = [refs.pop(0) for _ in streams]
        wb_refs = refs
        i = pl.program_id(1)

        @pl.when(i == 0)
        def _():
            for w_ref, wb_ref in zip(w_refs, wb_refs):
                wb_ref[...] = w_ref[...].astype(BF16)

        def run(s):
            row_refs, tile_refs = stream_refs[s]
            out = body(*[r[...] for r in row_refs], *[r[...] for r in wb_refs],
                       *[r[...] for r in tile_refs], *[r[...] for r in b_refs])
            o_refs[s][...] = out.astype(out_dtype)

        for s in range(1, len(streams)):
            pl.when(i == s - 1)(functools.partial(run, s))
        pl.when(i >= extra)(functools.partial(run, 0))

    return pl.pallas_call(
        kernel_fn,
        grid=(n // tn, nb + len(streams) - 1),
        in_specs=specs,
        out_specs=out_specs,
        out_shape=out_shapes,
        scratch_shapes=[pltpu.VMEM((wt.shape[1], tn), BF16) for wt in weights],
        compiler_params=_params(2),
        name=name,
    )(*operands)


def _norm_gate_kernel(x_ref, gain_ref, wl_ref, wg_ref, b_ref, h_ref, g_ref):
    x = x_ref[...]
    h = (x * lax.rsqrt(jnp.mean(x * x, axis=-1, keepdims=True) + EPS) * gain_ref[...]).astype(BF16)
    h_ref[...] = h
    pad = LANES - GLA_RANK
    wl = jnp.concatenate([wl_ref[0], jnp.zeros((pad, wl_ref.shape[2]), F32)], axis=0).astype(BF16)
    wg = jnp.concatenate([wg_ref[...], jnp.zeros((pad, wg_ref.shape[1]), F32)], axis=0).astype(BF16)
    glr = lax.dot_general(h, wl, NT, preferred_element_type=F32)
    logit = _dot(glr.astype(BF16), wg) + b_ref[...]
    log_sig = jnp.minimum(logit, 0.0) - jnp.log1p(jnp.exp(-jnp.abs(logit)))
    g_ref[...] = log_sig / GLA_GATE_NORM


def norm_and_gate(x, gain, w_rows, w_gate, b_gate, layer):
    m, k = x.shape
    n = w_gate.shape[2]
    tm = min(m, 512)
    return pl.pallas_call(
        _norm_gate_kernel,
        grid=(m // tm,),
        in_specs=[pl.BlockSpec((tm, k), lambda i: (i, 0)),
                  pl.BlockSpec((1, k), lambda i: (0, 0)),
                  pl.BlockSpec((None, 1, GLA_RANK, k), lambda i: (layer, GLR_COL // GLA_RANK, 0, 0)),
                  pl.BlockSpec((None, GLA_RANK, n), lambda i: (layer, 0, 0)),
                  pl.BlockSpec((None, 1, n), lambda i: (layer, 0, 0))],
        out_specs=[pl.BlockSpec((tm, k), lambda i: (i, 0)), pl.BlockSpec((tm, n), lambda i: (i, 0))],
        out_shape=[jax.ShapeDtypeStruct((m, k), BF16), jax.ShapeDtypeStruct((m, n), F32)],
        compiler_params=_params(1),
        name="norm_and_gate",
    )(x, gain.reshape(1, k), w_rows, w_gate, b_gate.reshape(b_gate.shape[0], 1, n))


def s5_glu(y5_streams, w, b, layer, tm, tn):
    n = w.shape[2]
    body = lambda y, wb, y_tile, bias: y_tile * jax.nn.sigmoid(_dot(y.astype(BF16), wb) + bias)
    return _fused_matmul("s5_glu", body, [([y], [(y, 0)]) for y in y5_streams], [w],
                         [b.reshape(b.shape[0], 1, n)], layer, n, tm, tn, BF16)


def merge_branches(branch_streams, proj_streams, w5, wg, wr, layer, tm, tn):
    def body(a5, ag, ar, w5b, wgb, wrb, m5, mg, mr):
        gate = lambda t: jax.nn.sigmoid(t.astype(F32))
        return gate(m5) * _dot(a5, w5b) + gate(mg) * _dot(ag, wgb) + gate(mr) * _dot(ar, wrb)

    streams = [(list(branches), [(proj, OFF_MG + i * D_MODEL) for i in range(N_BRANCH)])
               for branches, proj in zip(branch_streams, proj_streams)]
    return _fused_matmul("merge_branches", body, streams, [w5, wg, wr], [], layer, w5.shape[2],
                         tm, tn, BF16)


def residual_matmul(a_streams, w, x_streams, layer, tm, tn):
    body = lambda a, wb, x: x + _dot(a, wb)
    return _fused_matmul("residual_matmul", body, [([a], [(x, 0)]) for a, x in zip(a_streams, x_streams)],
                         [w], [], layer, w.shape[2], tm, tn, F32)


def ffn_up(h_streams, wg, wu, layer, tm, tn):
    body = lambda h, wgb, wub: _silu(_dot(h, wgb)) * _dot(h, wub)
    return _fused_matmul("ffn_up", body, [([h], []) for h in h_streams], [wg, wu], [], layer,
                         wg.shape[2], tm, tn, BF16)


def _s5_prep_kernel(*refs):
    for g in range(refs[0].shape[0]):
        _s5_prep_group(*[r.at[g:g + 1] for r in refs])


def _s5_prep_group(ldt_ref, arc_ref, aic_ref, arr_ref, air_ref, b1_ref, crt_ref, cit_ref,
                   t_ref, p_ref, q_ref, bbt_ref, w0_ref):
    s = S5_CHUNK
    dt = jnp.exp(ldt_ref[0])
    ar_row, ai_row = arr_ref[0], air_ref[0]
    lam_r_row, lam_i_row = ar_row * dt, ai_row * dt
    lam_r_col, lam_i_col = arc_ref[0] * dt, aic_ref[0] * dt

    mag = jnp.exp(lam_r_row)
    abr, abi = mag * jnp.cos(lam_i_row), mag * jnp.sin(lam_i_row)
    den = ar_row * ar_row + ai_row * ai_row
    nr = abr - 1.0
    fr = (nr * ar_row + abi * ai_row) / den
    fi = (abi * ar_row - nr * ai_row) / den

    b1 = b1_ref[0]
    lane = lax.broadcasted_iota(jnp.int32, b1.shape, 1)
    b2 = pltpu.roll(b1, S5_STATE, axis=1) * jnp.where(lane < S5_STATE, -1.0, 1.0)
    bbt = fr * b1 + fi * b2
    bbt_sw = fr * b2 - fi * b1
    bbt_ref[0] = bbt

    tau_p = (s - 1 - lax.broadcasted_iota(jnp.int32, (s, LANES), 0)).astype(F32)
    mag_p = jnp.exp(lam_r_row * tau_p)
    ar_p, ai_p = mag_p * jnp.cos(lam_i_row * tau_p), mag_p * jnp.sin(lam_i_row * tau_p)
    repeat_rows = lambda a: jnp.concatenate(
        [jnp.broadcast_to(a[i:i + 1], (S5_GROUP, LANES)) for i in range(s)], axis=0)
    p_ref[0] = (jnp.concatenate([bbt] * s, axis=0) * repeat_rows(ar_p)
                + jnp.concatenate([bbt_sw] * s, axis=0) * repeat_rows(ai_p)).astype(p_ref.dtype)

    crt, cit = crt_ref[0], cit_ref[0]
    power = jnp.minimum(lax.broadcasted_iota(jnp.int32, (S5_STATE, LANES), 1), s).astype(F32)
    mag_c = jnp.exp(lam_r_col * power)
    ar_c, ai_c = mag_c * jnp.cos(lam_i_col * power), mag_c * jnp.sin(lam_i_col * power)
    src = lax.broadcasted_iota(jnp.int32, (LANES, s * S5_GROUP), 0)
    dst_block = jnp.right_shift(lax.broadcasted_iota(jnp.int32, (LANES, s * S5_GROUP), 1), 4)

    def coeffs(tau):
        spread = (src == dst_block + tau).astype(F32)
        ar_t = jnp.dot(ar_c, spread, precision=HIGHEST, preferred_element_type=F32)
        ai_t = jnp.dot(ai_c, spread, precision=HIGHEST, preferred_element_type=F32)
        return jnp.concatenate([ar_t * crt - ai_t * cit, -(ar_t * cit + ai_t * crt)], axis=0)

    w0 = coeffs(0)
    w0_ref[0] = w0
    q_ref[0] = coeffs(1).astype(q_ref.dtype)

    t0 = jnp.dot(bbt, w0, precision=HIGHEST, preferred_element_type=F32)
    lane_t = lax.broadcasted_iota(jnp.int32, t0.shape, 1)
    blocks = [t0]
    for i in range(1, s):
        blocks.append(jnp.where(lane_t >= i * S5_GROUP, pltpu.roll(t0, i * S5_GROUP, axis=1), 0.0))
    t_ref[0] = jnp.concatenate(blocks, axis=0).astype(t_ref.dtype)


def s5_prep(a_re, a_im, log_dt, b_re, b_im, c_re, c_im):
    g, p, n, s = S5_GROUPS, S5_STATE, S5_GROUP, S5_CHUNK
    dup = lambda a: jnp.concatenate([a, a], axis=-1)[:, None, :]
    b1 = jnp.concatenate([b_re.transpose(0, 2, 1), b_im.transpose(0, 2, 1)], axis=-1)
    crt = jnp.tile(c_re.transpose(0, 2, 1), (1, 1, s))
    cit = jnp.tile(c_im.transpose(0, 2, 1), (1, 1, s))
    spec = lambda *shape: pl.BlockSpec((GROUPS_PER_BLOCK,) + shape, lambda i: (i, 0, 0))
    return pl.pallas_call(
        _s5_prep_kernel,
        grid=(g // GROUPS_PER_BLOCK,),
        in_specs=[spec(1, 1), spec(p, 1), spec(p, 1), spec(1, 2 * p), spec(1, 2 * p),
                  spec(n, 2 * p), spec(p, s * n), spec(p, s * n)],
        out_specs=[spec(s * n, s * n), spec(s * n, 2 * p), spec(2 * p, s * n),
                   spec(n, 2 * p), spec(2 * p, s * n)],
        out_shape=[jax.ShapeDtypeStruct((g, s * n, s * n), BF16),
                   jax.ShapeDtypeStruct((g, s * n, 2 * p), BF16),
                   jax.ShapeDtypeStruct((g, 2 * p, s * n), BF16),
                   jax.ShapeDtypeStruct((g, n, 2 * p), F32),
                   jax.ShapeDtypeStruct((g, 2 * p, s * n), F32)],
        compiler_params=_params(1),
        name="s5_prep",
    )(log_dt[:, None, None], a_re[:, :, None], a_im[:, :, None], dup(a_re), dup(a_im), b1, crt, cit)


def _abar_power(ldt_ref, arr_ref, air_ref, power):
    dt = jnp.exp(ldt_ref[...])
    mag = jnp.exp(arr_ref[...] * dt * power)
    ang = air_ref[...] * dt * power
    lane = lax.broadcasted_iota(jnp.int32, mag.shape, 2)
    return mag * jnp.cos(ang), mag * jnp.sin(ang) * jnp.where(lane < S5_STATE, -1.0, 1.0)


def _complex_scale(x, a_r, a_i_signed):
    return a_r * x + a_i_signed * pltpu.roll(x, S5_STATE, axis=x.ndim - 1)


GROUPS_PER_BLOCK = LANES // S5_GROUP


def _s5_prompt_kernel(u_ref, t_ref, p_ref, q_ref, ldt_ref, arr_ref, air_ref, d_ref,
                      y_ref, fin_ref, ug_ref, z_ref, x_ref, yg_ref, *, bsz):
    s, n, gp = S5_CHUNK, S5_GROUP, GROUPS_PER_BLOCK
    rows = u_ref.shape[1]
    n_chunks = rows // bsz
    lane_block = jnp.right_shift(lax.broadcasted_iota(jnp.int32, (rows, LANES), 1), 4)

    for g in range(gp):
        for half in range(s // gp):
            acc = None
            for t8 in range(gp):
                piece = u_ref[half * gp + t8].astype(F32)
                shift = (n * (t8 - g)) % LANES
                if shift:
                    piece = pltpu.roll(piece, shift, axis=1)
                acc = piece if acc is None else jnp.where(lane_block == t8, piece, acc)
            ug_ref[g, :, half * LANES:(half + 1) * LANES] = acc.astype(BF16)

    for g in range(gp):
        z_ref[g] = _dot(ug_ref[g], p_ref[g])

    a_r, a_i = _abar_power(ldt_ref, arr_ref, air_ref, float(s))

    def step(j, x):
        idx = pl.ds(j, bsz, stride=n_chunks)
        x_ref[:, idx, :] = x
        return _complex_scale(x, a_r, a_i) + z_ref[:, idx, :]

    fin_ref[...] = lax.fori_loop(0, n_chunks, step, jnp.zeros(fin_ref.shape, F32))

    for g in range(gp):
        yg_ref[g] = _dot(ug_ref[g], t_ref[g]) + _dot(x_ref[g].astype(BF16), q_ref[g])

    for t in range(s):
        half, t8 = divmod(t, gp)
        acc = None
        for g in range(gp):
            piece = yg_ref[g, :, half * LANES:(half + 1) * LANES]
            shift = (n * (g - t8)) % LANES
            if shift:
                piece = pltpu.roll(piece, shift, axis=1)
            acc = piece if acc is None else jnp.where(lane_block == g, piece, acc)
        y_ref[t] = jax.nn.gelu(acc + d_ref[...] * u_ref[t].astype(F32))


def _steps_major(a, s):
    return a.reshape(a.shape[0] // s, s, a.shape[1]).swapaxes(0, 1)


def s5_prompt(u, bsz, length, t_mat, p_mat, q_mat, log_dt, a_re, a_im, d_skip):
    g, n, s, p2, gp = S5_GROUPS, S5_GROUP, S5_CHUNK, 2 * S5_STATE, GROUPS_PER_BLOCK
    rows = bsz * length // s
    dup = lambda a: jnp.concatenate([a, a], axis=-1)[:, None, :]
    gspec = lambda *shape: pl.BlockSpec((gp,) + shape, lambda k: (k, 0, 0))
    io_spec = pl.BlockSpec((s, rows, LANES), lambda k: (0, 0, k))
    return pl.pallas_call(
        functools.partial(_s5_prompt_kernel, bsz=bsz),
        grid=(g // gp,),
        in_specs=[io_spec,
                  gspec(s * n, s * n), gspec(s * n, p2), gspec(p2, s * n),
                  gspec(1, 1), gspec(1, p2), gspec(1, p2),
                  pl.BlockSpec((1, LANES), lambda k: (0, k))],
        out_specs=[io_spec, gspec(bsz, p2)],
        out_shape=[jax.ShapeDtypeStruct((s, rows, g * n), F32),
                   jax.ShapeDtypeStruct((g, bsz, p2), F32)],
        scratch_shapes=[pltpu.VMEM((gp, rows, s * n), BF16),
                        pltpu.VMEM((gp, rows, p2), F32),
                        pltpu.VMEM((gp, rows, p2), F32),
                        pltpu.VMEM((gp, rows, s * n), F32)],
        compiler_params=_params(1),
        name="s5_prompt",
    )(_steps_major(u, s), t_mat, p_mat, q_mat, log_dt[:, None, None], dup(a_re), dup(a_im),
      d_skip.reshape(1, g * n))


def _s5_sample_kernel(*refs):
    for g in range(refs[0].shape[0]):
        _s5_sample_group(*[r.at[g:g + 1] for r in refs])


def _s5_sample_group(u_ref, xr_ref, xi_ref, bbt_ref, w0_ref, ldt_ref, arr_ref, air_ref, d_ref,
                     y_ref, nr_ref, ni_ref):
    p = S5_STATE
    dt = jnp.exp(ldt_ref[0])
    mag = jnp.exp(arr_ref[0][:, :p] * dt)
    ang = air_ref[0][:, :p] * dt
    abr, abi = mag * jnp.cos(ang), mag * jnp.sin(ang)
    u = u_ref[0].astype(F32)
    z = _dot(u.astype(BF16), bbt_ref[0].astype(BF16))
    xr, xi = xr_ref[0], xi_ref[0]
    new_r = abr * xr - abi * xi + z[:, :p]
    new_i = abr * xi + abi * xr + z[:, p:]
    nr_ref[0] = new_r
    ni_ref[0] = new_i
    c = w0_ref[0][:, :S5_GROUP].astype(BF16)
    y = _dot(new_r.astype(BF16), c[:p]) + _dot(new_i.astype(BF16), c[p:]) + d_ref[0] * u
    y_ref[0] = jax.nn.gelu(y)


def s5_sample(u, x_re, x_im, bbt, w0, log_dt, a_re, a_im, d_skip):
    g, n, p = S5_GROUPS, S5_GROUP, S5_STATE
    bsz = u.shape[0]
    dup = lambda a: jnp.concatenate([a, a], axis=-1)[:, None, :]
    gspec = lambda *shape: pl.BlockSpec((GROUPS_PER_BLOCK,) + shape, lambda i: (i, 0, 0))
    y, new_r, new_i = pl.pallas_call(
        _s5_sample_kernel,
        grid=(g // GROUPS_PER_BLOCK,),
        in_specs=[gspec(bsz, n), gspec(bsz, p), gspec(bsz, p), gspec(n, 2 * p),
                  gspec(2 * p, S5_CHUNK * n), gspec(1, 1), gspec(1, 2 * p), gspec(1, 2 * p),
                  gspec(1, n)],
        out_specs=[gspec(bsz, n), gspec(bsz, p), gspec(bsz, p)],
        out_shape=[jax.ShapeDtypeStruct((g, bsz, n), F32),
                   jax.ShapeDtypeStruct((g, bsz, p), F32),
                   jax.ShapeDtypeStruct((g, bsz, p), F32)],
        compiler_params=_params(1),
        name="s5_sample",
    )(u.reshape(bsz, g, n).transpose(1, 0, 2), x_re.transpose(1, 0, 2), x_im.transpose(1, 0, 2),
      bbt, w0, log_dt[:, None, None], dup(a_re), dup(a_im), d_skip.reshape(g, 1, n))
    return (y.transpose(1, 0, 2).reshape(bsz, g * n), new_r.transpose(1, 0, 2),
            new_i.transpose(1, 0, 2))


FACTORED_DECAY_LIMIT = -60.0


def _gla_kernel(q_ref, k_ref, v_ref, gr_ref, g_ref, nrm_ref, o_ref, st_ref, s_ref, b_ref,
                qb_ref, kb_ref, kend_ref, sc_ref, acc_ref):
    t = pl.program_id(0)

    @pl.when(t == 0)
    def _():
        s_ref[...] = jnp.zeros(s_ref.shape, F32)

    bsz, c = q_ref.shape[0], q_ref.shape[1]
    row = lax.broadcasted_iota(jnp.int32, (c, c), 0)
    col = lax.broadcasted_iota(jnp.int32, (c, c), 1)
    causal = col <= row
    ones_tri = causal.astype(F32)
    step = lax.broadcasted_iota(jnp.int32, (c, GLA_DK), 0)

    lowest = None
    for i in range(bsz):
        b_ref[i] = jnp.dot(ones_tri, g_ref[i], precision=HIGHEST, preferred_element_type=F32)
        end_min = jnp.min(b_ref[i, c - 1:c, :])
        lowest = end_min if lowest is None else jnp.minimum(lowest, end_min)
    factorable = lowest >= FACTORED_DECAY_LIMIT

    pairs = [(i, h) for i in range(bsz) for h in range(GLA_HEADS)]
    head_k = lambda h: slice(h * GLA_DK, (h + 1) * GLA_DK)
    head_v = lambda h: slice(h * GLA_DV, (h + 1) * GLA_DV)

    def finish(i, h, o):
        o = o * lax.rsqrt(jnp.mean(o * o, axis=-1, keepdims=True) + EPS) * nrm_ref[...]
        o_ref[i, :, head_v(h)] = (o * _silu(gr_ref[i, :, head_v(h)].astype(F32))).astype(o_ref.dtype)

    def all_chunks_factored():
        for n, (i, h) in enumerate(pairs):
            b = b_ref[i, :, head_k(h)]
            k = k_ref[i, :, head_k(h)].astype(F32)
            qb_ref[n] = (q_ref[i, :, head_k(h)].astype(F32) * (GLA_DK ** -0.5) * jnp.exp(b)).astype(BF16)
            kb_ref[n] = (k * jnp.exp(-b)).astype(BF16)
            kend_ref[n] = (k * jnp.exp(b[c - 1:c, :] - b)).astype(BF16)
        for n, (i, h) in enumerate(pairs):
            scores = lax.dot_general(qb_ref[n], kb_ref[n], NT, preferred_element_type=F32)
            sc_ref[n] = jnp.where(causal, scores, 0.0).astype(BF16)
        for n, (i, h) in enumerate(pairs):
            v = v_ref[i, :, head_v(h)].astype(BF16)
            s_t = s_ref[i, h]
            acc_ref[n] = _dot(sc_ref[n], v) + lax.dot_general(qb_ref[n], s_t.astype(BF16), NT,
                                                              preferred_element_type=F32)
            s_ref[i, h] = (jnp.exp(b_ref[i, c - 1:c, head_k(h)]) * s_t
                           + lax.dot_general(v, kend_ref[n], TN, preferred_element_type=F32))
        for n, (i, h) in enumerate(pairs):
            finish(i, h, acc_ref[n])

    def intra_stepwise(i, ks, vs, b, q_in, v):
        q = q_ref[i, :, ks].astype(F32) * (GLA_DK ** -0.5)

        def add_sources(tile, acc):
            first = pl.multiple_of(tile * ROW_TILE, ROW_TILE)
            rows = pl.ds(first, ROW_TILE)
            b_s = b_ref[i, rows, ks]
            k_s, v_s = k_ref[i, rows, ks].astype(F32), v_ref[i, rows, vs].astype(F32)
            for r in range(ROW_TILE):
                decay = jnp.exp(jnp.where(step >= first + r, b - b_s[r:r + 1], -jnp.inf))
                score = jnp.sum(q * k_s[r:r + 1] * decay, axis=1, keepdims=True)
                acc = acc + score * v_s[r:r + 1]
            return acc

        return lax.fori_loop(0, c // ROW_TILE, add_sources, jnp.zeros((c, GLA_DV), F32))

    def all_chunks_stepwise():
        for i, h in pairs:
            ks, vs = head_k(h), head_v(h)
            b = b_ref[i, :, ks]
            b_end = b[c - 1:c, :]
            q_in = (q_ref[i, :, ks].astype(F32) * (GLA_DK ** -0.5) * jnp.exp(b)).astype(BF16)
            k_end = (k_ref[i, :, ks].astype(F32) * jnp.exp(b_end - b)).astype(BF16)
            v = v_ref[i, :, vs].astype(BF16)
            s_t = s_ref[i, h]
            o = (intra_stepwise(i, ks, vs, b, q_in, v)
                 + lax.dot_general(q_in, s_t.astype(BF16), NT, preferred_element_type=F32))
            s_ref[i, h] = jnp.exp(b_end) * s_t + lax.dot_general(v, k_end, TN,
                                                                 preferred_element_type=F32)
            finish(i, h, o)

    pl.when(factorable)(all_chunks_factored)
    pl.when(jnp.logical_not(factorable))(all_chunks_stepwise)

    @pl.when(t == pl.num_programs(0) - 1)
    def _():
        st_ref[...] = s_ref[...]


def gla_prompt(proj, g, gla_norm, bsz, length):
    c = CHUNK
    nc = length // c
    hk, hv = GLA_HEADS * GLA_DK, GLA_HEADS * GLA_DV
    rows = lambda width, off: pl.BlockSpec((bsz, c, width), lambda t: (0, t, off // width))
    st_shape = (bsz, GLA_HEADS, GLA_DV, GLA_DK)
    proj3 = proj.reshape(bsz, length, proj.shape[1])
    o, st = pl.pallas_call(
        _gla_kernel,
        grid=(nc,),
        in_specs=[rows(hk, OFF_GQ), rows(hk, OFF_GK), rows(hv, OFF_GV), rows(hv, OFF_GR),
                  rows(hk, 0), pl.BlockSpec((1, GLA_DV), lambda t: (0, 0))],
        out_specs=[rows(hv, 0), pl.BlockSpec(st_shape, lambda t: (0, 0, 0, 0))],
        out_shape=[jax.ShapeDtypeStruct((bsz, length, hv), BF16),
                   jax.ShapeDtypeStruct(st_shape, F32)],
        scratch_shapes=[pltpu.VMEM(st_shape, F32),
                        pltpu.VMEM((bsz, c, hk), F32)]
        + [pltpu.VMEM((bsz * GLA_HEADS, c, GLA_DK), BF16)] * 3
        + [pltpu.VMEM((bsz * GLA_HEADS, c, c), BF16),
           pltpu.VMEM((bsz * GLA_HEADS, c, GLA_DV), F32)],
        compiler_params=_params(1),
        name="gla_prompt",
    )(proj3, proj3, proj3, proj3, g.reshape(bsz, length, hk), gla_norm.reshape(1, GLA_DV))
    return o.reshape(bsz * length, hv), st.swapaxes(2, 3)


def _pair_swap_matrix(n):
    src = lax.broadcasted_iota(jnp.int32, (n, n), 0)
    dst = lax.broadcasted_iota(jnp.int32, (n, n), 1)
    return (src == jnp.bitwise_xor(dst, 1)).astype(BF16)


def _rotate_pairs(x, cos2, sin2, swap):
    return x.astype(F32) * cos2 + _dot(x, swap) * sin2


def _ret_kernel(q_ref, k_ref, v_ref, rg_ref, cos_ref, sin_ref, dm_ref, qd_ref, kd_ref, gc_ref,
                o_ref, st_ref, s_ref, qb_ref, kb_ref, qdec_ref, kdec_ref, sc_ref, acc_ref):
    t = pl.program_id(0)

    @pl.when(t == 0)
    def _():
        s_ref[...] = jnp.zeros(s_ref.shape, F32)

    cos2, sin2, swap = cos_ref[...], sin_ref[...], _pair_swap_matrix(RET_DK)
    pairs = [(i, h) for i in range(q_ref.shape[0]) for h in range(RET_HEADS)]
    head_k = lambda h: slice(h * RET_DK, (h + 1) * RET_DK)
    head_v = lambda h: slice(h * RET_DV, (h + 1) * RET_DV)
    for n, (i, h) in enumerate(pairs):
        q = _rotate_pairs(q_ref[i, :, head_k(h)], cos2, sin2, swap)
        k = _rotate_pairs(k_ref[i, :, head_k(h)], cos2, sin2, swap) * (RET_DK ** -0.5)
        qb_ref[n] = q.astype(BF16)
        kb_ref[n] = k.astype(BF16)
        qdec_ref[n] = (q * qd_ref[h]).astype(BF16)
        kdec_ref[n] = (k * kd_ref[h]).astype(BF16)
    for n, (i, h) in enumerate(pairs):
        scores = lax.dot_general(qb_ref[n], kb_ref[n], NT, preferred_element_type=F32) * dm_ref[h]
        sc_ref[n] = scores.astype(BF16)
    for n, (i, h) in enumerate(pairs):
        v = v_ref[i, :, head_v(h)].astype(BF16)
        s_t = s_ref[i, h]
        acc_ref[n] = _dot(sc_ref[n], v) + lax.dot_general(qdec_ref[n], s_t.astype(BF16), NT,
                                                          preferred_element_type=F32)
        s_ref[i, h] = gc_ref[h] * s_t + lax.dot_general(v, kdec_ref[n], TN, preferred_element_type=F32)
    for n, (i, h) in enumerate(pairs):
        o = acc_ref[n]
        oc = o - jnp.mean(o, axis=-1, keepdims=True)
        o = oc * lax.rsqrt(jnp.mean(oc * oc, axis=-1, keepdims=True) + EPS)
        o_ref[i, :, head_v(h)] = (o * _silu(rg_ref[i, :, head_v(h)].astype(F32))).astype(o_ref.dtype)

    @pl.when(t == pl.num_programs(0) - 1)
    def _():
        st_ref[...] = s_ref[...]


def _rotary_tables(pos):
    half = RET_DK // 2
    inv = 1.0 / (ROPE_BASE ** jnp.linspace(0.0, 1.0, half, dtype=F32))
    ang = pos[:, None] * inv[None, :]
    cos2 = jnp.repeat(jnp.cos(ang), 2, axis=-1)
    sin2 = jnp.stack([-jnp.sin(ang), jnp.sin(ang)], axis=-1).reshape(pos.shape[0], RET_DK)
    return cos2, sin2


def _log_gamma():
    return jnp.log1p(-jnp.power(2.0, -5.0 - jnp.arange(RET_HEADS, dtype=F32)))


def ret_prompt(proj, bsz, length):
    c = CHUNK
    nc = length // c
    hk, hv = RET_HEADS * RET_DK, RET_HEADS * RET_DV
    cos2, sin2 = _rotary_tables(jnp.arange(length, dtype=F32))
    lg = _log_gamma()[:, None, None]
    steps = jnp.arange(c, dtype=F32)
    lag = steps[:, None] - steps[None, :]
    dmat = jnp.where(lag >= 0, jnp.exp(lg * jnp.maximum(lag, 0.0)), 0.0)
    wide = lambda a: jnp.broadcast_to(a, (RET_HEADS, a.shape[1], LANES))
    qdec = wide(jnp.exp(lg * (steps + 1.0)[None, :, None]))
    kdec = wide(jnp.exp(lg * (c - 1.0 - steps)[None, :, None]))
    gchunk = wide(jnp.exp(lg * float(c)))
    rows = lambda width, off: pl.BlockSpec((bsz, c, width), lambda t: (0, t, off // width))
    table = pl.BlockSpec((c, RET_DK), lambda t: (t, 0))
    const = lambda a: pl.BlockSpec(a.shape, lambda t: (0, 0, 0))
    st_shape = (bsz, RET_HEADS, RET_DV, RET_DK)
    proj3 = proj.reshape(bsz, length, proj.shape[1])
    o, st = pl.pallas_call(
        _ret_kernel,
        grid=(nc,),
        in_specs=[rows(hk, OFF_RQ), rows(hk, OFF_RK), rows(hv, OFF_RV), rows(hv, OFF_RG),
                  table, table, const(dmat), const(qdec), const(kdec), const(gchunk)],
        out_specs=[rows(hv, 0), pl.BlockSpec(st_shape, lambda t: (0, 0, 0, 0))],
        out_shape=[jax.ShapeDtypeStruct((bsz, length, hv), BF16),
                   jax.ShapeDtypeStruct(st_shape, F32)],
        scratch_shapes=[pltpu.VMEM(st_shape, F32)]
        + [pltpu.VMEM((bsz * RET_HEADS, c, RET_DK), BF16)] * 4
        + [pltpu.VMEM((bsz * RET_HEADS, c, c), BF16),
           pltpu.VMEM((bsz * RET_HEADS, c, RET_DV), F32)],
        compiler_params=_params(1),
        name="ret_prompt",
    )(proj3, proj3, proj3, proj3, cos2, sin2, dmat, qdec, kdec, gchunk)
    return o.reshape(bsz * length, hv), st.swapaxes(2, 3)


def _one_column(a_t, i):
    col = lax.broadcasted_iota(jnp.int32, a_t.shape, 1)
    return jnp.where(col == i, a_t, 0.0).astype(BF16)


def _gla_sample_kernel(s_ref, q_ref, kt_ref, gt_ref, v_ref, gr_ref, nrm_ref, *rest):
    so_ref, o_ref, acc_ref = rest[-3:]
    ones = jnp.ones((SAMPLE_BLOCK, GLA_DV), BF16)
    head_k = lambda h: slice(h * GLA_DK, (h + 1) * GLA_DK)
    head_v = lambda h: slice(h * GLA_DV, (h + 1) * GLA_DV)
    for h in range(GLA_HEADS):
        v = v_ref[:, head_v(h)].astype(BF16)
        k_t = kt_ref[0, h]
        decay_t = jnp.exp(gt_ref[0, h])
        decay_hi = decay_t.astype(BF16).astype(F32)
        decay_lo = decay_t - decay_hi
        for i in range(SAMPLE_BLOCK):
            decay = _dot(_one_column(decay_hi, i), ones) + _dot(_one_column(decay_lo, i), ones)
            so_ref[i, h] = decay * s_ref[i, h] + _dot(_one_column(k_t, i), v)
    for h in range(GLA_HEADS):
        q = (q_ref[:, head_k(h)].astype(F32) * (GLA_DK ** -0.5)).astype(BF16)
        for i in range(SAMPLE_BLOCK):
            acc_ref[i:i + 1, head_v(h)] = _dot(q, so_ref[i, h].astype(BF16))[i:i + 1]
    for h in range(GLA_HEADS):
        o = acc_ref[:, head_v(h)]
        o = o * lax.rsqrt(jnp.mean(o * o, axis=-1, keepdims=True) + EPS) * nrm_ref[...]
        o_ref[:, head_v(h)] = (o * _silu(gr_ref[:, head_v(h)].astype(F32))).astype(o_ref.dtype)


def _columns(a, heads, dk):
    bsz = a.shape[0]
    return a.reshape(bsz // SAMPLE_BLOCK, SAMPLE_BLOCK, heads, dk).transpose(0, 2, 3, 1)


def _stacked_state_call(kernel_fn, name, state, layer, prev, heads, dk, dv, in_specs, operands,
                        o_spec, o_shape):
    bb = SAMPLE_BLOCK
    st_spec = pl.BlockSpec((None, bb, heads, dk, dv), lambda i: (layer, i, 0, 0, 0))
    aliases = {}
    if prev is not None:
        in_specs = in_specs + [pl.BlockSpec(memory_space=pl.ANY)]
        operands = operands + (prev,)
        aliases = {len(in_specs): 0}
    return pl.pallas_call(
        kernel_fn,
        grid=(state.shape[1] // bb,),
        in_specs=[st_spec] + in_specs,
        out_specs=[st_spec, o_spec],
        out_shape=[jax.ShapeDtypeStruct(state.shape, F32), o_shape],
        scratch_shapes=[pltpu.VMEM((bb, heads * dv), F32)],
        input_output_aliases=aliases,
        compiler_params=_params(1),
        name=name,
    )(state, *operands)


def gla_sample(proj, g, state, gla_norm, layer, prev):
    bsz = proj.shape[0]
    bb = SAMPLE_BLOCK
    hk, hv = GLA_HEADS * GLA_DK, GLA_HEADS * GLA_DV
    col_spec = pl.BlockSpec((1, GLA_HEADS, GLA_DK, bb), lambda i: (i, 0, 0, 0))
    rows = lambda off: pl.BlockSpec((bb, hv), lambda i: (i, off // hv))
    return _stacked_state_call(
        _gla_sample_kernel, "gla_sample", state, layer, prev, GLA_HEADS, GLA_DK, GLA_DV,
        [pl.BlockSpec((bb, hk), lambda i: (i, OFF_GQ // hk)), col_spec, col_spec, rows(OFF_GV),
         rows(OFF_GR), pl.BlockSpec((None, 1, GLA_DV), lambda i: (layer, 0, 0))],
        (proj, _columns(proj[:, OFF_GK:OFF_GK + hk], GLA_HEADS, GLA_DK), _columns(g, GLA_HEADS, GLA_DK),
         proj, proj, gla_norm.reshape(gla_norm.shape[0], 1, GLA_DV)),
        rows(0), jax.ShapeDtypeStruct((bsz, hv), BF16))


def _rot_kernel(q_ref, k_ref, cos_ref, sin_ref, qo_ref, ko_ref):
    cos2, sin2, swap = cos_ref[...], sin_ref[...], _pair_swap_matrix(RET_DK)
    for h in range(RET_HEADS):
        ks = slice(h * RET_DK, (h + 1) * RET_DK)
        qo_ref[:, ks] = _rotate_pairs(q_ref[:, ks], cos2, sin2, swap)
        ko_ref[:, ks] = _rotate_pairs(k_ref[:, ks], cos2, sin2, swap) * (RET_DK ** -0.5)


def _ret_sample_kernel(s_ref, q_ref, kt_ref, gam_ref, v_ref, rg_ref, *rest):
    so_ref, o_ref, acc_ref = rest[-3:]
    head_k = lambda h: slice(h * RET_DK, (h + 1) * RET_DK)
    head_v = lambda h: slice(h * RET_DV, (h + 1) * RET_DV)
    for h in range(RET_HEADS):
        v = v_ref[:, head_v(h)].astype(BF16)
        k_t = kt_ref[0, h]
        for i in range(SAMPLE_BLOCK):
            so_ref[i, h] = gam_ref[h] * s_ref[i, h] + _dot(_one_column(k_t, i), v)
    for h in range(RET_HEADS):
        q = q_ref[:, head_k(h)].astype(BF16)
        for i in range(SAMPLE_BLOCK):
            acc_ref[i:i + 1, head_v(h)] = _dot(q, so_ref[i, h].astype(BF16))[i:i + 1]
    for h in range(RET_HEADS):
        o = acc_ref[:, head_v(h)]
        oc = o - jnp.mean(o, axis=-1, keepdims=True)
        o = oc * lax.rsqrt(jnp.mean(oc * oc, axis=-1, keepdims=True) + EPS)
        o_ref[:, head_v(h)] = (o * _silu(rg_ref[:, head_v(h)].astype(F32))).astype(o_ref.dtype)


def ret_sample(proj, state, layer, prev):
    bsz = proj.shape[0]
    bb = SAMPLE_BLOCK
    hk, hv = RET_HEADS * RET_DK, RET_HEADS * RET_DV
    cos2, sin2 = _rotary_tables(PAST_LEN + jnp.arange(1, dtype=F32))
    full = pl.BlockSpec((bsz, hk), lambda i: (0, 0))
    q_rot, k_rot = pl.pallas_call(
        _rot_kernel,
        grid=(1,),
        in_specs=[pl.BlockSpec((bsz, hk), lambda i: (0, OFF_RQ // hk)),
                  pl.BlockSpec((bsz, hk), lambda i: (0, OFF_RK // hk)),
                  pl.BlockSpec((1, RET_DK), lambda i: (0, 0)), pl.BlockSpec((1, RET_DK), lambda i: (0, 0))],
        out_specs=[full, full],
        out_shape=[jax.ShapeDtypeStruct((bsz, hk), F32)] * 2,
        compiler_params=_params(1),
        name="ret_rotary",
    )(proj, proj, cos2, sin2)
    gamma = jnp.broadcast_to(jnp.exp(_log_gamma())[:, None, None], (RET_HEADS, 1, LANES))
    col_spec = pl.BlockSpec((1, RET_HEADS, RET_DK, bb), lambda i: (i, 0, 0, 0))
    rows = lambda off: pl.BlockSpec((bb, hv), lambda i: (i, off // hv))
    return _stacked_state_call(
        _ret_sample_kernel, "ret_sample", state, layer, prev, RET_HEADS, RET_DK, RET_DV,
        [rows(0), col_spec, pl.BlockSpec(gamma.shape, lambda i: (0, 0, 0)), rows(OFF_RV), rows(OFF_RG)],
        (q_rot, _columns(k_rot, RET_HEADS, RET_DK), gamma, proj, proj),
        rows(0), jax.ShapeDtypeStruct((bsz, hv), BF16))


def _layer(xp, xs, layer, prompt_shape, states, w, s5_mats):
    t_mat, p_mat, q_mat, bbt, w0 = s5_mats
    bsz, length = prompt_shape
    mp = xp.shape[0]
    tm, tm_wide, tn = min(mp, 512), min(mp, 1024), 512
    s5_par = (w["s5_log_dt"][layer], w["s5_a_re"][layer], w["s5_a_im"][layer], w["s5_d"][layer])
    st_s5_re, st_s5_im, st_gla, st_ret, prev_gla, prev_ret = states

    gate_w = (w["w_in_rows"], w["gla_w_gate"], w["gla_b_gate"], layer)
    hp, g_p = norm_and_gate(xp, w["norm_mix"][layer], *gate_w)
    hs, g_s = norm_and_gate(xs, w["norm_mix"][layer], *gate_w)
    proj_p, proj_s = in_proj(hp, hs, w["w_in_rows"], layer, tm_wide, 2 * tn)

    y5_p, x_fin = s5_prompt(proj_p[:, OFF_U:OFF_U + S5_WIDTH], bsz, length, t_mat, p_mat, q_mat, *s5_par)
    y5_p = y5_p.reshape(mp, S5_WIDTH)
    x_fin = x_fin.transpose(1, 0, 2)
    o_gla_p, new_gla_p = gla_prompt(proj_p, g_p, w["gla_norm"][layer], bsz, length)
    o_ret_p, new_ret_p = ret_prompt(proj_p, bsz, length)

    y5_s, new_s5_re_s, new_s5_im_s = s5_sample(proj_s[:, OFF_U:OFF_U + S5_WIDTH], st_s5_re[layer],
                                               st_s5_im[layer], bbt, w0, *s5_par)
    new_gla_s, o_gla_s = gla_sample(proj_s, g_s, st_gla, w["gla_norm"], layer, prev_gla)
    new_ret_s, o_ret_s = ret_sample(proj_s, st_ret, layer, prev_ret)

    a5_p, a5_s = s5_glu([y5_p, y5_s], w["s5_w_glu"], w["s5_b_glu"], layer, tm, 2 * tn)
    a5_p = a5_p.reshape(S5_CHUNK, mp // S5_CHUNK, S5_WIDTH).swapaxes(0, 1).reshape(mp, S5_WIDTH)
    merged = merge_branches([(a5_p, o_gla_p, o_ret_p), (a5_s, o_gla_s, o_ret_s)], [proj_p, proj_s],
                            w["s5_w_out"], w["gla_w_out"], w["ret_w_out"], layer, tm, 2 * tn)
    xp, xs = residual_matmul(merged, w["w_mix_out"], [xp, xs], layer, tm_wide, 2 * tn)
    h2 = [rmsnorm(xp, w["norm_ffn"][layer], BF16), rmsnorm(xs, w["norm_ffn"][layer], BF16)]
    ff = ffn_up(h2, w["w_ffn_gate"], w["w_ffn_up"], layer, tm_wide, tn)
    xp, xs = residual_matmul(ff, w["w_ffn_down"], [xp, xs], layer, tm_wide, tn)
    new_p = (x_fin[..., :S5_STATE], x_fin[..., S5_STATE:], new_gla_p, new_ret_p)
    return xp, xs, new_p, (new_s5_re_s, new_s5_im_s, new_gla_s, new_ret_s)


def kernel(x_prompt, x_sample, state_s5_re, state_s5_im, state_gla, state_ret, norm_mix, w_in, s5_a_re, s5_a_im, s5_log_dt, s5_b_re, s5_b_im, s5_c_re, s5_c_im, s5_d, s5_w_glu, s5_b_glu, s5_w_out, gla_w_gate, gla_b_gate, gla_norm, gla_w_out, ret_w_out, w_mix_out, norm_ffn, w_ffn_gate, w_ffn_up, w_ffn_down, norm_final):
    bp, lp, d = x_prompt.shape
    bs = x_sample.shape[0]
    depth = w_in.shape[0]
    hp = x_prompt.reshape(bp * lp, d)
    hs = x_sample.reshape(bs, d)
    w = dict(norm_mix=norm_mix, s5_log_dt=s5_log_dt, s5_a_re=s5_a_re, s5_a_im=s5_a_im, s5_d=s5_d,
             s5_w_glu=s5_w_glu, s5_b_glu=s5_b_glu, s5_w_out=s5_w_out, gla_w_gate=gla_w_gate,
             gla_b_gate=gla_b_gate, gla_norm=gla_norm, gla_w_out=gla_w_out, ret_w_out=ret_w_out,
             w_mix_out=w_mix_out, norm_ffn=norm_ffn, w_ffn_gate=w_ffn_gate, w_ffn_up=w_ffn_up,
             w_ffn_down=w_ffn_down,
             w_in_rows=jnp.swapaxes(w_in, 1, 2).reshape(depth, w_in.shape[2] // GLA_RANK, GLA_RANK, d))
    outs_p, outs_s = [], []
    new_gla_s = new_ret_s = None
    for l in range(depth):
        s5_mats = s5_prep(s5_a_re[l], s5_a_im[l], s5_log_dt[l], s5_b_re[l], s5_b_im[l], s5_c_re[l],
                          s5_c_im[l])
        hp, hs, new_p, (s5r, s5i, new_gla_s, new_ret_s) = _layer(
            hp, hs, l, (bp, lp), (state_s5_re, state_s5_im, state_gla, state_ret, new_gla_s, new_ret_s),
            w, s5_mats)
        outs_p.append(new_p)
        outs_s.append((s5r, s5i))
    y_prompt = rmsnorm(hp, norm_final, F32).reshape(bp, lp, d)
    y_sample = rmsnorm(hs, norm_final, F32).reshape(bs, 1, d)
    stack = lambda outs, i: jnp.stack([o[i] for o in outs])
    return (y_prompt, y_sample,
            stack(outs_p, 0), stack(outs_p, 1), stack(outs_p, 2), stack(outs_p, 3),
            stack(outs_s, 0), stack(outs_s, 1), new_gla_s, new_ret_s)
```

```python
import functools
import math

import jax
import jax.numpy as jnp
import numpy as np
from jax import lax
from jax.experimental import pallas as pl
from jax.experimental.pallas import tpu as pltpu

F32 = jnp.float32
BF16 = jnp.bfloat16
HIGHEST = lax.Precision.HIGHEST

D_MODEL = 2048
S5_WIDTH = D_MODEL // 2
S5_GROUP = 16
S5_GROUPS = S5_WIDTH // S5_GROUP
S5_STATE = 64
S5_CHUNK = 16
GLA_HEADS = 4
GLA_DK = D_MODEL // 16
GLA_DV = D_MODEL // 8
GLA_RANK = 16
GLA_GATE_NORM = 16.0
RET_HEADS = 8
RET_DK = D_MODEL // 16
RET_DV = D_MODEL // 16
ROPE_BASE = 10000.0
PAST_LEN = 16384
N_BRANCH = 3
D_FF = -(-8 * D_MODEL // (3 * 256)) * 256
EPS = 1e-6
CHUNK = 128
LANES = 128
ROW_TILE = 16
SAMPLE_BLOCK = 16

OFF_U = 0
OFF_GQ = OFF_U + S5_WIDTH
OFF_GK = OFF_GQ + GLA_HEADS * GLA_DK
OFF_GV = OFF_GK + GLA_HEADS * GLA_DK
OFF_GR = OFF_GV + GLA_HEADS * GLA_DV
OFF_RQ = OFF_GR + GLA_HEADS * GLA_DV
OFF_RK = OFF_RQ + RET_HEADS * RET_DK
OFF_RV = OFF_RK + RET_HEADS * RET_DK
OFF_RG = OFF_RV + RET_HEADS * RET_DV
OFF_MG = OFF_RG + RET_HEADS * RET_DV
W_MAIN = OFF_MG + N_BRANCH * D_MODEL
GLR_COL = OFF_RQ

VMEM_LIMIT = 56 * 1024 * 1024

NT = (((1,), (1,)), ((), ()))
TN = (((0,), (0,)), ((), ()))


def _params(n_grid):
    return pltpu.CompilerParams(dimension_semantics=("arbitrary",) * n_grid,
                                vmem_limit_bytes=VMEM_LIMIT)


def _silu(x):
    return x * jax.nn.sigmoid(x)


def _rmsnorm_kernel(x_ref, g_ref, o_ref):
    x = x_ref[...]
    y = x * lax.rsqrt(jnp.mean(x * x, axis=-1, keepdims=True) + EPS)
    o_ref[...] = (y * g_ref[...]).astype(o_ref.dtype)


def rmsnorm(x, gain, out_dtype):
    m, d = x.shape
    tm = min(m, 512)
    return pl.pallas_call(
        _rmsnorm_kernel,
        grid=(m // tm,),
        in_specs=[pl.BlockSpec((tm, d), lambda i: (i, 0)),
                  pl.BlockSpec((1, d), lambda i: (0, 0))],
        out_specs=pl.BlockSpec((tm, d), lambda i: (i, 0)),
        out_shape=jax.ShapeDtypeStruct((m, d), out_dtype),
        compiler_params=_params(1),
        name="rmsnorm",
    )(x, gain.reshape(1, d))


def _dot(a, b):
    return jnp.dot(a, b, preferred_element_type=F32)


def _in_proj_kernel(a_ref, as_ref, w_ref, o_ref, os_ref, wb_ref):
    i = pl.program_id(1)

    @pl.when(i == 0)
    def _():
        wb_ref[...] = w_ref[...].reshape(wb_ref.shape).astype(BF16)

    def project(src_ref, dst_ref):
        dst_ref[...] = lax.dot_general(src_ref[...], wb_ref[...], NT,
                                       preferred_element_type=F32).astype(dst_ref.dtype)

    pl.when(i == 0)(lambda: project(as_ref, os_ref))
    pl.when(i > 0)(lambda: project(a_ref, o_ref))


def in_proj(h, h_s, w_rows, layer, tm, tn):
    m, k = h.shape
    ms = h_s.shape[0]
    nb = m // tm
    rb = tn // GLA_RANK
    skip_from = GLR_COL // tn

    def w_index(j, i):
        return (layer, rb * j + jnp.where(j >= skip_from, 1, 0), 0, 0)

    return pl.pallas_call(
        _in_proj_kernel,
        grid=(W_MAIN // tn, nb + 1),
        in_specs=[pl.BlockSpec((tm, k), lambda j, i: (jnp.maximum(i - 1, 0), 0)),
                  pl.BlockSpec((ms, k), lambda j, i: (0, 0)),
                  pl.BlockSpec((pl.Element(1), pl.Element(rb), pl.Element(GLA_RANK), pl.Element(k)),
                               w_index)],
        out_specs=[pl.BlockSpec((tm, tn), lambda j, i: (jnp.maximum(i - 1, 0), j)),
                   pl.BlockSpec((ms, tn), lambda j, i: (0, j))],
        out_shape=[jax.ShapeDtypeStruct((m, W_MAIN), BF16), jax.ShapeDtypeStruct((ms, W_MAIN), BF16)],
        scratch_shapes=[pltpu.VMEM((tn, k), BF16)],
        compiler_params=_params(2),
        name="in_proj",
    )(h, h_s, w_rows)


def _fused_matmul(name, body, streams, weights, biases, layer, n, tm, tn, out_dtype):
    nb = streams[0][0][0].shape[0] // tm
    extra = len(streams) - 1
    specs, operands, counts, out_specs, out_shapes = [], [], [], [], []
    for s, (row_ops, tile_ops) in enumerate(streams):
        m_s = row_ops[0].shape[0]
        rows = tm if s == 0 else m_s
        row_block = (lambda i: jnp.maximum(i - extra, 0)) if s == 0 else (lambda i: 0)
        for a in row_ops:
            specs.append(pl.BlockSpec((rows, a.shape[1]), lambda j, i, r=row_block: (r(i), 0)))
            operands.append(a)
        for a, first in tile_ops:
            specs.append(pl.BlockSpec((rows, tn), lambda j, i, r=row_block, c=first // tn: (r(i), c + j)))
            operands.append(a)
        counts.append((len(row_ops), len(tile_ops)))
        out_specs.append(pl.BlockSpec((rows, tn), lambda j, i, r=row_block: (r(i), j)))
        out_shapes.append(jax.ShapeDtypeStruct((m_s, n), out_dtype))
    weight_bytes = sum(wt.shape[1] * tn * 4 for wt in weights)
    weight_buffers = 2 if 2 * weight_bytes <= VMEM_LIMIT // 3 else 1
    for wt in weights:
        specs.append(pl.BlockSpec((None, wt.shape[1], tn), lambda j, i: (layer, 0, j),
                                  pipeline_mode=pl.Buffered(weight_buffers)))
        operands.append(wt)
    for b in biases:
        specs.append(pl.BlockSpec((None, 1, tn), lambda j, i: (layer, 0, j)))
        operands.append(b)

    def kernel_fn(*refs):
        refs = list(refs)
        stream_refs = [(tuple(refs.pop(0) for _ in range(nr)), tuple(refs.pop(0) for _ in range(nt)))
                       for nr, nt in counts]
        w_refs = [refs.pop(0) for _ in weights]
        b_refs = [refs.pop(0) for _ in biases]
        o_refs = [refs.pop(0) for _ in streams]
        wb_refs = refs
        i = pl.program_id(1)

        @pl.when(i == 0)
        def _():
            for w_ref, wb_ref in zip(w_refs, wb_refs):
                wb_ref[...] = w_ref[...].astype(BF16)

        def run(s):
            row_refs, tile_refs = stream_refs[s]
            out = body(*[r[...] for r in row_refs], *[r[...] for r in wb_refs],
                       *[r[...] for r in tile_refs], *[r[...] for r in b_refs])
            o_refs[s][...] = out.astype(out_dtype)

        for s in range(1, len(streams)):
            pl.when(i == s - 1)(functools.partial(run, s))
        pl.when(i >= extra)(functools.partial(run, 0))

    return pl.pallas_call(
        kernel_fn,
        grid=(n // tn, nb + len(streams) - 1),
        in_specs=specs,
        out_specs=out_specs,
        out_shape=out_shapes,
        scratch_shapes=[pltpu.VMEM((wt.shape[1], tn), BF16) for wt in weights],
        compiler_params=_params(2),
        name=name,
    )(*operands)


def _norm_gate_kernel(x_ref, gain_ref, wl_ref, wg_ref, b_ref, h_ref, g_ref):
    x = x_ref[...]
    h = (x * lax.rsqrt(jnp.mean(x * x, axis=-1, keepdims=True) + EPS) * gain_ref[...]).astype(BF16)
    h_ref[...] = h
    pad = LANES - GLA_RANK
    wl = jnp.concatenate([wl_ref[0], jnp.zeros((pad, wl_ref.shape[2]), F32)], axis=0).astype(BF16)
    wg = jnp.concatenate([wg_ref[...], jnp.zeros((pad, wg_ref.shape[1]), F32)], axis=0).astype(BF16)
    glr = lax.dot_general(h, wl, NT, preferred_element_type=F32)
    logit = _dot(glr.astype(BF16), wg) + b_ref[...]
    log_sig = jnp.minimum(logit, 0.0) - jnp.log1p(jnp.exp(-jnp.abs(logit)))
    g_ref[...] = log_sig / GLA_GATE_NORM


def norm_and_gate(x, gain, w_rows, w_gate, b_gate, layer):
    m, k = x.shape
    n = w_gate.shape[2]
    tm = min(m, 512)
    return pl.pallas_call(
        _norm_gate_kernel,
        grid=(m // tm,),
        in_specs=[pl.BlockSpec((tm, k), lambda i: (i, 0)),
                  pl.BlockSpec((1, k), lambda i: (0, 0)),
                  pl.BlockSpec((None, 1, GLA_RANK, k), lambda i: (layer, GLR_COL // GLA_RANK, 0, 0)),
                  pl.BlockSpec((None, GLA_RANK, n), lambda i: (layer, 0, 0)),
                  pl.BlockSpec((None, 1, n), lambda i: (layer, 0, 0))],
        out_specs=[pl.BlockSpec((tm, k), lambda i: (i, 0)), pl.BlockSpec((tm, n), lambda i: (i, 0))],
        out_shape=[jax.ShapeDtypeStruct((m, k), BF16), jax.ShapeDtypeStruct((m, n), F32)],
        compiler_params=_params(1),
        name="norm_and_gate",
    )(x, gain.reshape(1, k), w_rows, w_gate, b_gate.reshape(b_gate.shape[0], 1, n))


def s5_glu(y5_streams, w, b, layer, tm, tn):
    n = w.shape[2]
    body = lambda y, wb, y_tile, bias: y_tile * jax.nn.sigmoid(_dot(y.astype(BF16), wb) + bias)
    return _fused_matmul("s5_glu", body, [([y], [(y, 0)]) for y in y5_streams], [w],
                         [b.reshape(b.shape[0], 1, n)], layer, n, tm, tn, BF16)


def merge_branches(branch_streams, proj_streams, w5, wg, wr, layer, tm, tn):
    def body(a5, ag, ar, w5b, wgb, wrb, m5, mg, mr):
        gate = lambda t: jax.nn.sigmoid(t.astype(F32))
        return gate(m5) * _dot(a5, w5b) + gate(mg) * _dot(ag, wgb) + gate(mr) * _dot(ar, wrb)

    streams = [(list(branches), [(proj, OFF_MG + i * D_MODEL) for i in range(N_BRANCH)])
               for branches, proj in zip(branch_streams, proj_streams)]
    return _fused_matmul("merge_branches", body, streams, [w5, wg, wr], [], layer, w5.shape[2],
                         tm, tn, BF16)


def residual_matmul(a_streams, w, x_streams, layer, tm, tn):
    body = lambda a, wb, x: x + _dot(a, wb)
    return _fused_matmul("residual_matmul", body, [([a], [(x, 0)]) for a, x in zip(a_streams, x_streams)],
                         [w], [], layer, w.shape[2], tm, tn, F32)


def ffn_up(h_streams, wg, wu, layer, tm, tn):
    body = lambda h, wgb, wub: _silu(_dot(h, wgb)) * _dot(h, wub)
    return _fused_matmul("ffn_up", body, [([h], []) for h in h_streams], [wg, wu], [], layer,
                         wg.shape[2], tm, tn, BF16)


def _s5_prep_kernel(*refs):
    for g in range(refs[0].shape[0]):
        _s5_prep_group(*[r.at[g:g + 1] for r in refs])


def _s5_prep_group(ldt_ref, arc_ref, aic_ref, arr_ref, air_ref, b1_ref, crt_ref, cit_ref,
                   t_ref, p_ref, q_ref, bbt_ref, w0_ref):
    s = S5_CHUNK
    dt = jnp.exp(ldt_ref[0])
    ar_row, ai_row = arr_ref[0], air_ref[0]
    lam_r_row, lam_i_row = ar_row * dt, ai_row * dt
    lam_r_col, lam_i_col = arc_ref[0] * dt, aic_ref[0] * dt

    mag = jnp.exp(lam_r_row)
    abr, abi = mag * jnp.cos(lam_i_row), mag * jnp.sin(lam_i_row)
    den = ar_row * ar_row + ai_row * ai_row
    nr = abr - 1.0
    fr = (nr * ar_row + abi * ai_row) / den
    fi = (abi * ar_row - nr * ai_row) / den

    b1 = b1_ref[0]
    lane = lax.broadcasted_iota(jnp.int32, b1.shape, 1)
    b2 = pltpu.roll(b1, S5_STATE, axis=1) * jnp.where(lane < S5_STATE, -1.0, 1.0)
    bbt = fr * b1 + fi * b2
    bbt_sw = fr * b2 - fi * b1
    bbt_ref[0] = bbt

    tau_p = (s - 1 - lax.broadcasted_iota(jnp.int32, (s, LANES), 0)).astype(F32)
    mag_p = jnp.exp(lam_r_row * tau_p)
    ar_p, ai_p = mag_p * jnp.cos(lam_i_row * tau_p), mag_p * jnp.sin(lam_i_row * tau_p)
    repeat_rows = lambda a: jnp.concatenate(
        [jnp.broadcast_to(a[i:i + 1], (S5_GROUP, LANES)) for i in range(s)], axis=0)
    p_ref[0] = (jnp.concatenate([bbt] * s, axis=0) * repeat_rows(ar_p)
                + jnp.concatenate([bbt_sw] * s, axis=0) * repeat_rows(ai_p)).astype(p_ref.dtype)

    crt, cit = crt_ref[0], cit_ref[0]
    power = jnp.minimum(lax.broadcasted_iota(jnp.int32, (S5_STATE, LANES), 1), s).astype(F32)
    mag_c = jnp.exp(lam_r_col * power)
    ar_c, ai_c = mag_c * jnp.cos(lam_i_col * power), mag_c * jnp.sin(lam_i_col * power)
    src = lax.broadcasted_iota(jnp.int32, (LANES, s * S5_GROUP), 0)
    dst_block = jnp.right_shift(lax.broadcasted_iota(jnp.int32, (LANES, s * S5_GROUP), 1), 4)

    def coeffs(tau):
        spread = (src == dst_block + tau).astype(F32)
        ar_t = jnp.dot(ar_c, spread, precision=HIGHEST, preferred_element_type=F32)
        ai_t = jnp.dot(ai_c, spread, precision=HIGHEST, preferred_element_type=F32)
        return jnp.concatenate([ar_t * crt - ai_t * cit, -(ar_t * cit + ai_t * crt)], axis=0)

    w0 = coeffs(0)
    w0_ref[0] = w0
    q_ref[0] = coeffs(1).astype(q_ref.dtype)

    t0 = jnp.dot(bbt, w0, precision=HIGHEST, preferred_element_type=F32)
    lane_t = lax.broadcasted_iota(jnp.int32, t0.shape, 1)
    blocks = [t0]
    for i in range(1, s):
        blocks.append(jnp.where(lane_t >= i * S5_GROUP, pltpu.roll(t0, i * S5_GROUP, axis=1), 0.0))
    t_ref[0] = jnp.concatenate(blocks, axis=0).astype(t_ref.dtype)


def s5_prep(a_re, a_im, log_dt, b_re, b_im, c_re, c_im):
    g, p, n, s = S5_GROUPS, S5_STATE, S5_GROUP, S5_CHUNK
    dup = lambda a: jnp.concatenate([a, a], axis=-1)[:, None, :]
    b1 = jnp.concatenate([b_re.transpose(0, 2, 1), b_im.transpose(0, 2, 1)], axis=-1)
    crt = jnp.tile(c_re.transpose(0, 2, 1), (1, 1, s))
    cit = jnp.tile(c_im.transpose(0, 2, 1), (1, 1, s))
    spec = lambda *shape: pl.BlockSpec((GROUPS_PER_BLOCK,) + shape, lambda i: (i, 0, 0))
    return pl.pallas_call(
        _s5_prep_kernel,
        grid=(g // GROUPS_PER_BLOCK,),
        in_specs=[spec(1, 1), spec(p, 1), spec(p, 1), spec(1, 2 * p), spec(1, 2 * p),
                  spec(n, 2 * p), spec(p, s * n), spec(p, s * n)],
        out_specs=[spec(s * n, s * n), spec(s * n, 2 * p), spec(2 * p, s * n),
                   spec(n, 2 * p), spec(2 * p, s * n)],
        out_shape=[jax.ShapeDtypeStruct((g, s * n, s * n), BF16),
                   jax.ShapeDtypeStruct((g, s * n, 2 * p), BF16),
                   jax.ShapeDtypeStruct((g, 2 * p, s * n), BF16),
                   jax.ShapeDtypeStruct((g, n, 2 * p), F32),
                   jax.ShapeDtypeStruct((g, 2 * p, s * n), F32)],
        compiler_params=_params(1),
        name="s5_prep",
    )(log_dt[:, None, None], a_re[:, :, None], a_im[:, :, None], dup(a_re), dup(a_im), b1, crt, cit)


def _abar_power(ldt_ref, arr_ref, air_ref, power):
    dt = jnp.exp(ldt_ref[...])
    mag = jnp.exp(arr_ref[...] * dt * power)
    ang = air_ref[...] * dt * power
    lane = lax.broadcasted_iota(jnp.int32, mag.shape, 2)
    return mag * jnp.cos(ang), mag * jnp.sin(ang) * jnp.where(lane < S5_STATE, -1.0, 1.0)


GROUPS_PER_BLOCK = LANES // S5_GROUP
BLOCK_TRANSPOSE_ROWS = 128


def _transpose_lane_blocks(xs):
    count = len(xs)
    lane_block = jnp.right_shift(lax.broadcasted_iota(jnp.int32, xs[0].shape, 1), 4)
    dist = count // 2
    while dist:
        upper = jnp.bitwise_and(lane_block, dist) != 0
        nxt = list(xs)
        for i in range(count):
            if not i & dist:
                lo, hi = xs[i], xs[i + dist]
                nxt[i] = jnp.where(upper, pltpu.roll(hi, S5_GROUP * dist, axis=1), lo)
                nxt[i + dist] = jnp.where(upper, hi, pltpu.roll(lo, LANES - S5_GROUP * dist, axis=1))
        xs = nxt
        dist //= 2
    return xs


def _s5_prompt_kernel(u_ref, t_ref, p_ref, q_ref, ldt_ref, arr_ref, air_ref, d_ref,
                      y_ref, fin_ref, ug_ref, z_ref, zsw_ref, x_ref, yg_ref, *, bsz):
    s, gp = S5_CHUNK, GROUPS_PER_BLOCK
    rows = u_ref.shape[1]
    n_chunks = rows // bsz
    row_tiles = [slice(r, r + BLOCK_TRANSPOSE_ROWS) for r in range(0, rows, BLOCK_TRANSPOSE_ROWS)]

    for half in range(s // gp):
        for rt in row_tiles:
            by_group = _transpose_lane_blocks([u_ref[half * gp + t8, rt, :].astype(F32)
                                               for t8 in range(gp)])
            for g in range(gp):
                ug_ref[g, rt, half * LANES:(half + 1) * LANES] = by_group[g].astype(BF16)

    for g in range(gp):
        z = _dot(ug_ref[g], p_ref[g])
        z_ref[g] = z
        zsw_ref[g] = pltpu.roll(z, S5_STATE, axis=1)

    a_r, a_i = _abar_power(ldt_ref, arr_ref, air_ref, float(s))

    def step(j, carry):
        x, x_sw = carry
        idx = pl.ds(j, bsz, stride=n_chunks)
        x_ref[:, idx, :] = x
        return (a_r * x + a_i * x_sw + z_ref[:, idx, :],
                a_r * x_sw + (-a_i) * x + zsw_ref[:, idx, :])

    start = jnp.zeros(fin_ref.shape, F32)
    fin_ref[...] = lax.fori_loop(0, n_chunks, step, (start, start))[0]

    for g in range(gp):
        yg_ref[g] = _dot(ug_ref[g], t_ref[g]) + _dot(x_ref[g].astype(BF16), q_ref[g])

    for half in range(s // gp):
        for rt in row_tiles:
            by_step = _transpose_lane_blocks([yg_ref[g, rt, half * LANES:(half + 1) * LANES]
                                              for g in range(gp)])
            for t8 in range(gp):
                t = half * gp + t8
                y_ref[t, rt, :] = jax.nn.gelu(by_step[t8] + d_ref[...] * u_ref[t, rt, :].astype(F32))


def _steps_major(a, s):
    return a.reshape(a.shape[0] // s, s, a.shape[1]).swapaxes(0, 1)


def s5_prompt(u, bsz, length, t_mat, p_mat, q_mat, log_dt, a_re, a_im, d_skip):
    g, n, s, p2, gp = S5_GROUPS, S5_GROUP, S5_CHUNK, 2 * S5_STATE, GROUPS_PER_BLOCK
    rows = bsz * length // s
    dup = lambda a: jnp.concatenate([a, a], axis=-1)[:, None, :]
    gspec = lambda *shape: pl.BlockSpec((gp,) + shape, lambda k: (k, 0, 0))
    io_spec = pl.BlockSpec((s, rows, LANES), lambda k: (0, 0, k))
    return pl.pallas_call(
        functools.partial(_s5_prompt_kernel, bsz=bsz),
        grid=(g // gp,),
        in_specs=[io_spec,
                  gspec(s * n, s * n), gspec(s * n, p2), gspec(p2, s * n),
                  gspec(1, 1), gspec(1, p2), gspec(1, p2),
                  pl.BlockSpec((1, LANES), lambda k: (0, k))],
        out_specs=[io_spec, gspec(bsz, p2)],
        out_shape=[jax.ShapeDtypeStruct((s, rows, g * n), F32),
                   jax.ShapeDtypeStruct((g, bsz, p2), F32)],
        scratch_shapes=[pltpu.VMEM((gp, rows, s * n), BF16),
                        pltpu.VMEM((gp, rows, p2), F32),
                        pltpu.VMEM((gp, rows, p2), F32),
                        pltpu.VMEM((gp, rows, p2), F32),
                        pltpu.VMEM((gp, rows, s * n), F32)],
        compiler_params=_params(1),
        name="s5_prompt",
    )(_steps_major(u, s), t_mat, p_mat, q_mat, log_dt[:, None, None], dup(a_re), dup(a_im),
      d_skip.reshape(1, g * n))


def _s5_sample_kernel(*refs):
    for g in range(refs[0].shape[0]):
        _s5_sample_group(*[r.at[g:g + 1] for r in refs])


def _s5_sample_group(u_ref, xr_ref, xi_ref, bbt_ref, w0_ref, ldt_ref, arr_ref, air_ref, d_ref,
                     y_ref, nr_ref, ni_ref):
    p = S5_STATE
    dt = jnp.exp(ldt_ref[0])
    mag = jnp.exp(arr_ref[0][:, :p] * dt)
    ang = air_ref[0][:, :p] * dt
    abr, abi = mag * jnp.cos(ang), mag * jnp.sin(ang)
    u = u_ref[0].astype(F32)
    z = _dot(u.astype(BF16), bbt_ref[0].astype(BF16))
    xr, xi = xr_ref[0], xi_ref[0]
    new_r = abr * xr - abi * xi + z[:, :p]
    new_i = abr * xi + abi * xr + z[:, p:]
    nr_ref[0] = new_r
    ni_ref[0] = new_i
    c = w0_ref[0][:, :S5_GROUP].astype(BF16)
    y = _dot(new_r.astype(BF16), c[:p]) + _dot(new_i.astype(BF16), c[p:]) + d_ref[0] * u
    y_ref[0] = jax.nn.gelu(y)


def s5_sample(u, x_re, x_im, bbt, w0, log_dt, a_re, a_im, d_skip):
    g, n, p = S5_GROUPS, S5_GROUP, S5_STATE
    bsz = u.shape[0]
    dup = lambda a: jnp.concatenate([a, a], axis=-1)[:, None, :]
    gspec = lambda *shape: pl.BlockSpec((GROUPS_PER_BLOCK,) + shape, lambda i: (i, 0, 0))
    y, new_r, new_i = pl.pallas_call(
        _s5_sample_kernel,
        grid=(g // GROUPS_PER_BLOCK,),
        in_specs=[gspec(bsz, n), gspec(bsz, p), gspec(bsz, p), gspec(n, 2 * p),
                  gspec(2 * p, S5_CHUNK * n), gspec(1, 1), gspec(1, 2 * p), gspec(1, 2 * p),
                  gspec(1, n)],
        out_specs=[gspec(bsz, n), gspec(bsz, p), gspec(bsz, p)],
        out_shape=[jax.ShapeDtypeStruct((g, bsz, n), F32),
                   jax.ShapeDtypeStruct((g, bsz, p), F32),
                   jax.ShapeDtypeStruct((g, bsz, p), F32)],
        compiler_params=_params(1),
        name="s5_sample",
    )(u.reshape(bsz, g, n).transpose(1, 0, 2), x_re.transpose(1, 0, 2), x_im.transpose(1, 0, 2),
      bbt, w0, log_dt[:, None, None], dup(a_re), dup(a_im), d_skip.reshape(g, 1, n))
    return (y.transpose(1, 0, 2).reshape(bsz, g * n), new_r.transpose(1, 0, 2),
            new_i.transpose(1, 0, 2))


FACTORED_DECAY_LIMIT = -60.0


def _gla_kernel(q_ref, k_ref, v_ref, gr_ref, g_ref, nrm_ref, o_ref, st_ref, s_ref, b_ref,
                qb_ref, kb_ref, kend_ref, sc_ref, acc_ref):
    t = pl.program_id(0)

    @pl.when(t == 0)
    def _():
        s_ref[...] = jnp.zeros(s_ref.shape, F32)

    bsz, c = q_ref.shape[0], q_ref.shape[1]
    row = lax.broadcasted_iota(jnp.int32, (c, c), 0)
    col = lax.broadcasted_iota(jnp.int32, (c, c), 1)
    causal = col <= row
    ones_tri = causal.astype(F32)
    step = lax.broadcasted_iota(jnp.int32, (c, GLA_DK), 0)

    lowest = None
    for i in range(bsz):
        b_ref[i] = jnp.dot(ones_tri, g_ref[i], precision=HIGHEST, preferred_element_type=F32)
        end_min = jnp.min(b_ref[i, c - 1:c, :])
        lowest = end_min if lowest is None else jnp.minimum(lowest, end_min)
    factorable = lowest >= FACTORED_DECAY_LIMIT

    pairs = [(i, h) for i in range(bsz) for h in range(GLA_HEADS)]
    head_k = lambda h: slice(h * GLA_DK, (h + 1) * GLA_DK)
    head_v = lambda h: slice(h * GLA_DV, (h + 1) * GLA_DV)

    def finish(i, h, o):
        o = o * lax.rsqrt(jnp.mean(o * o, axis=-1, keepdims=True) + EPS) * nrm_ref[...]
        o_ref[i, :, head_v(h)] = (o * _silu(gr_ref[i, :, head_v(h)].astype(F32))).astype(o_ref.dtype)

    def all_chunks_factored():
        for n, (i, h) in enumerate(pairs):
            b = b_ref[i, :, head_k(h)]
            k = k_ref[i, :, head_k(h)].astype(F32)
            qb_ref[n] = (q_ref[i, :, head_k(h)].astype(F32) * (GLA_DK ** -0.5) * jnp.exp(b)).astype(BF16)
            kb_ref[n] = (k * jnp.exp(-b)).astype(BF16)
            kend_ref[n] = (k * jnp.exp(b[c - 1:c, :] - b)).astype(BF16)
        for n, (i, h) in enumerate(pairs):
            scores = lax.dot_general(qb_ref[n], kb_ref[n], NT, preferred_element_type=F32)
            sc_ref[n] = jnp.where(causal, scores, 0.0).astype(BF16)
        for n, (i, h) in enumerate(pairs):
            v = v_ref[i, :, head_v(h)].astype(BF16)
            s_t = s_ref[i, h]
            acc_ref[n] = _dot(sc_ref[n], v) + lax.dot_general(qb_ref[n], s_t.astype(BF16), NT,
                                                              preferred_element_type=F32)
            s_ref[i, h] = (jnp.exp(b_ref[i, c - 1:c, head_k(h)]) * s_t
                           + lax.dot_general(v, kend_ref[n], TN, preferred_element_type=F32))
        for n, (i, h) in enumerate(pairs):
            finish(i, h, acc_ref[n])

    def intra_stepwise(i, ks, vs, b, q_in, v):
        q = q_ref[i, :, ks].astype(F32) * (GLA_DK ** -0.5)

        def add_sources(tile, acc):
            first = pl.multiple_of(tile * ROW_TILE, ROW_TILE)
            rows = pl.ds(first, ROW_TILE)
            b_s = b_ref[i, rows, ks]
            k_s, v_s = k_ref[i, rows, ks].astype(F32), v_ref[i, rows, vs].astype(F32)
            for r in range(ROW_TILE):
                decay = jnp.exp(jnp.where(step >= first + r, b - b_s[r:r + 1], -jnp.inf))
                score = jnp.sum(q * k_s[r:r + 1] * decay, axis=1, keepdims=True)
                acc = acc + score * v_s[r:r + 1]
            return acc

        return lax.fori_loop(0, c // ROW_TILE, add_sources, jnp.zeros((c, GLA_DV), F32))

    def all_chunks_stepwise():
        for i, h in pairs:
            ks, vs = head_k(h), head_v(h)
            b = b_ref[i, :, ks]
            b_end = b[c - 1:c, :]
            q_in = (q_ref[i, :, ks].astype(F32) * (GLA_DK ** -0.5) * jnp.exp(b)).astype(BF16)
            k_end = (k_ref[i, :, ks].astype(F32) * jnp.exp(b_end - b)).astype(BF16)
            v = v_ref[i, :, vs].astype(BF16)
            s_t = s_ref[i, h]
            o = (intra_stepwise(i, ks, vs, b, q_in, v)
                 + lax.dot_general(q_in, s_t.astype(BF16), NT, preferred_element_type=F32))
            s_ref[i, h] = jnp.exp(b_end) * s_t + lax.dot_general(v, k_end, TN,
                                                                 preferred_element_type=F32)
            finish(i, h, o)

    pl.when(factorable)(all_chunks_factored)
    pl.when(jnp.logical_not(factorable))(all_chunks_stepwise)

    @pl.when(t == pl.num_programs(0) - 1)
    def _():
        st_ref[...] = s_ref[...]


def gla_prompt(proj, g, gla_norm, bsz, length):
    c = CHUNK
    nc = length // c
    hk, hv = GLA_HEADS * GLA_DK, GLA_HEADS * GLA_DV
    rows = lambda width, off: pl.BlockSpec((bsz, c, width), lambda t: (0, t, off // width))
    st_shape = (bsz, GLA_HEADS, GLA_DV, GLA_DK)
    proj3 = proj.reshape(bsz, length, proj.shape[1])
    o, st = pl.pallas_call(
        _gla_kernel,
        grid=(nc,),
        in_specs=[rows(hk, OFF_GQ), rows(hk, OFF_GK), rows(hv, OFF_GV), rows(hv, OFF_GR),
                  rows(hk, 0), pl.BlockSpec((1, GLA_DV), lambda t: (0, 0))],
        out_specs=[rows(hv, 0), pl.BlockSpec(st_shape, lambda t: (0, 0, 0, 0))],
        out_shape=[jax.ShapeDtypeStruct((bsz, length, hv), BF16),
                   jax.ShapeDtypeStruct(st_shape, F32)],
        scratch_shapes=[pltpu.VMEM(st_shape, F32),
                        pltpu.VMEM((bsz, c, hk), F32)]
        + [pltpu.VMEM((bsz * GLA_HEADS, c, GLA_DK), BF16)] * 3
        + [pltpu.VMEM((bsz * GLA_HEADS, c, c), BF16),
           pltpu.VMEM((bsz * GLA_HEADS, c, GLA_DV), F32)],
        compiler_params=_params(1),
        name="gla_prompt",
    )(proj3, proj3, proj3, proj3, g.reshape(bsz, length, hk), gla_norm.reshape(1, GLA_DV))
    return o.reshape(bsz * length, hv), st.swapaxes(2, 3)


def _pair_swap_matrix(n):
    src = lax.broadcasted_iota(jnp.int32, (n, n), 0)
    dst = lax.broadcasted_iota(jnp.int32, (n, n), 1)
    return (src == jnp.bitwise_xor(dst, 1)).astype(BF16)


def _rotate_pairs(x, cos2, sin2, swap):
    return x.astype(F32) * cos2 + _dot(x, swap) * sin2


def _ret_kernel(q_ref, k_ref, v_ref, rg_ref, cos_ref, sin_ref, dm_ref, qd_ref, kd_ref, gc_ref,
                o_ref, st_ref, s_ref, qb_ref, kb_ref, qdec_ref, kdec_ref, sc_ref, acc_ref):
    t = pl.program_id(0)

    @pl.when(t == 0)
    def _():
        s_ref[...] = jnp.zeros(s_ref.shape, F32)

    cos2, sin2, swap = cos_ref[...], sin_ref[...], _pair_swap_matrix(RET_DK)
    pairs = [(i, h) for i in range(q_ref.shape[0]) for h in range(RET_HEADS)]
    head_k = lambda h: slice(h * RET_DK, (h + 1) * RET_DK)
    head_v = lambda h: slice(h * RET_DV, (h + 1) * RET_DV)
    for n, (i, h) in enumerate(pairs):
        q = _rotate_pairs(q_ref[i, :, head_k(h)], cos2, sin2, swap)
        k = _rotate_pairs(k_ref[i, :, head_k(h)], cos2, sin2, swap) * (RET_DK ** -0.5)
        qb_ref[n] = q.astype(BF16)
        kb_ref[n] = k.astype(BF16)
        qdec_ref[n] = (q * qd_ref[h]).astype(BF16)
        kdec_ref[n] = (k * kd_ref[h]).astype(BF16)
    for n, (i, h) in enumerate(pairs):
        scores = lax.dot_general(qb_ref[n], kb_ref[n], NT, preferred_element_type=F32) * dm_ref[h]
        sc_ref[n] = scores.astype(BF16)
    for n, (i, h) in enumerate(pairs):
        v = v_ref[i, :, head_v(h)].astype(BF16)
        s_t = s_ref[i, h]
        acc_ref[n] = _dot(sc_ref[n], v) + lax.dot_general(qdec_ref[n], s_t.astype(BF16), NT,
                                                          preferred_element_type=F32)
        s_ref[i, h] = gc_ref[h] * s_t + lax.dot_general(v, kdec_ref[n], TN, preferred_element_type=F32)
    for n, (i, h) in enumerate(pairs):
        o = acc_ref[n]
        oc = o - jnp.mean(o, axis=-1, keepdims=True)
        o = oc * lax.rsqrt(jnp.mean(oc * oc, axis=-1, keepdims=True) + EPS)
        o_ref[i, :, head_v(h)] = (o * _silu(rg_ref[i, :, head_v(h)].astype(F32))).astype(o_ref.dtype)

    @pl.when(t == pl.num_programs(0) - 1)
    def _():
        st_ref[...] = s_ref[...]


def _rotary_tables(pos):
    half = RET_DK // 2
    inv = 1.0 / (ROPE_BASE ** jnp.linspace(0.0, 1.0, half, dtype=F32))
    ang = pos[:, None] * inv[None, :]
    cos2 = jnp.repeat(jnp.cos(ang), 2, axis=-1)
    sin2 = jnp.stack([-jnp.sin(ang), jnp.sin(ang)], axis=-1).reshape(pos.shape[0], RET_DK)
    return cos2, sin2


def _log_gamma():
    return jnp.log1p(-jnp.power(2.0, -5.0 - jnp.arange(RET_HEADS, dtype=F32)))


def ret_prompt(proj, bsz, length):
    c = CHUNK
    nc = length // c
    hk, hv = RET_HEADS * RET_DK, RET_HEADS * RET_DV
    cos2, sin2 = _rotary_tables(jnp.arange(length, dtype=F32))
    lg = _log_gamma()[:, None, None]
    steps = jnp.arange(c, dtype=F32)
    lag = steps[:, None] - steps[None, :]
    dmat = jnp.where(lag >= 0, jnp.exp(lg * jnp.maximum(lag, 0.0)), 0.0)
    wide = lambda a: jnp.broadcast_to(a, (RET_HEADS, a.shape[1], LANES))
    qdec = wide(jnp.exp(lg * (steps + 1.0)[None, :, None]))
    kdec = wide(jnp.exp(lg * (c - 1.0 - steps)[None, :, None]))
    gchunk = wide(jnp.exp(lg * float(c)))
    rows = lambda width, off: pl.BlockSpec((bsz, c, width), lambda t: (0, t, off // width))
    table = pl.BlockSpec((c, RET_DK), lambda t: (t, 0))
    const = lambda a: pl.BlockSpec(a.shape, lambda t: (0, 0, 0))
    st_shape = (bsz, RET_HEADS, RET_DV, RET_DK)
    proj3 = proj.reshape(bsz, length, proj.shape[1])
    o, st = pl.pallas_call(
        _ret_kernel,
        grid=(nc,),
        in_specs=[rows(hk, OFF_RQ), rows(hk, OFF_RK), rows(hv, OFF_RV), rows(hv, OFF_RG),
                  table, table, const(dmat), const(qdec), const(kdec), const(gchunk)],
        out_specs=[rows(hv, 0), pl.BlockSpec(st_shape, lambda t: (0, 0, 0, 0))],
        out_shape=[jax.ShapeDtypeStruct((bsz, length, hv), BF16),
                   jax.ShapeDtypeStruct(st_shape, F32)],
        scratch_shapes=[pltpu.VMEM(st_shape, F32)]
        + [pltpu.VMEM((bsz * RET_HEADS, c, RET_DK), BF16)] * 4
        + [pltpu.VMEM((bsz * RET_HEADS, c, c), BF16),
           pltpu.VMEM((bsz * RET_HEADS, c, RET_DV), F32)],
        compiler_params=_params(1),
        name="ret_prompt",
    )(proj3, proj3, proj3, proj3, cos2, sin2, dmat, qdec, kdec, gchunk)
    return o.reshape(bsz * length, hv), st.swapaxes(2, 3)


def _one_column(a_t, i):
    col = lax.broadcasted_iota(jnp.int32, a_t.shape, 1)
    return jnp.where(col == i, a_t, 0.0).astype(BF16)


def _gla_sample_kernel(s_ref, q_ref, kt_ref, gt_ref, v_ref, gr_ref, nrm_ref, *rest):
    so_ref, o_ref, acc_ref = rest[-3:]
    ones = jnp.ones((SAMPLE_BLOCK, GLA_DV), BF16)
    head_k = lambda h: slice(h * GLA_DK, (h + 1) * GLA_DK)
    head_v = lambda h: slice(h * GLA_DV, (h + 1) * GLA_DV)
    for h in range(GLA_HEADS):
        v = v_ref[:, head_v(h)].astype(BF16)
        k_t = kt_ref[0, h]
        decay_t = jnp.exp(gt_ref[0, h])
        decay_hi = decay_t.astype(BF16).astype(F32)
        decay_lo = decay_t - decay_hi
        for i in range(SAMPLE_BLOCK):
            decay = _dot(_one_column(decay_hi, i), ones) + _dot(_one_column(decay_lo, i), ones)
            so_ref[i, h] = decay * s_ref[i, h] + _dot(_one_column(k_t, i), v)
    for h in range(GLA_HEADS):
        q = (q_ref[:, head_k(h)].astype(F32) * (GLA_DK ** -0.5)).astype(BF16)
        for i in range(SAMPLE_BLOCK):
            acc_ref[i:i + 1, head_v(h)] = _dot(q, so_ref[i, h].astype(BF16))[i:i + 1]
    for h in range(GLA_HEADS):
        o = acc_ref[:, head_v(h)]
        o = o * lax.rsqrt(jnp.mean(o * o, axis=-1, keepdims=True) + EPS) * nrm_ref[...]
        o_ref[:, head_v(h)] = (o * _silu(gr_ref[:, head_v(h)].astype(F32))).astype(o_ref.dtype)


def _columns(a, heads, dk):
    bsz = a.shape[0]
    return a.reshape(bsz // SAMPLE_BLOCK, SAMPLE_BLOCK, heads, dk).transpose(0, 2, 3, 1)


def _stacked_state_call(kernel_fn, name, state, layer, prev, heads, dk, dv, in_specs, operands,
                        o_spec, o_shape):
    bb = SAMPLE_BLOCK
    st_spec = pl.BlockSpec((None, bb, heads, dk, dv), lambda i: (layer, i, 0, 0, 0))
    aliases = {}
    if prev is not None:
        in_specs = in_specs + [pl.BlockSpec(memory_space=pl.ANY)]
        operands = operands + (prev,)
        aliases = {len(in_specs): 0}
    return pl.pallas_call(
        kernel_fn,
        grid=(state.shape[1] // bb,),
        in_specs=[st_spec] + in_specs,
        out_specs=[st_spec, o_spec],
        out_shape=[jax.ShapeDtypeStruct(state.shape, F32), o_shape],
        scratch_shapes=[pltpu.VMEM((bb, heads * dv), F32)],
        input_output_aliases=aliases,
        compiler_params=_params(1),
        name=name,
    )(state, *operands)


def gla_sample(proj, g, state, gla_norm, layer, prev):
    bsz = proj.shape[0]
    bb = SAMPLE_BLOCK
    hk, hv = GLA_HEADS * GLA_DK, GLA_HEADS * GLA_DV
    col_spec = pl.BlockSpec((1, GLA_HEADS, GLA_DK, bb), lambda i: (i, 0, 0, 0))
    rows = lambda off: pl.BlockSpec((bb, hv), lambda i: (i, off // hv))
    return _stacked_state_call(
        _gla_sample_kernel, "gla_sample", state, layer, prev, GLA_HEADS, GLA_DK, GLA_DV,
        [pl.BlockSpec((bb, hk), lambda i: (i, OFF_GQ // hk)), col_spec, col_spec, rows(OFF_GV),
         rows(OFF_GR), pl.BlockSpec((None, 1, GLA_DV), lambda i: (layer, 0, 0))],
        (proj, _columns(proj[:, OFF_GK:OFF_GK + hk], GLA_HEADS, GLA_DK), _columns(g, GLA_HEADS, GLA_DK),
         proj, proj, gla_norm.reshape(gla_norm.shape[0], 1, GLA_DV)),
        rows(0), jax.ShapeDtypeStruct((bsz, hv), BF16))


def _rot_kernel(q_ref, k_ref, cos_ref, sin_ref, qo_ref, ko_ref):
    cos2, sin2, swap = cos_ref[...], sin_ref[...], _pair_swap_matrix(RET_DK)
    for h in range(RET_HEADS):
        ks = slice(h * RET_DK, (h + 1) * RET_DK)
        qo_ref[:, ks] = _rotate_pairs(q_ref[:, ks], cos2, sin2, swap)
        ko_ref[:, ks] = _rotate_pairs(k_ref[:, ks], cos2, sin2, swap) * (RET_DK ** -0.5)


def _ret_sample_kernel(s_ref, q_ref, kt_ref, gam_ref, v_ref, rg_ref, *rest):
    so_ref, o_ref, acc_ref = rest[-3:]
    head_k = lambda h: slice(h * RET_DK, (h + 1) * RET_DK)
    head_v = lambda h: slice(h * RET_DV, (h + 1) * RET_DV)
    for h in range(RET_HEADS):
        v = v_ref[:, head_v(h)].astype(BF16)
        k_t = kt_ref[0, h]
        for i in range(SAMPLE_BLOCK):
            so_ref[i, h] = gam_ref[h] * s_ref[i, h] + _dot(_one_column(k_t, i), v)
    for h in range(RET_HEADS):
        q = q_ref[:, head_k(h)].astype(BF16)
        for i in range(SAMPLE_BLOCK):
            acc_ref[i:i + 1, head_v(h)] = _dot(q, so_ref[i, h].astype(BF16))[i:i + 1]
    for h in range(RET_HEADS):
        o = acc_ref[:, head_v(h)]
        oc = o - jnp.mean(o, axis=-1, keepdims=True)
        o = oc * lax.rsqrt(jnp.mean(oc * oc, axis=-1, keepdims=True) + EPS)
        o_ref[:, head_v(h)] = (o * _silu(rg_ref[:, head_v(h)].astype(F32))).astype(o_ref.dtype)


def ret_sample(proj, state, layer, prev):
    bsz = proj.shape[0]
    bb = SAMPLE_BLOCK
    hk, hv = RET_HEADS * RET_DK, RET_HEADS * RET_DV
    cos2, sin2 = _rotary_tables(PAST_LEN + jnp.arange(1, dtype=F32))
    full = pl.BlockSpec((bsz, hk), lambda i: (0, 0))
    q_rot, k_rot = pl.pallas_call(
        _rot_kernel,
        grid=(1,),
        in_specs=[pl.BlockSpec((bsz, hk), lambda i: (0, OFF_RQ // hk)),
                  pl.BlockSpec((bsz, hk), lambda i: (0, OFF_RK // hk)),
                  pl.BlockSpec((1, RET_DK), lambda i: (0, 0)), pl.BlockSpec((1, RET_DK), lambda i: (0, 0))],
        out_specs=[full, full],
        out_shape=[jax.ShapeDtypeStruct((bsz, hk), F32)] * 2,
        compiler_params=_params(1),
        name="ret_rotary",
    )(proj, proj, cos2, sin2)
    gamma = jnp.broadcast_to(jnp.exp(_log_gamma())[:, None, None], (RET_HEADS, 1, LANES))
    col_spec = pl.BlockSpec((1, RET_HEADS, RET_DK, bb), lambda i: (i, 0, 0, 0))
    rows = lambda off: pl.BlockSpec((bb, hv), lambda i: (i, off // hv))
    return _stacked_state_call(
        _ret_sample_kernel, "ret_sample", state, layer, prev, RET_HEADS, RET_DK, RET_DV,
        [rows(0), col_spec, pl.BlockSpec(gamma.shape, lambda i: (0, 0, 0)), rows(OFF_RV), rows(OFF_RG)],
        (q_rot, _columns(k_rot, RET_HEADS, RET_DK), gamma, proj, proj),
        rows(0), jax.ShapeDtypeStruct((bsz, hv), BF16))


def _layer(xp, xs, layer, prompt_shape, states, w, s5_mats):
    t_mat, p_mat, q_mat, bbt, w0 = s5_mats
    bsz, length = prompt_shape
    mp = xp.shape[0]
    tm, tm_wide, tn = min(mp, 512), min(mp, 1024), 512
    s5_par = (w["s5_log_dt"][layer], w["s5_a_re"][layer], w["s5_a_im"][layer], w["s5_d"][layer])
    st_s5_re, st_s5_im, st_gla, st_ret, prev_gla, prev_ret = states

    gate_w = (w["w_in_rows"], w["gla_w_gate"], w["gla_b_gate"], layer)
    hp, g_p = norm_and_gate(xp, w["norm_mix"][layer], *gate_w)
    hs, g_s = norm_and_gate(xs, w["norm_mix"][layer], *gate_w)
    proj_p, proj_s = in_proj(hp, hs, w["w_in_rows"], layer, tm_wide, 2 * tn)

    y5_p, x_fin = s5_prompt(proj_p[:, OFF_U:OFF_U + S5_WIDTH], bsz, length, t_mat, p_mat, q_mat, *s5_par)
    y5_p = y5_p.reshape(mp, S5_WIDTH)
    x_fin = x_fin.transpose(1, 0, 2)
    o_gla_p, new_gla_p = gla_prompt(proj_p, g_p, w["gla_norm"][layer], bsz, length)
    o_ret_p, new_ret_p = ret_prompt(proj_p, bsz, length)

    y5_s, new_s5_re_s, new_s5_im_s = s5_sample(proj_s[:, OFF_U:OFF_U + S5_WIDTH], st_s5_re[layer],
                                               st_s5_im[layer], bbt, w0, *s5_par)
    new_gla_s, o_gla_s = gla_sample(proj_s, g_s, st_gla, w["gla_norm"], layer, prev_gla)
    new_ret_s, o_ret_s = ret_sample(proj_s, st_ret, layer, prev_ret)

    a5_p, a5_s = s5_glu([y5_p, y5_s], w["s5_w_glu"], w["s5_b_glu"], layer, tm, 2 * tn)
    a5_p = a5_p.reshape(S5_CHUNK, mp // S5_CHUNK, S5_WIDTH).swapaxes(0, 1).reshape(mp, S5_WIDTH)
    merged = merge_branches([(a5_p, o_gla_p, o_ret_p), (a5_s, o_gla_s, o_ret_s)], [proj_p, proj_s],
                            w["s5_w_out"], w["gla_w_out"], w["ret_w_out"], layer, tm, 2 * tn)
    xp, xs = residual_matmul(merged, w["w_mix_out"], [xp, xs], layer, tm_wide, 2 * tn)
    h2 = [rmsnorm(xp, w["norm_ffn"][layer], BF16), rmsnorm(xs, w["norm_ffn"][layer], BF16)]
    ff = ffn_up(h2, w["w_ffn_gate"], w["w_ffn_up"], layer, tm_wide, tn)
    xp, xs = residual_matmul(ff, w["w_ffn_down"], [xp, xs], layer, tm_wide, tn)
    new_p = (x_fin[..., :S5_STATE], x_fin[..., S5_STATE:], new_gla_p, new_ret_p)
    return xp, xs, new_p, (new_s5_re_s, new_s5_im_s, new_gla_s, new_ret_s)


def kernel(x_prompt, x_sample, state_s5_re, state_s5_im, state_gla, state_ret, norm_mix, w_in, s5_a_re, s5_a_im, s5_log_dt, s5_b_re, s5_b_im, s5_c_re, s5_c_im, s5_d, s5_w_glu, s5_b_glu, s5_w_out, gla_w_gate, gla_b_gate, gla_norm, gla_w_out, ret_w_out, w_mix_out, norm_ffn, w_ffn_gate, w_ffn_up, w_ffn_down, norm_final):
    bp, lp, d = x_prompt.shape
    bs = x_sample.shape[0]
    depth = w_in.shape[0]
    hp = x_prompt.reshape(bp * lp, d)
    hs = x_sample.reshape(bs, d)
    w = dict(norm_mix=norm_mix, s5_log_dt=s5_log_dt, s5_a_re=s5_a_re, s5_a_im=s5_a_im, s5_d=s5_d,
             s5_w_glu=s5_w_glu, s5_b_glu=s5_b_glu, s5_w_out=s5_w_out, gla_w_gate=gla_w_gate,
             gla_b_gate=gla_b_gate, gla_norm=gla_norm, gla_w_out=gla_w_out, ret_w_out=ret_w_out,
             w_mix_out=w_mix_out, norm_ffn=norm_ffn, w_ffn_gate=w_ffn_gate, w_ffn_up=w_ffn_up,
             w_ffn_down=w_ffn_down,
             w_in_rows=jnp.swapaxes(w_in, 1, 2).reshape(depth, w_in.shape[2] // GLA_RANK, GLA_RANK, d))
    outs_p, outs_s = [], []
    new_gla_s = new_ret_s = None
    for l in range(depth):
        s5_mats = s5_prep(s5_a_re[l], s5_a_im[l], s5_log_dt[l], s5_b_re[l], s5_b_im[l], s5_c_re[l],
                          s5_c_im[l])
        hp, hs, new_p, (s5r, s5i, new_gla_s, new_ret_s) = _layer(
            hp, hs, l, (bp, lp), (state_s5_re, state_s5_im, state_gla, state_ret, new_gla_s, new_ret_s),
            w, s5_mats)
        outs_p.append(new_p)
        outs_s.append((s5r, s5i))
    y_prompt = rmsnorm(hp, norm_final, F32).reshape(bp, lp, d)
    y_sample = rmsnorm(hs, norm_final, F32).reshape(bs, 1, d)
    stack = lambda outs, i: jnp.stack([o[i] for o in outs])
    return (y_prompt, y_sample,
            stack(outs_p, 0), stack(outs_p, 1), stack(outs_p, 2), stack(outs_p, 3),
            stack(outs_s, 0), stack(outs_s, 1), new_gla_s, new_ret_s)
```

```python
import functools
import math

import jax
import jax.numpy as jnp
import numpy as np
from jax import lax
from jax.experimental import pallas as pl
from jax.experimental.pallas import tpu as pltpu

F32 = jnp.float32
BF16 = jnp.bfloat16
HIGHEST = lax.Precision.HIGHEST

D_MODEL = 2048
S5_WIDTH = D_MODEL // 2
S5_GROUP = 16
S5_GROUPS = S5_WIDTH // S5_GROUP
S5_STATE = 64
S5_CHUNK = 16
GLA_HEADS = 4
GLA_DK = D_MODEL // 16
GLA_DV = D_MODEL // 8
GLA_RANK = 16
GLA_GATE_NORM = 16.0
RET_HEADS = 8
RET_DK = D_MODEL // 16
RET_DV = D_MODEL // 16
ROPE_BASE = 10000.0
PAST_LEN = 16384
N_BRANCH = 3
D_FF = -(-8 * D_MODEL // (3 * 256)) * 256
EPS = 1e-6
CHUNK = 128
LANES = 128
ROW_TILE = 16
SAMPLE_BLOCK = 16

OFF_U = 0
OFF_GQ = OFF_U + S5_WIDTH
OFF_GK = OFF_GQ + GLA_HEADS * GLA_DK
OFF_GV = OFF_GK + GLA_HEADS * GLA_DK
OFF_GR = OFF_GV + GLA_HEADS * GLA_DV
OFF_RQ = OFF_GR + GLA_HEADS * GLA_DV
OFF_RK = OFF_RQ + RET_HEADS * RET_DK
OFF_RV = OFF_RK + RET_HEADS * RET_DK
OFF_RG = OFF_RV + RET_HEADS * RET_DV
OFF_MG = OFF_RG + RET_HEADS * RET_DV
W_MAIN = OFF_MG + N_BRANCH * D_MODEL
GLR_COL = OFF_RQ

VMEM_LIMIT = 56 * 1024 * 1024

NT = (((1,), (1,)), ((), ()))
TN = (((0,), (0,)), ((), ()))


def _params(n_grid):
    return pltpu.CompilerParams(dimension_semantics=("arbitrary",) * n_grid,
                                vmem_limit_bytes=VMEM_LIMIT)


def _silu(x):
    return x * jax.nn.sigmoid(x)


def _rmsnorm_kernel(x_ref, g_ref, o_ref):
    x = x_ref[...]
    y = x * lax.rsqrt(jnp.mean(x * x, axis=-1, keepdims=True) + EPS)
    o_ref[...] = (y * g_ref[...]).astype(o_ref.dtype)


def rmsnorm(x, gain, out_dtype):
    m, d = x.shape
    tm = min(m, 512)
    return pl.pallas_call(
        _rmsnorm_kernel,
        grid=(m // tm,),
        in_specs=[pl.BlockSpec((tm, d), lambda i: (i, 0)),
                  pl.BlockSpec((1, d), lambda i: (0, 0))],
        out_specs=pl.BlockSpec((tm, d), lambda i: (i, 0)),
        out_shape=jax.ShapeDtypeStruct((m, d), out_dtype),
        compiler_params=_params(1),
        name="rmsnorm",
    )(x, gain.reshape(1, d))


def _dot(a, b):
    return jnp.dot(a, b, preferred_element_type=F32)


def _in_proj_kernel(a_ref, as_ref, w_ref, o_ref, os_ref, wb_ref):
    i = pl.program_id(1)

    @pl.when(i == 0)
    def _():
        wb_ref[...] = w_ref[...].reshape(wb_ref.shape).astype(BF16)

    def project(src_ref, dst_ref):
        dst_ref[...] = lax.dot_general(src_ref[...], wb_ref[...], NT,
                                       preferred_element_type=F32).astype(dst_ref.dtype)

    pl.when(i == 0)(lambda: project(as_ref, os_ref))
    pl.when(i > 0)(lambda: project(a_ref, o_ref))


def in_proj(h, h_s, w_rows, layer, tm, tn):
    m, k = h.shape
    ms = h_s.shape[0]
    nb = m // tm
    rb = tn // GLA_RANK
    skip_from = GLR_COL // tn

    def w_index(j, i):
        return (layer, rb * j + jnp.where(j >= skip_from, 1, 0), 0, 0)

    return pl.pallas_call(
        _in_proj_kernel,
        grid=(W_MAIN // tn, nb + 1),
        in_specs=[pl.BlockSpec((tm, k), lambda j, i: (jnp.maximum(i - 1, 0), 0)),
                  pl.BlockSpec((ms, k), lambda j, i: (0, 0)),
                  pl.BlockSpec((pl.Element(1), pl.Element(rb), pl.Element(GLA_RANK), pl.Element(k)),
                               w_index)],
        out_specs=[pl.BlockSpec((tm, tn), lambda j, i: (jnp.maximum(i - 1, 0), j)),
                   pl.BlockSpec((ms, tn), lambda j, i: (0, j))],
        out_shape=[jax.ShapeDtypeStruct((m, W_MAIN), BF16), jax.ShapeDtypeStruct((ms, W_MAIN), BF16)],
        scratch_shapes=[pltpu.VMEM((tn, k), BF16)],
        compiler_params=_params(2),
        name="in_proj",
    )(h, h_s, w_rows)


def _fused_matmul(name, body, streams, weights, biases, layer, n, tm, tn, out_dtype):
    nb = streams[0][0][0].shape[0] // tm
    extra = len(streams) - 1
    specs, operands, counts, out_specs, out_shapes = [], [], [], [], []
    for s, (row_ops, tile_ops) in enumerate(streams):
        m_s = row_ops[0].shape[0]
        rows = tm if s == 0 else m_s
        row_block = (lambda i: jnp.maximum(i - extra, 0)) if s == 0 else (lambda i: 0)
        for a in row_ops:
            specs.append(pl.BlockSpec((rows, a.shape[1]), lambda j, i, r=row_block: (r(i), 0)))
            operands.append(a)
        for a, first in tile_ops:
            specs.append(pl.BlockSpec((rows, tn), lambda j, i, r=row_block, c=first // tn: (r(i), c + j)))
            operands.append(a)
        counts.append((len(row_ops), len(tile_ops)))
        out_specs.append(pl.BlockSpec((rows, tn), lambda j, i, r=row_block: (r(i), j)))
        out_shapes.append(jax.ShapeDtypeStruct((m_s, n), out_dtype))
    weight_bytes = sum(wt.shape[1] * tn * 4 for wt in weights)
    weight_buffers = 2 if 2 * weight_bytes <= VMEM_LIMIT // 3 else 1
    for wt in weights:
        specs.append(pl.BlockSpec((None, wt.shape[1], tn), lambda j, i: (layer, 0, j),
                                  pipeline_mode=pl.Buffered(weight_buffers)))
        operands.append(wt)
    for b in biases:
        specs.append(pl.BlockSpec((None, 1, tn), lambda j, i: (layer, 0, j)))
        operands.append(b)

    def kernel_fn(*refs):
        refs = list(refs)
        stream_refs = [(tuple(refs.pop(0) for _ in range(nr)), tuple(refs.pop(0) for _ in range(nt)))
                       for nr, nt in counts]
        w_refs = [refs.pop(0) for _ in weights]
        b_refs = [refs.pop(0) for _ in biases]
        o_refs = [refs.pop(0) for _ in streams]
        wb_refs = refs
        i = pl.program_id(1)

        @pl.when(i == 0)
        def _():
            for w_ref, wb_ref in zip(w_refs, wb_refs):
                wb_ref[...] = w_ref[...].astype(BF16)

        def run(s):
            row_refs, tile_refs = stream_refs[s]
            out = body(*[r[...] for r in row_refs], *[r[...] for r in wb_refs],
                       *[r[...] for r in tile_refs], *[r[...] for r in b_refs])
            o_refs[s][...] = out.astype(out_dtype)

        for s in range(1, len(streams)):
            pl.when(i == s - 1)(functools.partial(run, s))
        pl.when(i >= extra)(functools.partial(run, 0))

    return pl.pallas_call(
        kernel_fn,
        grid=(n // tn, nb + len(streams) - 1),
        in_specs=specs,
        out_specs=out_specs,
        out_shape=out_shapes,
        scratch_shapes=[pltpu.VMEM((wt.shape[1], tn), BF16) for wt in weights],
        compiler_params=_params(2),
        name=name,
    )(*operands)


def _norm_gate_kernel(x_ref, gain_ref, wl_ref, wg_ref, b_ref, h_ref, g_ref):
    x = x_ref[...]
    h = (x * lax.rsqrt(jnp.mean(x * x, axis=-1, keepdims=True) + EPS) * gain_ref[...]).astype(BF16)
    h_ref[...] = h
    pad = LANES - GLA_RANK
    wl = jnp.concatenate([wl_ref[0], jnp.zeros((pad, wl_ref.shape[2]), F32)], axis=0).astype(BF16)
    wg = jnp.concatenate([wg_ref[...], jnp.zeros((pad, wg_ref.shape[1]), F32)], axis=0).astype(BF16)
    glr = lax.dot_general(h, wl, NT, preferred_element_type=F32)
    logit = _dot(glr.astype(BF16), wg) + b_ref[...]
    log_sig = jnp.minimum(logit, 0.0) - jnp.log1p(jnp.exp(-jnp.abs(logit)))
    g_ref[...] = log_sig / GLA_GATE_NORM


def norm_and_gate(x, gain, w_rows, w_gate, b_gate, layer):
    m, k = x.shape
    n = w_gate.shape[2]
    tm = min(m, 512)
    return pl.pallas_call(
        _norm_gate_kernel,
        grid=(m // tm,),
        in_specs=[pl.BlockSpec((tm, k), lambda i: (i, 0)),
                  pl.BlockSpec((1, k), lambda i: (0, 0)),
                  pl.BlockSpec((None, 1, GLA_RANK, k), lambda i: (layer, GLR_COL // GLA_RANK, 0, 0)),
                  pl.BlockSpec((None, GLA_RANK, n), lambda i: (layer, 0, 0)),
                  pl.BlockSpec((None, 1, n), lambda i: (layer, 0, 0))],
        out_specs=[pl.BlockSpec((tm, k), lambda i: (i, 0)), pl.BlockSpec((tm, n), lambda i: (i, 0))],
        out_shape=[jax.ShapeDtypeStruct((m, k), BF16), jax.ShapeDtypeStruct((m, n), F32)],
        compiler_params=_params(1),
        name="norm_and_gate",
    )(x, gain.reshape(1, k), w_rows, w_gate, b_gate.reshape(b_gate.shape[0], 1, n))


def s5_glu(y5_streams, w, b, layer, tm, tn):
    n = w.shape[2]
    body = lambda y, wb, y_tile, bias: y_tile * jax.nn.sigmoid(_dot(y.astype(BF16), wb) + bias)
    return _fused_matmul("s5_glu", body, [([y], [(y, 0)]) for y in y5_streams], [w],
                         [b.reshape(b.shape[0], 1, n)], layer, n, tm, tn, BF16)


def merge_branches(branch_streams, proj_streams, w5, wg, wr, layer, tm, tn):
    def body(a5, ag, ar, w5b, wgb, wrb, m5, mg, mr):
        gate = lambda t: jax.nn.sigmoid(t.astype(F32))
        return gate(m5) * _dot(a5, w5b) + gate(mg) * _dot(ag, wgb) + gate(mr) * _dot(ar, wrb)

    streams = [(list(branches), [(proj, OFF_MG + i * D_MODEL) for i in range(N_BRANCH)])
               for branches, proj in zip(branch_streams, proj_streams)]
    return _fused_matmul("merge_branches", body, streams, [w5, wg, wr], [], layer, w5.shape[2],
                         tm, tn, BF16)


def residual_matmul(a_streams, w, x_streams, layer, tm, tn):
    body = lambda a, wb, x: x + _dot(a, wb)
    return _fused_matmul("residual_matmul", body, [([a], [(x, 0)]) for a, x in zip(a_streams, x_streams)],
                         [w], [], layer, w.shape[2], tm, tn, F32)


def ffn_up(h_streams, wg, wu, layer, tm, tn):
    body = lambda h, wgb, wub: _silu(_dot(h, wgb)) * _dot(h, wub)
    return _fused_matmul("ffn_up", body, [([h], []) for h in h_streams], [wg, wu], [], layer,
                         wg.shape[2], tm, tn, BF16)


def _s5_prep_kernel(*refs):
    for g in range(refs[0].shape[0]):
        _s5_prep_group(*[r.at[g:g + 1] for r in refs])


def _s5_prep_group(ldt_ref, arc_ref, aic_ref, arr_ref, air_ref, b1_ref, crt_ref, cit_ref,
                   t_ref, p_ref, q_ref, bbt_ref, w0_ref):
    s = S5_CHUNK
    dt = jnp.exp(ldt_ref[0])
    ar_row, ai_row = arr_ref[0], air_ref[0]
    lam_r_row, lam_i_row = ar_row * dt, ai_row * dt
    lam_r_col, lam_i_col = arc_ref[0] * dt, aic_ref[0] * dt

    mag = jnp.exp(lam_r_row)
    abr, abi = mag * jnp.cos(lam_i_row), mag * jnp.sin(lam_i_row)
    den = ar_row * ar_row + ai_row * ai_row
    nr = abr - 1.0
    fr = (nr * ar_row + abi * ai_row) / den
    fi = (abi * ar_row - nr * ai_row) / den

    b1 = b1_ref[0]
    lane = lax.broadcasted_iota(jnp.int32, b1.shape, 1)
    b2 = pltpu.roll(b1, S5_STATE, axis=1) * jnp.where(lane < S5_STATE, -1.0, 1.0)
    bbt = fr * b1 + fi * b2
    bbt_sw = fr * b2 - fi * b1
    bbt_ref[0] = bbt

    tau_p = (s - 1 - lax.broadcasted_iota(jnp.int32, (s, LANES), 0)).astype(F32)
    mag_p = jnp.exp(lam_r_row * tau_p)
    ar_p, ai_p = mag_p * jnp.cos(lam_i_row * tau_p), mag_p * jnp.sin(lam_i_row * tau_p)
    repeat_rows = lambda a: jnp.concatenate(
        [jnp.broadcast_to(a[i:i + 1], (S5_GROUP, LANES)) for i in range(s)], axis=0)
    p_ref[0] = (jnp.concatenate([bbt] * s, axis=0) * repeat_rows(ar_p)
                + jnp.concatenate([bbt_sw] * s, axis=0) * repeat_rows(ai_p)).astype(p_ref.dtype)

    crt, cit = crt_ref[0], cit_ref[0]
    power = jnp.minimum(lax.broadcasted_iota(jnp.int32, (S5_STATE, LANES), 1), s).astype(F32)
    mag_c = jnp.exp(lam_r_col * power)
    ar_c, ai_c = mag_c * jnp.cos(lam_i_col * power), mag_c * jnp.sin(lam_i_col * power)
    src = lax.broadcasted_iota(jnp.int32, (LANES, s * S5_GROUP), 0)
    dst_block = jnp.right_shift(lax.broadcasted_iota(jnp.int32, (LANES, s * S5_GROUP), 1), 4)

    def coeffs(tau):
        spread = (src == dst_block + tau).astype(F32)
        ar_t = jnp.dot(ar_c, spread, precision=HIGHEST, preferred_element_type=F32)
        ai_t = jnp.dot(ai_c, spread, precision=HIGHEST, preferred_element_type=F32)
        return jnp.concatenate([ar_t * crt - ai_t * cit, -(ar_t * cit + ai_t * crt)], axis=0)

    w0 = coeffs(0)
    w0_ref[0] = w0
    q_ref[0] = coeffs(1).astype(q_ref.dtype)

    t0 = jnp.dot(bbt, w0, precision=HIGHEST, preferred_element_type=F32)
    lane_t = lax.broadcasted_iota(jnp.int32, t0.shape, 1)
    blocks = [t0]
    for i in range(1, s):
        blocks.append(jnp.where(lane_t >= i * S5_GROUP, pltpu.roll(t0, i * S5_GROUP, axis=1), 0.0))
    t_ref[0] = jnp.concatenate(blocks, axis=0).astype(t_ref.dtype)


def s5_prep(a_re, a_im, log_dt, b_re, b_im, c_re, c_im):
    g, p, n, s = S5_GROUPS, S5_STATE, S5_GROUP, S5_CHUNK
    dup = lambda a: jnp.concatenate([a, a], axis=-1)[:, None, :]
    b1 = jnp.concatenate([b_re.transpose(0, 2, 1), b_im.transpose(0, 2, 1)], axis=-1)
    crt = jnp.tile(c_re.transpose(0, 2, 1), (1, 1, s))
    cit = jnp.tile(c_im.transpose(0, 2, 1), (1, 1, s))
    spec = lambda *shape: pl.BlockSpec((GROUPS_PER_BLOCK,) + shape, lambda i: (i, 0, 0))
    return pl.pallas_call(
        _s5_prep_kernel,
        grid=(g // GROUPS_PER_BLOCK,),
        in_specs=[spec(1, 1), spec(p, 1), spec(p, 1), spec(1, 2 * p), spec(1, 2 * p),
                  spec(n, 2 * p), spec(p, s * n), spec(p, s * n)],
        out_specs=[spec(s * n, s * n), spec(s * n, 2 * p), spec(2 * p, s * n),
                   spec(n, 2 * p), spec(2 * p, s * n)],
        out_shape=[jax.ShapeDtypeStruct((g, s * n, s * n), BF16),
                   jax.ShapeDtypeStruct((g, s * n, 2 * p), BF16),
                   jax.ShapeDtypeStruct((g, 2 * p, s * n), BF16),
                   jax.ShapeDtypeStruct((g, n, 2 * p), F32),
                   jax.ShapeDtypeStruct((g, 2 * p, s * n), F32)],
        compiler_params=_params(1),
        name="s5_prep",
    )(log_dt[:, None, None], a_re[:, :, None], a_im[:, :, None], dup(a_re), dup(a_im), b1, crt, cit)


def _abar_power(ldt_ref, arr_ref, air_ref, power):
    dt = jnp.exp(ldt_ref[...])
    mag = jnp.exp(arr_ref[...] * dt * power)
    ang = air_ref[...] * dt * power
    lane = lax.broadcasted_iota(jnp.int32, mag.shape, 2)
    return mag * jnp.cos(ang), mag * jnp.sin(ang) * jnp.where(lane < S5_STATE, -1.0, 1.0)


GROUPS_PER_BLOCK = LANES // S5_GROUP
BLOCK_TRANSPOSE_ROWS = 128


def _transpose_lane_blocks(xs):
    count = len(xs)
    lane_block = jnp.right_shift(lax.broadcasted_iota(jnp.int32, xs[0].shape, 1), 4)
    dist = count // 2
    while dist:
        upper = jnp.bitwise_and(lane_block, dist) != 0
        nxt = list(xs)
        for i in range(count):
            if not i & dist:
                lo, hi = xs[i], xs[i + dist]
                nxt[i] = jnp.where(upper, pltpu.roll(hi, S5_GROUP * dist, axis=1), lo)
                nxt[i + dist] = jnp.where(upper, hi, pltpu.roll(lo, LANES - S5_GROUP * dist, axis=1))
        xs = nxt
        dist //= 2
    return xs


def _s5_prompt_kernel(u_ref, t_ref, p_ref, q_ref, ldt_ref, arr_ref, air_ref, d_ref,
                      y_ref, fin_ref, ug_ref, z_ref, zsw_ref, x_ref, yg_ref, *, bsz):
    s, gp = S5_CHUNK, GROUPS_PER_BLOCK
    rows = u_ref.shape[1]
    n_chunks = rows // bsz
    row_tiles = [slice(r, r + BLOCK_TRANSPOSE_ROWS) for r in range(0, rows, BLOCK_TRANSPOSE_ROWS)]

    for half in range(s // gp):
        for rt in row_tiles:
            by_group = _transpose_lane_blocks([u_ref[half * gp + t8, rt, :].astype(F32)
                                               for t8 in range(gp)])
            for g in range(gp):
                ug_ref[g, rt, half * LANES:(half + 1) * LANES] = by_group[g].astype(BF16)

    for g in range(gp):
        z = _dot(ug_ref[g], p_ref[g])
        z_ref[g] = z
        zsw_ref[g] = pltpu.roll(z, S5_STATE, axis=1)

    a_r, a_i = _abar_power(ldt_ref, arr_ref, air_ref, float(s))

    def step(j, carry):
        x, x_sw = carry
        idx = pl.ds(pl.multiple_of(j * bsz, bsz), bsz)
        x_ref[:, idx, :] = x
        return (a_r * x + a_i * x_sw + z_ref[:, idx, :],
                a_r * x_sw + (-a_i) * x + zsw_ref[:, idx, :])

    start = jnp.zeros(fin_ref.shape, F32)
    fin_ref[...] = lax.fori_loop(0, n_chunks, step, (start, start))[0]

    for g in range(gp):
        yg_ref[g] = _dot(ug_ref[g], t_ref[g]) + _dot(x_ref[g].astype(BF16), q_ref[g])

    for half in range(s // gp):
        for rt in row_tiles:
            by_step = _transpose_lane_blocks([yg_ref[g, rt, half * LANES:(half + 1) * LANES]
                                              for g in range(gp)])
            for t8 in range(gp):
                t = half * gp + t8
                y_ref[t, rt, :] = jax.nn.gelu(by_step[t8] + d_ref[...] * u_ref[t, rt, :].astype(F32))


def _steps_major(a, bsz, s):
    m, width = a.shape
    return a.reshape(bsz, m // (bsz * s), s, width).transpose(2, 1, 0, 3).reshape(s, m // s, width)


def _token_major(a, bsz, s):
    m, width = a.shape
    return a.reshape(s, m // (bsz * s), bsz, width).transpose(2, 1, 0, 3).reshape(m, width)


def s5_prompt(u, bsz, length, t_mat, p_mat, q_mat, log_dt, a_re, a_im, d_skip):
    g, n, s, p2, gp = S5_GROUPS, S5_GROUP, S5_CHUNK, 2 * S5_STATE, GROUPS_PER_BLOCK
    rows = bsz * length // s
    dup = lambda a: jnp.concatenate([a, a], axis=-1)[:, None, :]
    gspec = lambda *shape: pl.BlockSpec((gp,) + shape, lambda k: (k, 0, 0))
    io_spec = pl.BlockSpec((s, rows, LANES), lambda k: (0, 0, k))
    return pl.pallas_call(
        functools.partial(_s5_prompt_kernel, bsz=bsz),
        grid=(g // gp,),
        in_specs=[io_spec,
                  gspec(s * n, s * n), gspec(s * n, p2), gspec(p2, s * n),
                  gspec(1, 1), gspec(1, p2), gspec(1, p2),
                  pl.BlockSpec((1, LANES), lambda k: (0, k))],
        out_specs=[io_spec, gspec(bsz, p2)],
        out_shape=[jax.ShapeDtypeStruct((s, rows, g * n), F32),
                   jax.ShapeDtypeStruct((g, bsz, p2), F32)],
        scratch_shapes=[pltpu.VMEM((gp, rows, s * n), BF16),
                        pltpu.VMEM((gp, rows, p2), F32),
                        pltpu.VMEM((gp, rows, p2), F32),
                        pltpu.VMEM((gp, rows, p2), F32),
                        pltpu.VMEM((gp, rows, s * n), F32)],
        compiler_params=_params(1),
        name="s5_prompt",
    )(_steps_major(u, bsz, s), t_mat, p_mat, q_mat, log_dt[:, None, None], dup(a_re), dup(a_im),
      d_skip.reshape(1, g * n))


def _s5_sample_kernel(*refs):
    for g in range(refs[0].shape[0]):
        _s5_sample_group(*[r.at[g:g + 1] for r in refs])


def _s5_sample_group(u_ref, xr_ref, xi_ref, bbt_ref, w0_ref, ldt_ref, arr_ref, air_ref, d_ref,
                     y_ref, nr_ref, ni_ref):
    p = S5_STATE
    dt = jnp.exp(ldt_ref[0])
    mag = jnp.exp(arr_ref[0][:, :p] * dt)
    ang = air_ref[0][:, :p] * dt
    abr, abi = mag * jnp.cos(ang), mag * jnp.sin(ang)
    u = u_ref[0].astype(F32)
    z = _dot(u.astype(BF16), bbt_ref[0].astype(BF16))
    xr, xi = xr_ref[0], xi_ref[0]
    new_r = abr * xr - abi * xi + z[:, :p]
    new_i = abr * xi + abi * xr + z[:, p:]
    nr_ref[0] = new_r
    ni_ref[0] = new_i
    c = w0_ref[0][:, :S5_GROUP].astype(BF16)
    y = _dot(new_r.astype(BF16), c[:p]) + _dot(new_i.astype(BF16), c[p:]) + d_ref[0] * u
    y_ref[0] = jax.nn.gelu(y)


def s5_sample(u, x_re, x_im, bbt, w0, log_dt, a_re, a_im, d_skip):
    g, n, p = S5_GROUPS, S5_GROUP, S5_STATE
    bsz = u.shape[0]
    dup = lambda a: jnp.concatenate([a, a], axis=-1)[:, None, :]
    gspec = lambda *shape: pl.BlockSpec((GROUPS_PER_BLOCK,) + shape, lambda i: (i, 0, 0))
    y, new_r, new_i = pl.pallas_call(
        _s5_sample_kernel,
        grid=(g // GROUPS_PER_BLOCK,),
        in_specs=[gspec(bsz, n), gspec(bsz, p), gspec(bsz, p), gspec(n, 2 * p),
                  gspec(2 * p, S5_CHUNK * n), gspec(1, 1), gspec(1, 2 * p), gspec(1, 2 * p),
                  gspec(1, n)],
        out_specs=[gspec(bsz, n), gspec(bsz, p), gspec(bsz, p)],
        out_shape=[jax.ShapeDtypeStruct((g, bsz, n), F32),
                   jax.ShapeDtypeStruct((g, bsz, p), F32),
                   jax.ShapeDtypeStruct((g, bsz, p), F32)],
        compiler_params=_params(1),
        name="s5_sample",
    )(u.reshape(bsz, g, n).transpose(1, 0, 2), x_re.transpose(1, 0, 2), x_im.transpose(1, 0, 2),
      bbt, w0, log_dt[:, None, None], dup(a_re), dup(a_im), d_skip.reshape(g, 1, n))
    return (y.transpose(1, 0, 2).reshape(bsz, g * n), new_r.transpose(1, 0, 2),
            new_i.transpose(1, 0, 2))


FACTORED_DECAY_LIMIT = -60.0


def _gla_kernel(q_ref, k_ref, v_ref, gr_ref, g_ref, nrm_ref, o_ref, st_ref, s_ref, b_ref,
                qb_ref, kb_ref, kend_ref, sc_ref, acc_ref):
    t = pl.program_id(0)

    @pl.when(t == 0)
    def _():
        s_ref[...] = jnp.zeros(s_ref.shape, F32)

    bsz, c = q_ref.shape[0], q_ref.shape[1]
    row = lax.broadcasted_iota(jnp.int32, (c, c), 0)
    col = lax.broadcasted_iota(jnp.int32, (c, c), 1)
    causal = col <= row
    ones_tri = causal.astype(F32)
    step = lax.broadcasted_iota(jnp.int32, (c, GLA_DK), 0)

    lowest = None
    for i in range(bsz):
        b_ref[i] = jnp.dot(ones_tri, g_ref[i], precision=HIGHEST, preferred_element_type=F32)
        end_min = jnp.min(b_ref[i, c - 1:c, :])
        lowest = end_min if lowest is None else jnp.minimum(lowest, end_min)
    factorable = lowest >= FACTORED_DECAY_LIMIT

    pairs = [(i, h) for i in range(bsz) for h in range(GLA_HEADS)]
    head_k = lambda h: slice(h * GLA_DK, (h + 1) * GLA_DK)
    head_v = lambda h: slice(h * GLA_DV, (h + 1) * GLA_DV)

    def finish(i, h, o):
        o = o * lax.rsqrt(jnp.mean(o * o, axis=-1, keepdims=True) + EPS) * nrm_ref[...]
        o_ref[i, :, head_v(h)] = (o * _silu(gr_ref[i, :, head_v(h)].astype(F32))).astype(o_ref.dtype)

    def all_chunks_factored():
        for n, (i, h) in enumerate(pairs):
            b = b_ref[i, :, head_k(h)]
            k = k_ref[i, :, head_k(h)].astype(F32)
            qb_ref[n] = (q_ref[i, :, head_k(h)].astype(F32) * (GLA_DK ** -0.5) * jnp.exp(b)).astype(BF16)
            kb_ref[n] = (k * jnp.exp(-b)).astype(BF16)
            kend_ref[n] = (k * jnp.exp(b[c - 1:c, :] - b)).astype(BF16)
        for n, (i, h) in enumerate(pairs):
            scores = lax.dot_general(qb_ref[n], kb_ref[n], NT, preferred_element_type=F32)
            sc_ref[n] = jnp.where(causal, scores, 0.0).astype(BF16)
        for n, (i, h) in enumerate(pairs):
            v = v_ref[i, :, head_v(h)].astype(BF16)
            s_t = s_ref[i, h]
            acc_ref[n] = _dot(sc_ref[n], v) + lax.dot_general(qb_ref[n], s_t.astype(BF16), NT,
                                                              preferred_element_type=F32)
            s_ref[i, h] = (jnp.exp(b_ref[i, c - 1:c, head_k(h)]) * s_t
                           + lax.dot_general(v, kend_ref[n], TN, preferred_element_type=F32))
        for n, (i, h) in enumerate(pairs):
            finish(i, h, acc_ref[n])

    def intra_stepwise(i, ks, vs, b, q_in, v):
        q = q_ref[i, :, ks].astype(F32) * (GLA_DK ** -0.5)

        def add_sources(tile, acc):
            first = pl.multiple_of(tile * ROW_TILE, ROW_TILE)
            rows = pl.ds(first, ROW_TILE)
            b_s = b_ref[i, rows, ks]
            k_s, v_s = k_ref[i, rows, ks].astype(F32), v_ref[i, rows, vs].astype(F32)
            for r in range(ROW_TILE):
                decay = jnp.exp(jnp.where(step >= first + r, b - b_s[r:r + 1], -jnp.inf))
                score = jnp.sum(q * k_s[r:r + 1] * decay, axis=1, keepdims=True)
                acc = acc + score * v_s[r:r + 1]
            return acc

        return lax.fori_loop(0, c // ROW_TILE, add_sources, jnp.zeros((c, GLA_DV), F32))

    def all_chunks_stepwise():
        for i, h in pairs:
            ks, vs = head_k(h), head_v(h)
            b = b_ref[i, :, ks]
            b_end = b[c - 1:c, :]
            q_in = (q_ref[i, :, ks].astype(F32) * (GLA_DK ** -0.5) * jnp.exp(b)).astype(BF16)
            k_end = (k_ref[i, :, ks].astype(F32) * jnp.exp(b_end - b)).astype(BF16)
            v = v_ref[i, :, vs].astype(BF16)
            s_t = s_ref[i, h]
            o = (intra_stepwise(i, ks, vs, b, q_in, v)
                 + lax.dot_general(q_in, s_t.astype(BF16), NT, preferred_element_type=F32))
            s_ref[i, h] = jnp.exp(b_end) * s_t + lax.dot_general(v, k_end, TN,
                                                                 preferred_element_type=F32)
            finish(i, h, o)

    pl.when(factorable)(all_chunks_factored)
    pl.when(jnp.logical_not(factorable))(all_chunks_stepwise)

    @pl.when(t == pl.num_programs(0) - 1)
    def _():
        st_ref[...] = s_ref[...]


def gla_prompt(proj, g, gla_norm, bsz, length):
    c = CHUNK
    nc = length // c
    hk, hv = GLA_HEADS * GLA_DK, GLA_HEADS * GLA_DV
    rows = lambda width, off: pl.BlockSpec((bsz, c, width), lambda t: (0, t, off // width))
    st_shape = (bsz, GLA_HEADS, GLA_DV, GLA_DK)
    proj3 = proj.reshape(bsz, length, proj.shape[1])
    o, st = pl.pallas_call(
        _gla_kernel,
        grid=(nc,),
        in_specs=[rows(hk, OFF_GQ), rows(hk, OFF_GK), rows(hv, OFF_GV), rows(hv, OFF_GR),
                  rows(hk, 0), pl.BlockSpec((1, GLA_DV), lambda t: (0, 0))],
        out_specs=[rows(hv, 0), pl.BlockSpec(st_shape, lambda t: (0, 0, 0, 0))],
        out_shape=[jax.ShapeDtypeStruct((bsz, length, hv), BF16),
                   jax.ShapeDtypeStruct(st_shape, F32)],
        scratch_shapes=[pltpu.VMEM(st_shape, F32),
                        pltpu.VMEM((bsz, c, hk), F32)]
        + [pltpu.VMEM((bsz * GLA_HEADS, c, GLA_DK), BF16)] * 3
        + [pltpu.VMEM((bsz * GLA_HEADS, c, c), BF16),
           pltpu.VMEM((bsz * GLA_HEADS, c, GLA_DV), F32)],
        compiler_params=_params(1),
        name="gla_prompt",
    )(proj3, proj3, proj3, proj3, g.reshape(bsz, length, hk), gla_norm.reshape(1, GLA_DV))
    return o.reshape(bsz * length, hv), st.swapaxes(2, 3)


def _pair_swap_matrix(n):
    src = lax.broadcasted_iota(jnp.int32, (n, n), 0)
    dst = lax.broadcasted_iota(jnp.int32, (n, n), 1)
    return (src == jnp.bitwise_xor(dst, 1)).astype(BF16)


def _rotate_pairs(x, cos2, sin2, swap):
    return x.astype(F32) * cos2 + _dot(x, swap) * sin2


def _ret_kernel(q_ref, k_ref, v_ref, rg_ref, cos_ref, sin_ref, dm_ref, qd_ref, kd_ref, gc_ref,
                o_ref, st_ref, s_ref, qb_ref, kb_ref, qdec_ref, kdec_ref, sc_ref, acc_ref):
    t = pl.program_id(0)

    @pl.when(t == 0)
    def _():
        s_ref[...] = jnp.zeros(s_ref.shape, F32)

    cos2, sin2, swap = cos_ref[...], sin_ref[...], _pair_swap_matrix(RET_DK)
    pairs = [(i, h) for i in range(q_ref.shape[0]) for h in range(RET_HEADS)]
    head_k = lambda h: slice(h * RET_DK, (h + 1) * RET_DK)
    head_v = lambda h: slice(h * RET_DV, (h + 1) * RET_DV)
    for n, (i, h) in enumerate(pairs):
        q = _rotate_pairs(q_ref[i, :, head_k(h)], cos2, sin2, swap)
        k = _rotate_pairs(k_ref[i, :, head_k(h)], cos2, sin2, swap) * (RET_DK ** -0.5)
        qb_ref[n] = q.astype(BF16)
        kb_ref[n] = k.astype(BF16)
        qdec_ref[n] = (q * qd_ref[h]).astype(BF16)
        kdec_ref[n] = (k * kd_ref[h]).astype(BF16)
    for n, (i, h) in enumerate(pairs):
        scores = lax.dot_general(qb_ref[n], kb_ref[n], NT, preferred_element_type=F32) * dm_ref[h]
        sc_ref[n] = scores.astype(BF16)
    for n, (i, h) in enumerate(pairs):
        v = v_ref[i, :, head_v(h)].astype(BF16)
        s_t = s_ref[i, h]
        acc_ref[n] = _dot(sc_ref[n], v) + lax.dot_general(qdec_ref[n], s_t.astype(BF16), NT,
                                                          preferred_element_type=F32)
        s_ref[i, h] = gc_ref[h] * s_t + lax.dot_general(v, kdec_ref[n], TN, preferred_element_type=F32)
    for n, (i, h) in enumerate(pairs):
        o = acc_ref[n]
        oc = o - jnp.mean(o, axis=-1, keepdims=True)
        o = oc * lax.rsqrt(jnp.mean(oc * oc, axis=-1, keepdims=True) + EPS)
        o_ref[i, :, head_v(h)] = (o * _silu(rg_ref[i, :, head_v(h)].astype(F32))).astype(o_ref.dtype)

    @pl.when(t == pl.num_programs(0) - 1)
    def _():
        st_ref[...] = s_ref[...]


def _rotary_tables(pos):
    half = RET_DK // 2
    inv = 1.0 / (ROPE_BASE ** jnp.linspace(0.0, 1.0, half, dtype=F32))
    ang = pos[:, None] * inv[None, :]
    cos2 = jnp.repeat(jnp.cos(ang), 2, axis=-1)
    sin2 = jnp.stack([-jnp.sin(ang), jnp.sin(ang)], axis=-1).reshape(pos.shape[0], RET_DK)
    return cos2, sin2


def _log_gamma():
    return jnp.log1p(-jnp.power(2.0, -5.0 - jnp.arange(RET_HEADS, dtype=F32)))


def ret_prompt(proj, bsz, length):
    c = CHUNK
    nc = length // c
    hk, hv = RET_HEADS * RET_DK, RET_HEADS * RET_DV
    cos2, sin2 = _rotary_tables(jnp.arange(length, dtype=F32))
    lg = _log_gamma()[:, None, None]
    steps = jnp.arange(c, dtype=F32)
    lag = steps[:, None] - steps[None, :]
    dmat = jnp.where(lag >= 0, jnp.exp(lg * jnp.maximum(lag, 0.0)), 0.0)
    wide = lambda a: jnp.broadcast_to(a, (RET_HEADS, a.shape[1], LANES))
    qdec = wide(jnp.exp(lg * (steps + 1.0)[None, :, None]))
    kdec = wide(jnp.exp(lg * (c - 1.0 - steps)[None, :, None]))
    gchunk = wide(jnp.exp(lg * float(c)))
    rows = lambda width, off: pl.BlockSpec((bsz, c, width), lambda t: (0, t, off // width))
    table = pl.BlockSpec((c, RET_DK), lambda t: (t, 0))
    const = lambda a: pl.BlockSpec(a.shape, lambda t: (0, 0, 0))
    st_shape = (bsz, RET_HEADS, RET_DV, RET_DK)
    proj3 = proj.reshape(bsz, length, proj.shape[1])
    o, st = pl.pallas_call(
        _ret_kernel,
        grid=(nc,),
        in_specs=[rows(hk, OFF_RQ), rows(hk, OFF_RK), rows(hv, OFF_RV), rows(hv, OFF_RG),
                  table, table, const(dmat), const(qdec), const(kdec), const(gchunk)],
        out_specs=[rows(hv, 0), pl.BlockSpec(st_shape, lambda t: (0, 0, 0, 0))],
        out_shape=[jax.ShapeDtypeStruct((bsz, length, hv), BF16),
                   jax.ShapeDtypeStruct(st_shape, F32)],
        scratch_shapes=[pltpu.VMEM(st_shape, F32)]
        + [pltpu.VMEM((bsz * RET_HEADS, c, RET_DK), BF16)] * 4
        + [pltpu.VMEM((bsz * RET_HEADS, c, c), BF16),
           pltpu.VMEM((bsz * RET_HEADS, c, RET_DV), F32)],
        compiler_params=_params(1),
        name="ret_prompt",
    )(proj3, proj3, proj3, proj3, cos2, sin2, dmat, qdec, kdec, gchunk)
    return o.reshape(bsz * length, hv), st.swapaxes(2, 3)


def _one_column(a_t, i):
    col = lax.broadcasted_iota(jnp.int32, a_t.shape, 1)
    return jnp.where(col == i, a_t, 0.0).astype(BF16)


def _gla_sample_kernel(s_ref, q_ref, kt_ref, gt_ref, v_ref, gr_ref, nrm_ref, *rest):
    so_ref, o_ref, acc_ref = rest[-3:]
    ones = jnp.ones((SAMPLE_BLOCK, GLA_DV), BF16)
    head_k = lambda h: slice(h * GLA_DK, (h + 1) * GLA_DK)
    head_v = lambda h: slice(h * GLA_DV, (h + 1) * GLA_DV)
    for h in range(GLA_HEADS):
        v = v_ref[:, head_v(h)].astype(BF16)
        k_t = kt_ref[0, h]
        decay_t = jnp.exp(gt_ref[0, h])
        decay_hi = decay_t.astype(BF16).astype(F32)
        decay_lo = decay_t - decay_hi
        for i in range(SAMPLE_BLOCK):
            decay = _dot(_one_column(decay_hi, i), ones) + _dot(_one_column(decay_lo, i), ones)
            so_ref[i, h] = decay * s_ref[i, h] + _dot(_one_column(k_t, i), v)
    for h in range(GLA_HEADS):
        q = (q_ref[:, head_k(h)].astype(F32) * (GLA_DK ** -0.5)).astype(BF16)
        for i in range(SAMPLE_BLOCK):
            acc_ref[i:i + 1, head_v(h)] = _dot(q, so_ref[i, h].astype(BF16))[i:i + 1]
    for h in range(GLA_HEADS):
        o = acc_ref[:, head_v(h)]
        o = o * lax.rsqrt(jnp.mean(o * o, axis=-1, keepdims=True) + EPS) * nrm_ref[...]
        o_ref[:, head_v(h)] = (o * _silu(gr_ref[:, head_v(h)].astype(F32))).astype(o_ref.dtype)


def _columns(a, heads, dk):
    bsz = a.shape[0]
    return a.reshape(bsz // SAMPLE_BLOCK, SAMPLE_BLOCK, heads, dk).transpose(0, 2, 3, 1)


def _stacked_state_call(kernel_fn, name, state, layer, prev, heads, dk, dv, in_specs, operands,
                        o_spec, o_shape):
    bb = SAMPLE_BLOCK
    st_spec = pl.BlockSpec((None, bb, heads, dk, dv), lambda i: (layer, i, 0, 0, 0))
    aliases = {}
    if prev is not None:
        in_specs = in_specs + [pl.BlockSpec(memory_space=pl.ANY)]
        operands = operands + (prev,)
        aliases = {len(in_specs): 0}
    return pl.pallas_call(
        kernel_fn,
        grid=(state.shape[1] // bb,),
        in_specs=[st_spec] + in_specs,
        out_specs=[st_spec, o_spec],
        out_shape=[jax.ShapeDtypeStruct(state.shape, F32), o_shape],
        scratch_shapes=[pltpu.VMEM((bb, heads * dv), F32)],
        input_output_aliases=aliases,
        compiler_params=_params(1),
        name=name,
    )(state, *operands)


def gla_sample(proj, g, state, gla_norm, layer, prev):
    bsz = proj.shape[0]
    bb = SAMPLE_BLOCK
    hk, hv = GLA_HEADS * GLA_DK, GLA_HEADS * GLA_DV
    col_spec = pl.BlockSpec((1, GLA_HEADS, GLA_DK, bb), lambda i: (i, 0, 0, 0))
    rows = lambda off: pl.BlockSpec((bb, hv), lambda i: (i, off // hv))
    return _stacked_state_call(
        _gla_sample_kernel, "gla_sample", state, layer, prev, GLA_HEADS, GLA_DK, GLA_DV,
        [pl.BlockSpec((bb, hk), lambda i: (i, OFF_GQ // hk)), col_spec, col_spec, rows(OFF_GV),
         rows(OFF_GR), pl.BlockSpec((None, 1, GLA_DV), lambda i: (layer, 0, 0))],
        (proj, _columns(proj[:, OFF_GK:OFF_GK + hk], GLA_HEADS, GLA_DK), _columns(g, GLA_HEADS, GLA_DK),
         proj, proj, gla_norm.reshape(gla_norm.shape[0], 1, GLA_DV)),
        rows(0), jax.ShapeDtypeStruct((bsz, hv), BF16))


def _rot_kernel(q_ref, k_ref, cos_ref, sin_ref, qo_ref, ko_ref):
    cos2, sin2, swap = cos_ref[...], sin_ref[...], _pair_swap_matrix(RET_DK)
    for h in range(RET_HEADS):
        ks = slice(h * RET_DK, (h + 1) * RET_DK)
        qo_ref[:, ks] = _rotate_pairs(q_ref[:, ks], cos2, sin2, swap)
        ko_ref[:, ks] = _rotate_pairs(k_ref[:, ks], cos2, sin2, swap) * (RET_DK ** -0.5)


def _ret_sample_kernel(s_ref, q_ref, kt_ref, gam_ref, v_ref, rg_ref, *rest):
    so_ref, o_ref, acc_ref = rest[-3:]
    head_k = lambda h: slice(h * RET_DK, (h + 1) * RET_DK)
    head_v = lambda h: slice(h * RET_DV, (h + 1) * RET_DV)
    for h in range(RET_HEADS):
        v = v_ref[:, head_v(h)].astype(BF16)
        k_t = kt_ref[0, h]
        for i in range(SAMPLE_BLOCK):
            so_ref[i, h] = gam_ref[h] * s_ref[i, h] + _dot(_one_column(k_t, i), v)
    for h in range(RET_HEADS):
        q = q_ref[:, head_k(h)].astype(BF16)
        for i in range(SAMPLE_BLOCK):
            acc_ref[i:i + 1, head_v(h)] = _dot(q, so_ref[i, h].astype(BF16))[i:i + 1]
    for h in range(RET_HEADS):
        o = acc_ref[:, head_v(h)]
        oc = o - jnp.mean(o, axis=-1, keepdims=True)
        o = oc * lax.rsqrt(jnp.mean(oc * oc, axis=-1, keepdims=True) + EPS)
        o_ref[:, head_v(h)] = (o * _silu(rg_ref[:, head_v(h)].astype(F32))).astype(o_ref.dtype)


def ret_sample(proj, state, layer, prev):
    bsz = proj.shape[0]
    bb = SAMPLE_BLOCK
    hk, hv = RET_HEADS * RET_DK, RET_HEADS * RET_DV
    cos2, sin2 = _rotary_tables(PAST_LEN + jnp.arange(1, dtype=F32))
    full = pl.BlockSpec((bsz, hk), lambda i: (0, 0))
    q_rot, k_rot = pl.pallas_call(
        _rot_kernel,
        grid=(1,),
        in_specs=[pl.BlockSpec((bsz, hk), lambda i: (0, OFF_RQ // hk)),
                  pl.BlockSpec((bsz, hk), lambda i: (0, OFF_RK // hk)),
                  pl.BlockSpec((1, RET_DK), lambda i: (0, 0)), pl.BlockSpec((1, RET_DK), lambda i: (0, 0))],
        out_specs=[full, full],
        out_shape=[jax.ShapeDtypeStruct((bsz, hk), F32)] * 2,
        compiler_params=_params(1),
        name="ret_rotary",
    )(proj, proj, cos2, sin2)
    gamma = jnp.broadcast_to(jnp.exp(_log_gamma())[:, None, None], (RET_HEADS, 1, LANES))
    col_spec = pl.BlockSpec((1, RET_HEADS, RET_DK, bb), lambda i: (i, 0, 0, 0))
    rows = lambda off: pl.BlockSpec((bb, hv), lambda i: (i, off // hv))
    return _stacked_state_call(
        _ret_sample_kernel, "ret_sample", state, layer, prev, RET_HEADS, RET_DK, RET_DV,
        [rows(0), col_spec, pl.BlockSpec(gamma.shape, lambda i: (0, 0, 0)), rows(OFF_RV), rows(OFF_RG)],
        (q_rot, _columns(k_rot, RET_HEADS, RET_DK), gamma, proj, proj),
        rows(0), jax.ShapeDtypeStruct((bsz, hv), BF16))


def _layer(xp, xs, layer, prompt_shape, states, w, s5_mats):
    t_mat, p_mat, q_mat, bbt, w0 = s5_mats
    bsz, length = prompt_shape
    mp = xp.shape[0]
    tm, tm_wide, tn = min(mp, 512), min(mp, 1024), 512
    s5_par = (w["s5_log_dt"][layer], w["s5_a_re"][layer], w["s5_a_im"][layer], w["s5_d"][layer])
    st_s5_re, st_s5_im, st_gla, st_ret, prev_gla, prev_ret = states

    gate_w = (w["w_in_rows"], w["gla_w_gate"], w["gla_b_gate"], layer)
    hp, g_p = norm_and_gate(xp, w["norm_mix"][layer], *gate_w)
    hs, g_s = norm_and_gate(xs, w["norm_mix"][layer], *gate_w)
    proj_p, proj_s = in_proj(hp, hs, w["w_in_rows"], layer, tm_wide, 2 * tn)

    y5_p, x_fin = s5_prompt(proj_p[:, OFF_U:OFF_U + S5_WIDTH], bsz, length, t_mat, p_mat, q_mat, *s5_par)
    y5_p = y5_p.reshape(mp, S5_WIDTH)
    x_fin = x_fin.transpose(1, 0, 2)
    o_gla_p, new_gla_p = gla_prompt(proj_p, g_p, w["gla_norm"][layer], bsz, length)
    o_ret_p, new_ret_p = ret_prompt(proj_p, bsz, length)

    y5_s, new_s5_re_s, new_s5_im_s = s5_sample(proj_s[:, OFF_U:OFF_U + S5_WIDTH], st_s5_re[layer],
                                               st_s5_im[layer], bbt, w0, *s5_par)
    new_gla_s, o_gla_s = gla_sample(proj_s, g_s, st_gla, w["gla_norm"], layer, prev_gla)
    new_ret_s, o_ret_s = ret_sample(proj_s, st_ret, layer, prev_ret)

    a5_p, a5_s = s5_glu([y5_p, y5_s], w["s5_w_glu"], w["s5_b_glu"], layer, tm, 2 * tn)
    a5_p = _token_major(a5_p, bsz, S5_CHUNK)
    merged = merge_branches([(a5_p, o_gla_p, o_ret_p), (a5_s, o_gla_s, o_ret_s)], [proj_p, proj_s],
                            w["s5_w_out"], w["gla_w_out"], w["ret_w_out"], layer, tm, 2 * tn)
    xp, xs = residual_matmul(merged, w["w_mix_out"], [xp, xs], layer, tm_wide, 2 * tn)
    h2 = [rmsnorm(xp, w["norm_ffn"][layer], BF16), rmsnorm(xs, w["norm_ffn"][layer], BF16)]
    ff = ffn_up(h2, w["w_ffn_gate"], w["w_ffn_up"], layer, tm_wide, tn)
    xp, xs = residual_matmul(ff, w["w_ffn_down"], [xp, xs], layer, tm_wide, tn)
    new_p = (x_fin[..., :S5_STATE], x_fin[..., S5_STATE:], new_gla_p, new_ret_p)
    return xp, xs, new_p, (new_s5_re_s, new_s5_im_s, new_gla_s, new_ret_s)


def kernel(x_prompt, x_sample, state_s5_re, state_s5_im, state_gla, state_ret, norm_mix, w_in, s5_a_re, s5_a_im, s5_log_dt, s5_b_re, s5_b_im, s5_c_re, s5_c_im, s5_d, s5_w_glu, s5_b_glu, s5_w_out, gla_w_gate, gla_b_gate, gla_norm, gla_w_out, ret_w_out, w_mix_out, norm_ffn, w_ffn_gate, w_ffn_up, w_ffn_down, norm_final):
    bp, lp, d = x_prompt.shape
    bs = x_sample.shape[0]
    depth = w_in.shape[0]
    hp = x_prompt.reshape(bp * lp, d)
    hs = x_sample.reshape(bs, d)
    w = dict(norm_mix=norm_mix, s5_log_dt=s5_log_dt, s5_a_re=s5_a_re, s5_a_im=s5_a_im, s5_d=s5_d,
             s5_w_glu=s5_w_glu, s5_b_glu=s5_b_glu, s5_w_out=s5_w_out, gla_w_gate=gla_w_gate,
             gla_b_gate=gla_b_gate, gla_norm=gla_norm, gla_w_out=gla_w_out, ret_w_out=ret_w_out,
             w_mix_out=w_mix_out, norm_ffn=norm_ffn, w_ffn_gate=w_ffn_gate, w_ffn_up=w_ffn_up,
             w_ffn_down=w_ffn_down,
             w_in_rows=jnp.swapaxes(w_in, 1, 2).reshape(depth, w_in.shape[2] // GLA_RANK, GLA_RANK, d))
    outs_p, outs_s = [], []
    new_gla_s = new_ret_s = None
    for l in range(depth):
        s5_mats = s5_prep(s5_a_re[l], s5_a_im[l], s5_log_dt[l], s5_b_re[l], s5_b_im[l], s5_c_re[l],
                          s5_c_im[l])
        hp, hs, new_p, (s5r, s5i, new_gla_s, new_ret_s) = _layer(
            hp, hs, l, (bp, lp), (state_s5_re, state_s5_im, state_gla, state_ret, new_gla_s, new_ret_s),
            w, s5_mats)
        outs_p.append(new_p)
        outs_s.append((s5r, s5i))
    y_prompt = rmsnorm(hp, norm_final, F32).reshape(bp, lp, d)
    y_sample = rmsnorm(hs, norm_final, F32).reshape(bs, 1, d)
    stack = lambda outs, i: jnp.stack([o[i] for o in outs])
    return (y_prompt, y_sample,
            stack(outs_p, 0), stack(outs_p, 1), stack(outs_p, 2), stack(outs_p, 3),
            stack(outs_s, 0), stack(outs_s, 1), new_gla_s, new_ret_s)
```
